```python
import math
import jax, jax.numpy as jnp
from jax import lax
import numpy as np

D_MODEL = 2048
BATCH = 4
SEQ = 2048
DEPTH = 2
DEC_BATCH = 128
DEC_SEQ = 4
PAST_LEN = 16384
PAGE_SIZE = 128

N_MIXERS = 2
N_GDN_LAYERS = (DEPTH + 1) // 2
N_SCONV_LAYERS = DEPTH // 2
GDN_HEAD_K = 128
GDN_HEAD_V = 128
GDN_K_HEADS = D_MODEL // GDN_HEAD_K
GDN_V_HEADS = 2 * GDN_K_HEADS
GDN_K_DIM = GDN_K_HEADS * GDN_HEAD_K
GDN_V_DIM = GDN_V_HEADS * GDN_HEAD_V
GDN_CONV_DIM = 2 * GDN_K_DIM + GDN_V_DIM
GDN_IN_DIM = GDN_CONV_DIM + GDN_V_DIM + 2 * GDN_V_HEADS
GDN_CONV = 4
GDN_CHUNK = 64
SCONV_WIDTH = 3
FFN_DIM = 5632
N_EXPERTS = 8
TOP_K = 2
EXPERT_DIM = 7 * D_MODEL // 2
NORM_EPS = 1e-6

kernel_name = 'hybrid_gdn_shortconv_moe_step'


def _rmsnorm(x, gain):
    xf = x.astype(jnp.float32)
    y = xf * lax.rsqrt(jnp.mean(xf * xf, axis=-1, keepdims=True) + NORM_EPS)
    return (y * gain.astype(jnp.float32)).astype(x.dtype)


def _l2norm(x):
    xf = x.astype(jnp.float32)
    return xf * lax.rsqrt(jnp.sum(xf * xf, axis=-1, keepdims=True) + NORM_EPS)


def _modulation(c, w_ada, b_ada):
    m = jax.nn.silu(c) @ w_ada + b_ada
    return jnp.split(m[:, None, :], 6, axis=-1)


def _causal_dwconv(u, buf, w):
    width = w.shape[0]
    t = u.shape[1]
    full = jnp.concatenate([buf.astype(u.dtype), u], axis=1)
    y = full[:, 0:t] * w[0]
    for j in range(1, width):
        y = y + full[:, j:j + t] * w[j]
    return y, full[:, t:]


def _gated_delta_rule(q, k, v, beta, g, s0):
    b, t, h, dk = q.shape
    dv = v.shape[-1]
    c = min(GDN_CHUNK, t)
    n = -(-t // c)
    pad = n * c - t

    def blocks(a):
        a = jnp.pad(a.astype(jnp.float32), [(0, 0), (0, pad)] + [(0, 0)] * (a.ndim - 2))
        a = jnp.moveaxis(a, 1, 2)
        return a.reshape((b, h, n, c) + a.shape[3:])

    q = blocks(q) * (dk ** -0.5)
    k, v, beta, g = blocks(k), blocks(v), blocks(beta), blocks(g)
    g = jnp.cumsum(g, axis=-1)
    causal = jnp.tril(jnp.ones((c, c), dtype=bool))
    decay = jnp.exp(jnp.where(causal, g[..., :, None] - g[..., None, :], -jnp.inf))
    kb = k * beta[..., None]
    eye = jnp.eye(c, dtype=jnp.float32)
    lower = jnp.einsum('bhnid,bhnjd->bhnij', kb, k) * decay * (1.0 - eye)
    tinv = lax.linalg.triangular_solve(eye + lower, jnp.broadcast_to(eye, lower.shape),
                                       left_side=True, lower=True, unit_diagonal=True)
    u = jnp.einsum('bhnij,bhnjd->bhnid', tinv, v * beta[..., None])
    w = jnp.einsum('bhnij,bhnjd->bhnid', tinv, kb * jnp.exp(g)[..., None])
    a_intra = jnp.einsum('bhnid,bhnjd->bhnij', q, k) * decay
    q_dec = q * jnp.exp(g)[..., None]
    k_dec = k * jnp.exp(g[..., -1:] - g)[..., None]
    g_tot = jnp.exp(g[..., -1])

    def step(s, xs):
        q_i, k_i, u_i, w_i, a_i, gt_i = xs
        v_new = u_i - jnp.einsum('bhcd,bhde->bhce', w_i, s)
        o_i = jnp.einsum('bhcd,bhde->bhce', q_i, s) + jnp.einsum('bhij,bhje->bhie', a_i, v_new)
        s = s * gt_i[..., None, None] + jnp.einsum('bhcd,bhce->bhde', k_i, v_new)
        return s, o_i

    xs = tuple(jnp.moveaxis(a, 2, 0) for a in (q_dec, k_dec, u, w, a_intra, g_tot))
    s_fin, o = lax.scan(step, s0.astype(jnp.float32), xs)
    o = jnp.transpose(o, (1, 0, 3, 2, 4)).reshape(b, n * c, h, dv)[:, :t]
    return o, s_fin


def _gdn_mixer(h, s0, conv_buf, w_in, conv_w, a_log, dt_bias, g_onorm, w_out):
    b, t, _ = h.shape
    proj = h @ w_in
    qkv, z, bt, a = jnp.split(proj, [GDN_CONV_DIM, GDN_CONV_DIM + GDN_V_DIM,
                                     GDN_CONV_DIM + GDN_V_DIM + GDN_V_HEADS], axis=-1)
    qkv, new_buf = _causal_dwconv(qkv, conv_buf, conv_w)
    qkv = jax.nn.silu(qkv)
    q, k, v = jnp.split(qkv, [GDN_K_DIM, 2 * GDN_K_DIM], axis=-1)
    rep = GDN_V_HEADS // GDN_K_HEADS
    q = jnp.repeat(_l2norm(q.reshape(b, t, GDN_K_HEADS, GDN_HEAD_K)), rep, axis=2)
    k = jnp.repeat(_l2norm(k.reshape(b, t, GDN_K_HEADS, GDN_HEAD_K)), rep, axis=2)
    v = v.reshape(b, t, GDN_V_HEADS, GDN_HEAD_V)
    beta = jax.nn.sigmoid(bt.astype(jnp.float32))
    g = -jnp.exp(a_log.astype(jnp.float32)) * jax.nn.softplus(a.astype(jnp.float32) + dt_bias.astype(jnp.float32))
    o, s_new = _gated_delta_rule(q, k, v, beta, g, s0)
    zf = z.reshape(b, t, GDN_V_HEADS, GDN_HEAD_V).astype(jnp.float32)
    o = _rmsnorm(o, g_onorm) * jax.nn.silu(zf)
    y = o.reshape(b, t, GDN_V_DIM).astype(h.dtype) @ w_out
    return y, s_new.astype(s0.dtype), new_buf.astype(conv_buf.dtype)


def _shortconv_mixer(h, buf, w_in, conv_w, w_out):
    bg, cg, xin = jnp.split(h @ w_in, 3, axis=-1)
    y, new_buf = _causal_dwconv(cg * xin, buf, conv_w)
    return (bg * y) @ w_out, new_buf.astype(buf.dtype)


def _swiglu(h, w_up, w_down):
    gate, up = jnp.split(h @ w_up, 2, axis=-1)
    return (jax.nn.silu(gate) * up) @ w_down


def _moe(h, w_router, b_router, w_up, w_down):
    logits = (h @ w_router).astype(jnp.float32) + b_router.astype(jnp.float32)
    top_v, top_i = lax.top_k(logits, TOP_K)
    gates = jax.nn.softmax(top_v, axis=-1)
    dense_gate = jnp.einsum('btk,btke->bte', gates, jax.nn.one_hot(top_i, N_EXPERTS, dtype=jnp.float32))
    out = jnp.zeros(h.shape, jnp.float32)
    for e in range(N_EXPERTS):
        out = out + dense_gate[..., e:e + 1] * _swiglu(h, w_up[e], w_down[e]).astype(jnp.float32)
    return out.astype(h.dtype)


def _trunk(x, c, s_gdn, s_gconv, s_sconv, w_ada, b_ada, g_norm_mix, g_norm_ffn, g_norm_out,
           gdn_w_in, gdn_conv_w, gdn_a_log, gdn_dt_bias, gdn_g_onorm, gdn_w_out,
           sc_w_in, sc_conv_w, sc_w_out, ffn_w_up, ffn_w_down,
           moe_w_router, moe_b_router, moe_w_up, moe_w_down):
    new_gdn, new_gconv, new_sconv = [], [], []
    for i in range(DEPTH):
        j = i // N_MIXERS
        sh1, sc1, ga1, sh2, sc2, ga2 = _modulation(c, w_ada[i], b_ada[i])
        h = _rmsnorm(x, g_norm_mix[i]) * (1.0 + sc1) + sh1
        if i % N_MIXERS == 0:
            y, s_new, cb_new = _gdn_mixer(h, s_gdn[j], s_gconv[j], gdn_w_in[j], gdn_conv_w[j],
                                          gdn_a_log[j], gdn_dt_bias[j], gdn_g_onorm[j], gdn_w_out[j])
            new_gdn.append(s_new)
            new_gconv.append(cb_new)
        else:
            y, cb_new = _shortconv_mixer(h, s_sconv[j], sc_w_in[j], sc_conv_w[j], sc_w_out[j])
            new_sconv.append(cb_new)
        x = x + ga1 * y
        h = _rmsnorm(x, g_norm_ffn[i]) * (1.0 + sc2) + sh2
        if i % 2 == 0:
            f = _swiglu(h, ffn_w_up[j], ffn_w_down[j])
        else:
            f = _moe(h, moe_w_router[j], moe_b_router[j], moe_w_up[j], moe_w_down[j])
        x = x + ga2 * f
    return _rmsnorm(x, g_norm_out), jnp.stack(new_gdn), jnp.stack(new_gconv), jnp.stack(new_sconv)


def setup_inputs(seed: int = 0) -> dict:
    key = jax.random.key(seed)
    ks = iter(jax.random.split(key, 40))
    f32 = jnp.float32

    def nrm(shape, scale):
        return jax.random.normal(next(ks), shape, f32) * scale

    d = D_MODEL
    x_prompt = nrm((BATCH, SEQ, d), 1.0)
    x_sample = nrm((DEC_BATCH, DEC_SEQ, d), 1.0)
    c_prompt = nrm((BATCH, d), 1.0)
    c_sample = nrm((DEC_BATCH, d), 1.0)
    state_gdn = nrm((N_GDN_LAYERS, DEC_BATCH, GDN_V_HEADS, GDN_HEAD_K, GDN_HEAD_V), 0.1)
    state_gdn_conv = nrm((N_GDN_LAYERS, DEC_BATCH, GDN_CONV - 1, GDN_CONV_DIM), 1.0)
    state_sconv = nrm((N_SCONV_LAYERS, DEC_BATCH, SCONV_WIDTH - 1, d), 1.0)
    w_ada = nrm((DEPTH, d, 6 * d), 0.5 * d ** -0.5)
    b_ada = nrm((DEPTH, 6 * d), 0.02)
    g_norm_mix = 1.0 + nrm((DEPTH, d), 0.02)
    g_norm_ffn = 1.0 + nrm((DEPTH, d), 0.02)
    g_norm_out = 1.0 + nrm((d,), 0.02)
    gdn_w_in = nrm((N_GDN_LAYERS, d, GDN_IN_DIM), d ** -0.5)
    gdn_conv_w = nrm((N_GDN_LAYERS, GDN_CONV, GDN_CONV_DIM), GDN_CONV ** -0.5)
    gdn_a_log = jnp.log(jax.random.uniform(next(ks), (N_GDN_LAYERS, GDN_V_HEADS), f32, 1.0, 16.0))
    dt = jnp.exp(jax.random.uniform(next(ks), (N_GDN_LAYERS, GDN_V_HEADS), f32,
                                    math.log(1e-3), math.log(1e-1)))
    gdn_dt_bias = dt + jnp.log(-jnp.expm1(-dt))
    gdn_g_onorm = 1.0 + nrm((N_GDN_LAYERS, GDN_HEAD_V), 0.02)
    gdn_w_out = nrm((N_GDN_LAYERS, GDN_V_DIM, d), GDN_V_DIM ** -0.5)
    sc_w_in = nrm((N_SCONV_LAYERS, d, 3 * d), d ** -0.5)
    sc_conv_w = nrm((N_SCONV_LAYERS, SCONV_WIDTH, d), SCONV_WIDTH ** -0.5)
    sc_w_out = nrm((N_SCONV_LAYERS, d, d), d ** -0.5)
    ffn_w_up = nrm((N_GDN_LAYERS, d, 2 * FFN_DIM), d ** -0.5)
    ffn_w_down = nrm((N_GDN_LAYERS, FFN_DIM, d), FFN_DIM ** -0.5)
    moe_w_router = nrm((N_SCONV_LAYERS, d, N_EXPERTS), d ** -0.5)
    moe_b_router = nrm((N_SCONV_LAYERS, N_EXPERTS), 0.01)
    moe_w_up = nrm((N_SCONV_LAYERS, N_EXPERTS, d, 2 * EXPERT_DIM), d ** -0.5)
    moe_w_down = nrm((N_SCONV_LAYERS, N_EXPERTS, EXPERT_DIM, d), EXPERT_DIM ** -0.5)
    return {'x_prompt': x_prompt, 'x_sample': x_sample, 'c_prompt': c_prompt, 'c_sample': c_sample,
            'state_gdn': state_gdn, 'state_gdn_conv': state_gdn_conv, 'state_sconv': state_sconv,
            'w_ada': w_ada, 'b_ada': b_ada, 'g_norm_mix': g_norm_mix, 'g_norm_ffn': g_norm_ffn,
            'g_norm_out': g_norm_out, 'gdn_w_in': gdn_w_in, 'gdn_conv_w': gdn_conv_w,
            'gdn_a_log': gdn_a_log, 'gdn_dt_bias': gdn_dt_bias, 'gdn_g_onorm': gdn_g_onorm,
            'gdn_w_out': gdn_w_out, 'sc_w_in': sc_w_in, 'sc_conv_w': sc_conv_w, 'sc_w_out': sc_w_out,
            'ffn_w_up': ffn_w_up, 'ffn_w_down': ffn_w_down, 'moe_w_router': moe_w_router,
            'moe_b_router': moe_b_router, 'moe_w_up': moe_w_up, 'moe_w_down': moe_w_down}


def reference(x_prompt, x_sample, c_prompt, c_sample, state_gdn, state_gdn_conv, state_sconv,
              w_ada, b_ada, g_norm_mix, g_norm_ffn, g_norm_out, gdn_w_in, gdn_conv_w,
              gdn_a_log, gdn_dt_bias, gdn_g_onorm, gdn_w_out, sc_w_in, sc_conv_w, sc_w_out,
              ffn_w_up, ffn_w_down, moe_w_router, moe_b_router, moe_w_up, moe_w_down):
    bp = x_prompt.shape[0]
    p_gdn = jnp.zeros((state_gdn.shape[0], bp) + state_gdn.shape[2:], state_gdn.dtype)
    p_gconv = jnp.zeros((state_gdn_conv.shape[0], bp) + state_gdn_conv.shape[2:], state_gdn_conv.dtype)
    p_sconv = jnp.zeros((state_sconv.shape[0], bp) + state_sconv.shape[2:], state_sconv.dtype)
    y_prompt, gdn_p, gconv_p, sconv_p = _trunk(
        x_prompt, c_prompt, p_gdn, p_gconv, p_sconv, w_ada, b_ada, g_norm_mix, g_norm_ffn, g_norm_out,
        gdn_w_in, gdn_conv_w, gdn_a_log, gdn_dt_bias, gdn_g_onorm, gdn_w_out,
        sc_w_in, sc_conv_w, sc_w_out, ffn_w_up, ffn_w_down,
        moe_w_router, moe_b_router, moe_w_up, moe_w_down)
    y_sample, gdn_s, gconv_s, sconv_s = _trunk(
        x_sample, c_sample, state_gdn, state_gdn_conv, state_sconv, w_ada, b_ada, g_norm_mix, g_norm_ffn,
        g_norm_out, gdn_w_in, gdn_conv_w, gdn_a_log, gdn_dt_bias, gdn_g_onorm, gdn_w_out,
        sc_w_in, sc_conv_w, sc_w_out, ffn_w_up, ffn_w_down,
        moe_w_router, moe_b_router, moe_w_up, moe_w_down)
    return (y_prompt, y_sample, gdn_p, gconv_p, sconv_p, gdn_s, gconv_s, sconv_s)
```

```python
import functools

import jax
import jax.numpy as jnp
from jax import lax
from jax.experimental import pallas as pl
from jax.experimental.pallas import tpu as pltpu

F32 = jnp.float32
BF16 = jnp.bfloat16
NORM_EPS = 1e-6
HEAD = 128
V7X_VMEM_LIMIT = 56 * 1024 * 1024
HIGHEST = lax.Precision.HIGHEST


def _cparams(*sem):
    return pltpu.CompilerParams(dimension_semantics=sem, vmem_limit_bytes=V7X_VMEM_LIMIT)


def _tile(n, pref):
    return pref if n % pref == 0 else n


def _silu(x):
    return x * jax.nn.sigmoid(x)


def _softplus(x):
    return jnp.maximum(x, 0.0) + jnp.log1p(jnp.exp(-jnp.abs(x)))


def _blk(idx, size):
    assert size & (size - 1) == 0
    return lax.shift_right_logical(idx, size.bit_length() - 1)


def _dot(a, b):
    return jnp.dot(a.astype(BF16), b.astype(BF16), preferred_element_type=F32)


def _dot_nt(a, b):
    return lax.dot_general(a.astype(BF16), b.astype(BF16), (((1,), (1,)), ((), ())), preferred_element_type=F32)


def _dot_tn(a, b):
    return lax.dot_general(a.astype(BF16), b.astype(BF16), (((0,), (0,)), ((), ())), preferred_element_type=F32)


def _ada_kernel(c_ref, w_ref, b_ref, o_ref):
    c = c_ref[...]
    o_ref[0] = _dot(_silu(c), w_ref[0]) + b_ref[0]


def _ada(c_all, w_ada, b_ada):
    nl, d, n6 = w_ada.shape
    r = c_all.shape[0]
    tn = _tile(n6, 1024)
    return pl.pallas_call(
        _ada_kernel,
        grid=(nl, n6 // tn),
        in_specs=[pl.BlockSpec((r, d), lambda l, j: (0, 0)),
                  pl.BlockSpec((1, d, tn), lambda l, j: (l, 0, j)),
                  pl.BlockSpec((1, 1, tn), lambda l, j: (l, 0, j))],
        out_specs=pl.BlockSpec((1, r, tn), lambda l, j: (l, 0, j)),
        out_shape=jax.ShapeDtypeStruct((nl, r, n6), F32),
        compiler_params=_cparams("parallel", "parallel"),
        name="ada_mod",
    )(c_all, w_ada, b_ada.reshape(nl, 1, n6))


class _Mod:
    def __init__(self, mod_p, mod_s, d, tm, n_ptiles, tiles_per_seq):
        self.arrays = (mod_p, mod_s)
        self.n_ptiles = n_ptiles
        nb = mod_p.shape[0]

        def specs(chunk):
            return [pl.BlockSpec((1, 1, d), lambda i: (jnp.minimum(i // tiles_per_seq, nb - 1), 0, chunk)),
                    pl.BlockSpec((tm, d), lambda i: (0, chunk))]
        self.specs = specs

    def pick(self, p_ref, s_ref):
        is_prompt = pl.program_id(0) < self.n_ptiles
        return jnp.where(is_prompt, p_ref[0], s_ref[...])


def _rms(x, gain):
    ms = jnp.mean(x * x, axis=-1, keepdims=True)
    return x * lax.rsqrt(ms + NORM_EPS) * gain


def _prep_kernel(mod, x_ref, g_ref, scp, scs, shp, shs, h_ref):
    y = _rms(x_ref[...], g_ref[...])
    h_ref[...] = (y * (1.0 + mod.pick(scp, scs)) + mod.pick(shp, shs)).astype(h_ref.dtype)


def _prep(x, gain, mod, sc_chunk, sh_chunk, tm):
    n, d = x.shape
    row = pl.BlockSpec((tm, d), lambda i: (i, 0))
    return pl.pallas_call(
        functools.partial(_prep_kernel, mod),
        grid=(n // tm,),
        in_specs=[row, pl.BlockSpec((1, d), lambda i: (0, 0))] + mod.specs(sc_chunk) + mod.specs(sh_chunk),
        out_specs=row,
        out_shape=jax.ShapeDtypeStruct((n, d), BF16),
        compiler_params=_cparams("parallel"),
        name="norm_mod",
    )(x, gain.reshape(1, d), *mod.arrays, *mod.arrays)


def _resid_prep_kernel(mod, x_ref, y_ref, g_ref, gap, gas, scp, scs, shp, shs, xo_ref, h_ref):
    x = x_ref[...] + mod.pick(gap, gas) * y_ref[...]
    xo_ref[...] = x
    y = _rms(x, g_ref[...])
    h_ref[...] = (y * (1.0 + mod.pick(scp, scs)) + mod.pick(shp, shs)).astype(h_ref.dtype)


def _resid_prep(x, y, gain, mod_gate, ga_chunk, mod, sc_chunk, sh_chunk, tm):
    n, d = x.shape
    row = pl.BlockSpec((tm, d), lambda i: (i, 0))
    return pl.pallas_call(
        functools.partial(_resid_prep_kernel, mod),
        grid=(n // tm,),
        in_specs=[row, row, pl.BlockSpec((1, d), lambda i: (0, 0))]
        + mod_gate.specs(ga_chunk) + mod.specs(sc_chunk) + mod.specs(sh_chunk),
        out_specs=[row, row],
        out_shape=[jax.ShapeDtypeStruct((n, d), F32), jax.ShapeDtypeStruct((n, d), BF16)],
        compiler_params=_cparams("parallel"),
        name="resid_norm_mod",
    )(x, y, gain.reshape(1, d), *mod_gate.arrays, *mod.arrays, *mod.arrays)


def _resid_route_kernel(mod, n_exp, x_ref, y_ref, g_ref, gap, gas, scp, scs, shp, shs, wr_ref, br_ref,
                        xo_ref, h_ref, gate_ref):
    x = x_ref[...] + mod.pick(gap, gas) * y_ref[...]
    xo_ref[...] = x
    h = _rms(x, g_ref[...]) * (1.0 + mod.pick(scp, scs)) + mod.pick(shp, shs)
    h_ref[...] = h.astype(h_ref.dtype)
    logits = jnp.dot(h, wr_ref[...], precision=HIGHEST, preferred_element_type=F32) + br_ref[...]
    lane = lax.broadcasted_iota(jnp.int32, logits.shape, 1).astype(F32)
    m1 = jnp.max(logits, axis=-1, keepdims=True)
    i1 = jnp.min(jnp.where(logits == m1, lane, float(n_exp)), axis=-1, keepdims=True)
    rest = jnp.where(lane == i1, -jnp.inf, logits)
    m2 = jnp.max(rest, axis=-1, keepdims=True)
    i2 = jnp.min(jnp.where(rest == m2, lane, float(n_exp)), axis=-1, keepdims=True)
    e2 = jnp.exp(m2 - m1)
    g1 = 1.0 / (1.0 + e2)
    g2 = e2 / (1.0 + e2)
    gate_ref[...] = jnp.where(lane == i1, g1, 0.0) + jnp.where(lane == i2, g2, 0.0)


def _resid_route(x, y, gain, mod_gate, ga_chunk, mod, sc_chunk, sh_chunk, w_router, b_router, tm):
    n, d = x.shape
    n_exp = w_router.shape[1]
    row = pl.BlockSpec((tm, d), lambda i: (i, 0))
    return pl.pallas_call(
        functools.partial(_resid_route_kernel, mod, n_exp),
        grid=(n // tm,),
        in_specs=[row, row, pl.BlockSpec((1, d), lambda i: (0, 0))]
        + mod_gate.specs(ga_chunk) + mod.specs(sc_chunk) + mod.specs(sh_chunk)
        + [pl.BlockSpec((d, n_exp), lambda i: (0, 0)), pl.BlockSpec((1, n_exp), lambda i: (0, 0))],
        out_specs=[row, row, pl.BlockSpec((tm, n_exp), lambda i: (i, 0))],
        out_shape=[jax.ShapeDtypeStruct((n, d), F32), jax.ShapeDtypeStruct((n, d), BF16),
                   jax.ShapeDtypeStruct((n, n_exp), F32)],
        compiler_params=_cparams("parallel"),
        name="resid_norm_route",
    )(x, y, gain.reshape(1, d), *mod_gate.arrays, *mod.arrays, *mod.arrays, w_router, b_router.reshape(1, n_exp))


def _final_kernel(mod, x_ref, y_ref, g_ref, gap, gas, o_ref):
    x = x_ref[...] + mod.pick(gap, gas) * y_ref[...]
    o_ref[...] = _rms(x, g_ref[...])


def _final(x, y, gain, mod_gate, ga_chunk, tm):
    n, d = x.shape
    row = pl.BlockSpec((tm, d), lambda i: (i, 0))
    return pl.pallas_call(
        functools.partial(_final_kernel, mod_gate),
        grid=(n // tm,),
        in_specs=[row, row, pl.BlockSpec((1, d), lambda i: (0, 0))] + mod_gate.specs(ga_chunk),
        out_specs=row,
        out_shape=jax.ShapeDtypeStruct((n, d), F32),
        compiler_params=_cparams("parallel"),
        name="resid_final_norm",
    )(x, y, gain.reshape(1, d), *mod_gate.arrays)


def _mm_kernel(a_ref, w_ref, o_ref, wb_ref):
    @pl.when(pl.program_id(1) == 0)
    def _():
        wb_ref[...] = w_ref[...].astype(BF16)
    o_ref[...] = jnp.dot(a_ref[...], wb_ref[...], preferred_element_type=F32).astype(o_ref.dtype)


def _matmul(a, w, n_out, tn, tm=512, out_dtype=F32, name="matmul"):
    m, k = a.shape
    tm = _tile(m, tm)
    return pl.pallas_call(
        _mm_kernel,
        grid=(n_out // tn, m // tm),
        in_specs=[pl.BlockSpec((tm, k), lambda j, i: (i, 0)),
                  pl.BlockSpec((k, tn), lambda j, i: (0, j))],
        out_specs=pl.BlockSpec((tm, tn), lambda j, i: (i, j)),
        out_shape=jax.ShapeDtypeStruct((m, n_out), out_dtype),
        scratch_shapes=[pltpu.VMEM((k, tn), BF16)],
        compiler_params=_cparams("parallel", "arbitrary"),
        name=name,
    )(a, w)


def _swiglu_up_kernel(a_ref, wg_ref, wu_ref, o_ref, wgb_ref, wub_ref):
    @pl.when(pl.program_id(2) == 0)
    def _():
        wgb_ref[...] = wg_ref[0].astype(BF16)
        wub_ref[...] = wu_ref[0].astype(BF16)
    a = a_ref[...]
    gate = jnp.dot(a, wgb_ref[...], preferred_element_type=F32)
    up = jnp.dot(a, wub_ref[...], preferred_element_type=F32)
    o_ref[0] = (_silu(gate) * up).astype(o_ref.dtype)


def _swiglu_up(a, w_up, tn=512, tm=512):
    m, k = a.shape
    ne, _, f2 = w_up.shape
    f = f2 // 2
    tn = _tile(f, tn)
    tm = _tile(m, tm)
    nj = f // tn
    return pl.pallas_call(
        _swiglu_up_kernel,
        grid=(ne, nj, m // tm),
        in_specs=[pl.BlockSpec((tm, k), lambda e, j, i: (i, 0)),
                  pl.BlockSpec((1, k, tn), lambda e, j, i: (e, 0, j)),
                  pl.BlockSpec((1, k, tn), lambda e, j, i: (e, 0, nj + j))],
        out_specs=pl.BlockSpec((1, tm, tn), lambda e, j, i: (e, i, j)),
        out_shape=jax.ShapeDtypeStruct((ne, m, f), BF16),
        scratch_shapes=[pltpu.VMEM((k, tn), BF16), pltpu.VMEM((k, tn), BF16)],
        compiler_params=_cparams("parallel", "parallel", "arbitrary"),
        name="swiglu_up",
    )(a, w_up, w_up)


def _moe_down_kernel(acc_ref, a_ref, w_ref, g_ref, o_ref, wb_ref):
    e = pl.program_id(0)

    @pl.when(pl.program_id(2) == 0)
    def _():
        wb_ref[...] = w_ref[0].astype(BF16)
    g = g_ref[...]
    lane = lax.broadcasted_iota(jnp.int32, g.shape, 1)
    ge = jnp.sum(jnp.where(lane == e, g, 0.0), axis=-1, keepdims=True)
    o_ref[...] = acc_ref[...] + ge * jnp.dot(a_ref[0], wb_ref[...], preferred_element_type=F32)


def _moe_down(act, w_down, gates, tn=256, tm=512):
    ne, m, f = act.shape
    d = w_down.shape[2]
    tn = _tile(d, tn)
    tm = _tile(m, tm)
    acc = jnp.zeros((m, d), F32)
    return pl.pallas_call(
        _moe_down_kernel,
        grid=(ne, d // tn, m // tm),
        in_specs=[pl.BlockSpec((tm, tn), lambda e, j, i: (i, j)),
                  pl.BlockSpec((1, tm, f), lambda e, j, i: (e, i, 0)),
                  pl.BlockSpec((1, f, tn), lambda e, j, i: (e, 0, j)),
                  pl.BlockSpec((tm, ne), lambda e, j, i: (i, 0))],
        out_specs=pl.BlockSpec((tm, tn), lambda e, j, i: (i, j)),
        out_shape=jax.ShapeDtypeStruct((m, d), F32),
        scratch_shapes=[pltpu.VMEM((f, tn), BF16)],
        input_output_aliases={0: 0},
        compiler_params=_cparams("arbitrary", "arbitrary", "arbitrary"),
        name="moe_down",
    )(acc, act, w_down, gates)


def _shift_rows(carry, u, j):
    ext = jnp.concatenate([carry, u], axis=0)
    return pltpu.roll(ext, j, 0)[8:]


def _sconv_prompt_kernel(bg_ref, cg_ref, xi_ref, w_ref, o_ref, st_ref, carry_ref):
    @pl.when(pl.program_id(1) == 0)
    def _():
        carry_ref[...] = jnp.zeros_like(carry_ref)
    u = cg_ref[...] * xi_ref[...]
    carry = carry_ref[...]
    w = w_ref[...]
    y = w[2:3] * u + w[1:2] * _shift_rows(carry, u, 1) + w[0:1] * _shift_rows(carry, u, 2)
    o_ref[...] = (bg_ref[...] * y).astype(o_ref.dtype)
    tail = u[u.shape[0] - 8:]
    carry_ref[...] = tail
    st_ref[0] = tail


def _sconv_prompt(s_in, conv_w, n_seq, t_len, d, tt=256):
    tt = _tile(t_len, tt)
    nt = t_len // tt
    blk = lambda c: pl.BlockSpec((tt, d), lambda b, t: (b * nt + t, c))
    return pl.pallas_call(
        _sconv_prompt_kernel,
        grid=(n_seq, nt),
        in_specs=[blk(0), blk(1), blk(2), pl.BlockSpec((3, d), lambda b, t: (0, 0))],
        out_specs=[pl.BlockSpec((tt, d), lambda b, t: (b * nt + t, 0)),
                   pl.BlockSpec((1, 8, d), lambda b, t: (b, 0, 0))],
        out_shape=[jax.ShapeDtypeStruct((n_seq * t_len, d), BF16), jax.ShapeDtypeStruct((n_seq, 8, d), F32)],
        scratch_shapes=[pltpu.VMEM((8, d), F32)],
        compiler_params=_cparams("parallel", "arbitrary"),
        name="sconv_prompt",
    )(s_in, s_in, s_in, conv_w)


def _sconv_sample_kernel(nb, bg_ref, cg_ref, xi_ref, s0_ref, s1_ref, w_ref, o_ref, u_ref):
    u = cg_ref[...] * xi_ref[...]
    full = jnp.concatenate([s0_ref[...], s1_ref[...], u], axis=0)
    n = u.shape[0]
    w = w_ref[...]
    y = w[0:1] * full[0:n] + w[1:2] * full[nb:nb + n] + w[2:3] * full[2 * nb:2 * nb + n]
    o_ref[...] = (bg_ref[...] * y).astype(o_ref.dtype)
    u_ref[...] = u


def _sconv_sample(s_in, state2d, conv_w, row_block, nb, ns, d, ct=512):
    ct = _tile(d, ct)
    nj = d // ct
    blk = lambda c: pl.BlockSpec((ns, ct), lambda j: (row_block, c * nj + j))
    st = lambda s: pl.BlockSpec((nb, ct), lambda j: (0, s * nj + j))
    return pl.pallas_call(
        functools.partial(_sconv_sample_kernel, nb),
        grid=(nj,),
        in_specs=[blk(0), blk(1), blk(2), st(0), st(1), pl.BlockSpec((3, ct), lambda j: (0, j))],
        out_specs=[pl.BlockSpec((ns, ct), lambda j: (0, j)), pl.BlockSpec((ns, ct), lambda j: (0, j))],
        out_shape=[jax.ShapeDtypeStruct((ns, d), BF16), jax.ShapeDtypeStruct((ns, d), F32)],
        compiler_params=_cparams("parallel"),
        name="sconv_sample",
    )(s_in, s_in, s_in, state2d, state2d, conv_w)


def _l2norm_heads(x):
    parts = []
    for h in range(x.shape[1] // HEAD):
        seg = x[:, h * HEAD:(h + 1) * HEAD]
        parts.append(seg * lax.rsqrt(jnp.sum(seg * seg, axis=-1, keepdims=True) + NORM_EPS))
    return jnp.concatenate(parts, axis=1) if len(parts) > 1 else parts[0]


def _store_qkv(o_ref, act, is_qk):
    @pl.when(is_qk)
    def _():
        o_ref[...] = _l2norm_heads(act)

    @pl.when(jnp.logical_not(is_qk))
    def _():
        o_ref[...] = act


def _gconv_prompt_kernel(n_qk_tiles, x_ref, w_ref, o_ref, carry_ref):
    @pl.when(pl.program_id(2) == 0)
    def _():
        carry_ref[...] = jnp.zeros_like(carry_ref)
    u = x_ref[...]
    carry = carry_ref[...]
    w = w_ref[...]
    y = (w[3:4] * u + w[2:3] * _shift_rows(carry, u, 1) + w[1:2] * _shift_rows(carry, u, 2)
         + w[0:1] * _shift_rows(carry, u, 3))
    carry_ref[...] = u[u.shape[0] - 8:]
    _store_qkv(o_ref, _silu(y), pl.program_id(1) < n_qk_tiles)


def _gconv_prompt(proj, conv_w, n_seq, t_len, d, tt=256, ct=512):
    ch = 4 * d
    tt = _tile(t_len, tt)
    ct = _tile(ch, ct)
    nt = t_len // tt
    return pl.pallas_call(
        functools.partial(_gconv_prompt_kernel, 2 * d // ct),
        grid=(n_seq, ch // ct, nt),
        in_specs=[pl.BlockSpec((tt, ct), lambda b, j, t: (b * nt + t, j)),
                  pl.BlockSpec((4, ct), lambda b, j, t: (0, j))],
        out_specs=pl.BlockSpec((tt, ct), lambda b, j, t: (b * nt + t, j)),
        out_shape=jax.ShapeDtypeStruct((n_seq * t_len, ch), F32),
        scratch_shapes=[pltpu.VMEM((8, ct), F32)],
        compiler_params=_cparams("parallel", "parallel", "arbitrary"),
        name="gdn_conv_prompt",
    )(proj, conv_w)


def _gconv_sample_kernel(n_qk_tiles, nb, s0_ref, s1_ref, s2_ref, x_ref, w_ref, o_ref):
    u = x_ref[...]
    full = jnp.concatenate([s0_ref[...], s1_ref[...], s2_ref[...], u], axis=0)
    n = u.shape[0]
    w = w_ref[...]
    y = (w[0:1] * full[0:n] + w[1:2] * full[nb:nb + n] + w[2:3] * full[2 * nb:2 * nb + n]
         + w[3:4] * full[3 * nb:3 * nb + n])
    _store_qkv(o_ref, _silu(y), pl.program_id(0) < n_qk_tiles)


def _gconv_sample(proj, state2d, conv_w, row_block, nb, ns, d, ct=512):
    ch = 4 * d
    ct = _tile(ch, ct)
    nj = ch // ct
    st = lambda s: pl.BlockSpec((nb, ct), lambda j: (0, s * nj + j))
    return pl.pallas_call(
        functools.partial(_gconv_sample_kernel, 2 * d // ct, nb),
        grid=(nj,),
        in_specs=[st(0), st(1), st(2), pl.BlockSpec((ns, ct), lambda j: (row_block, j)),
                  pl.BlockSpec((4, ct), lambda j: (0, j))],
        out_specs=pl.BlockSpec((ns, ct), lambda j: (0, j)),
        out_shape=jax.ShapeDtypeStruct((ns, ch), F32),
        compiler_params=_cparams("parallel"),
        name="gdn_conv_sample",
    )(state2d, state2d, state2d, proj, conv_w)


def _gates_kernel(chunk, ba_ref, alog_ref, dtb_ref, beta_ref, gc_ref, gtot_ref, gct_ref):
    ba = ba_ref[...]
    nh = ba.shape[1] // 2
    r = ba.shape[0]
    beta_ref[...] = jax.nn.sigmoid(ba[:, :nh])
    g = -jnp.exp(alog_ref[...]) * _softplus(ba[:, nh:] + dtb_ref[...])
    ri = lax.broadcasted_iota(jnp.int32, (r, r), 0)
    ci = lax.broadcasted_iota(jnp.int32, (r, r), 1)
    same = _blk(ri, chunk) == _blk(ci, chunk)
    tri = jnp.where(same & (ri >= ci), 1.0, 0.0)
    blk = jnp.where(same, 1.0, 0.0)
    gc = jnp.dot(tri, g, precision=HIGHEST, preferred_element_type=F32)
    gc_ref[...] = gc
    gtot_ref[...] = jnp.dot(blk, g, precision=HIGHEST, preferred_element_type=F32)
    eye = jnp.where(lax.broadcasted_iota(jnp.int32, (nh, nh), 0) == lax.broadcasted_iota(jnp.int32, (nh, nh), 1),
                    1.0, 0.0)
    gct_ref[0] = lax.dot_general(eye, gc, (((1,), (1,)), ((), ())), precision=HIGHEST,
                                 preferred_element_type=F32)


def _gates(ba, a_log, dt_bias, chunk, r):
    n, h2 = ba.shape
    nh = h2 // 2
    row = pl.BlockSpec((r, nh), lambda i: (i, 0))
    vec = pl.BlockSpec((1, nh), lambda i: (0, 0))
    return pl.pallas_call(
        functools.partial(_gates_kernel, chunk),
        grid=(n // r,),
        in_specs=[pl.BlockSpec((r, h2), lambda i: (i, 0)), vec, vec],
        out_specs=[row, row, row, pl.BlockSpec((1, nh, r), lambda i: (i, 0, 0))],
        out_shape=[jax.ShapeDtypeStruct((n, nh), F32)] * 3 + [jax.ShapeDtypeStruct((n // r, nh, r), F32)],
        compiler_params=_cparams("parallel"),
        name="gdn_gates",
    )(ba, a_log.reshape(1, nh), dt_bias.reshape(1, nh))


def _pick_col(x, h):
    lane = lax.broadcasted_iota(jnp.int32, x.shape, 1)
    return jnp.sum(jnp.where(lane == h, x, 0.0), axis=1, keepdims=True)


def _chunk_terms(chunk, q, k, v, beta, gc, gtot, gc_row):
    r = q.shape[0]
    ri = lax.broadcasted_iota(jnp.int32, (r, r), 0)
    ci = lax.broadcasted_iota(jnp.int32, (r, r), 1)
    causal = (_blk(ri, chunk) == _blk(ci, chunk)) & (ri >= ci)
    decay = jnp.exp(jnp.where(causal, gc - gc_row, -jnp.inf))
    kb = k * beta
    low = jnp.where(ri > ci, _dot_nt(kb, k) * decay, 0.0)
    e = -jnp.where(_blk(ri, 2) == _blk(ci, 2), low, 0.0)
    s = 2
    while s < chunk:
        lm = jnp.where((_blk(ri, 2 * s) == _blk(ci, 2 * s)) & (_blk(ri, s) != _blk(ci, s)), low, 0.0)
        x = lm + _dot(e, lm)
        e = e - (x + _dot(x, e))
        s *= 2
    eg = jnp.exp(gc)
    rhs = jnp.concatenate([v * beta, kb * eg], axis=1)
    uw = rhs + _dot(e, rhs)
    dv = v.shape[1]
    qs = q * (HEAD ** -0.5)
    a_intra = _dot_nt(qs, k) * decay
    return uw[:, :dv], uw[:, dv:], a_intra, qs * eg, k * jnp.exp(gtot - gc)


def _gated_norm(o, z, gain):
    return (_rms(o, gain) * _silu(z)).astype(BF16)


def _gdn_prompt_kernel(chunk, hps, q_ref, k_ref, v_ref, z_ref, beta_ref, gc_ref, gtot_ref, gct_ref, gn_ref,
                       o_ref, s_ref, s_scr):
    @pl.when(pl.program_id(2) == 0)
    def _():
        s_scr[...] = jnp.zeros_like(s_scr)
    r = q_ref.shape[0]
    h0 = pl.program_id(1) * hps
    terms = []
    for hl in range(hps):
        kh = hl // 2
        q = q_ref[:, kh * HEAD:(kh + 1) * HEAD]
        k = k_ref[:, kh * HEAD:(kh + 1) * HEAD]
        v = v_ref[:, hl * HEAD:(hl + 1) * HEAD]
        beta = _pick_col(beta_ref[...], h0 + hl)
        gc = _pick_col(gc_ref[...], h0 + hl)
        gtot = _pick_col(gtot_ref[...], h0 + hl)
        gc_row = gct_ref[0, pl.ds(h0 + hl, 1), :]
        terms.append(_chunk_terms(chunk, q, k, v, beta, gc, gtot, gc_row) + (jnp.exp(gtot),))
    states = [s_scr[hl] for hl in range(hps)]
    outs = [[] for _ in range(hps)]
    for c in range(r // chunk):
        lo, hi = c * chunk, (c + 1) * chunk
        for hl in range(hps):
            u, w, a, qd, kd, egt = terms[hl]
            s = states[hl]
            pq = _dot(jnp.concatenate([w[lo:hi], qd[lo:hi]], axis=0), s)
            v_new = u[lo:hi] - pq[:chunk]
            outs[hl].append(pq[chunk:] + _dot(a[lo:hi, lo:hi], v_new))
            states[hl] = s * egt[lo:lo + 1] + _dot_tn(kd[lo:hi], v_new)
    for hl in range(hps):
        s_scr[hl] = states[hl]
        o = jnp.concatenate(outs[hl], axis=0)
        o_ref[:, hl * HEAD:(hl + 1) * HEAD] = _gated_norm(o, z_ref[:, hl * HEAD:(hl + 1) * HEAD], gn_ref[...])
    s_ref[0] = s_scr[...]


def _gdn_prompt(qkv, proj, beta, gc, gtot, gct, g_onorm, n_seq, t_len, d, chunk, r=256, hps=4):
    nvh = 2 * d // HEAD
    ns = t_len // r
    kc = hps // 2 * HEAD
    vc = hps * HEAD
    rows = lambda b, g, s: b * ns + s
    return pl.pallas_call(
        functools.partial(_gdn_prompt_kernel, chunk, hps),
        grid=(n_seq, nvh // hps, ns),
        in_specs=[pl.BlockSpec((r, kc), lambda b, g, s: (rows(b, g, s), g)),
                  pl.BlockSpec((r, kc), lambda b, g, s: (rows(b, g, s), d // kc + g)),
                  pl.BlockSpec((r, vc), lambda b, g, s: (rows(b, g, s), 2 * d // vc + g)),
                  pl.BlockSpec((r, vc), lambda b, g, s: (rows(b, g, s), 4 * d // vc + g)),
                  pl.BlockSpec((r, nvh), lambda b, g, s: (rows(b, g, s), 0)),
                  pl.BlockSpec((r, nvh), lambda b, g, s: (rows(b, g, s), 0)),
                  pl.BlockSpec((r, nvh), lambda b, g, s: (rows(b, g, s), 0)),
                  pl.BlockSpec((1, nvh, r), lambda b, g, s: (rows(b, g, s), 0, 0)),
                  pl.BlockSpec((1, HEAD), lambda b, g, s: (0, 0))],
        out_specs=[pl.BlockSpec((r, vc), lambda b, g, s: (rows(b, g, s), g)),
                   pl.BlockSpec((1, hps, HEAD, HEAD), lambda b, g, s: (b, g, 0, 0))],
        out_shape=[jax.ShapeDtypeStruct((n_seq * t_len, 2 * d), BF16),
                   jax.ShapeDtypeStruct((n_seq, nvh, HEAD, HEAD), F32)],
        scratch_shapes=[pltpu.VMEM((hps, HEAD, HEAD), F32)],
        compiler_params=_cparams("parallel", "parallel", "arbitrary"),
        name="gdn_prompt",
    )(qkv, qkv, qkv, proj, beta, gc, gtot, gct, g_onorm.reshape(1, HEAD))


def _gdn_sample_kernel(chunk, hps, q_ref, k_ref, v_ref, z_ref, beta_ref, gc_ref, gtot_ref, gct_ref, gn_ref, s0_ref,
                       o_ref, s_ref, wq_scr, kd_scr, eg_scr, pq_scr):
    r = q_ref.shape[0]
    h0 = pl.program_id(0) * hps
    per_row = 8 // chunk
    terms = []
    for hl in range(hps):
        kh = hl // 2
        q = q_ref[:, kh * HEAD:(kh + 1) * HEAD]
        k = k_ref[:, kh * HEAD:(kh + 1) * HEAD]
        v = v_ref[:, hl * HEAD:(hl + 1) * HEAD]
        beta = _pick_col(beta_ref[...], h0 + hl)
        gc = _pick_col(gc_ref[...], h0 + hl)
        gtot = _pick_col(gtot_ref[...], h0 + hl)
        gc_row = gct_ref[0, pl.ds(h0 + hl, 1), :]
        u, w, a, qd, kd = _chunk_terms(chunk, q, k, v, beta, gc, gtot, gc_row)
        wq_scr[hl, :, 0:8, :] = w.reshape(r // 8, 8, HEAD)
        wq_scr[hl, :, 8:16, :] = qd.reshape(r // 8, 8, HEAD)
        kd_scr[hl] = kd
        eg_scr[hl] = jnp.broadcast_to(jnp.exp(gtot), (r, HEAD))
        terms.append((u, a))

    row16 = lax.broadcasted_iota(jnp.int32, (16, HEAD), 0)
    row8 = lax.broadcasted_iota(jnp.int32, (8, HEAD), 0)

    def pair_pred(p, carry):
        for hl in range(hps):
            lhs = wq_scr[hl, p]
            acc = jnp.zeros((16, HEAD), F32)
            for e in range(per_row):
                pq = _dot(lhs, s0_ref[p * per_row + e, hl])
                acc = jnp.where(_blk(row16 & 7, chunk) == e, pq, acc)
            pq_scr[hl, p] = acc
        return carry
    lax.fori_loop(0, r // 8, pair_pred, 0)

    v_news = []
    for hl in range(hps):
        u, a = terms[hl]
        pq = pq_scr[hl]
        v_new = u - pq[:, 0:8, :].reshape(r, HEAD)
        o = pq[:, 8:16, :].reshape(r, HEAD) + _dot(a, v_new)
        o_ref[:, hl * HEAD:(hl + 1) * HEAD] = _gated_norm(o, z_ref[:, hl * HEAD:(hl + 1) * HEAD], gn_ref[...])
        pq_scr[hl, :, 0:8, :] = v_new.reshape(r // 8, 8, HEAD)

    def pair_update(p, carry):
        for hl in range(hps):
            v_new = pq_scr[hl, p, 0:8, :]
            kd = kd_scr[hl, pl.ds(pl.multiple_of(p * 8, 8), 8), :]
            for e in range(per_row):
                b = p * per_row + e
                decay = eg_scr[hl, pl.ds(p * 8 + e * chunk, 1), :]
                kd_e = jnp.where(_blk(row8, chunk) == e, kd, 0.0)
                s_ref[b, hl] = s0_ref[b, hl] * decay + _dot_tn(kd_e, v_new)
        return carry
    lax.fori_loop(0, r // 8, pair_update, 0)


def _gdn_sample(qkv, z, beta, gc, gtot, gct, g_onorm, state, d, chunk, r=128, hps=4):
    n = qkv.shape[0]
    nvh = 2 * d // HEAD
    kc = hps // 2 * HEAD
    vc = hps * HEAD
    nb = r // chunk
    return pl.pallas_call(
        functools.partial(_gdn_sample_kernel, chunk, hps),
        grid=(nvh // hps, n // r),
        in_specs=[pl.BlockSpec((r, kc), lambda g, s: (s, g)),
                  pl.BlockSpec((r, kc), lambda g, s: (s, d // kc + g)),
                  pl.BlockSpec((r, vc), lambda g, s: (s, 2 * d // vc + g)),
                  pl.BlockSpec((r, vc), lambda g, s: (s, g)),
                  pl.BlockSpec((r, nvh), lambda g, s: (s, 0)),
                  pl.BlockSpec((r, nvh), lambda g, s: (s, 0)),
                  pl.BlockSpec((r, nvh), lambda g, s: (s, 0)),
                  pl.BlockSpec((1, nvh, r), lambda g, s: (s, 0, 0)),
                  pl.BlockSpec((1, HEAD), lambda g, s: (0, 0)),
                  pl.BlockSpec((nb, hps, HEAD, HEAD), lambda g, s: (s, g, 0, 0))],
        out_specs=[pl.BlockSpec((r, vc), lambda g, s: (s, g)),
                   pl.BlockSpec((nb, hps, HEAD, HEAD), lambda g, s: (s, g, 0, 0))],
        out_shape=[jax.ShapeDtypeStruct((n, 2 * d), BF16),
                   jax.ShapeDtypeStruct(state.shape, F32)],
        scratch_shapes=[pltpu.VMEM((hps, r // 8, 16, HEAD), F32), pltpu.VMEM((hps, r, HEAD), F32),
                        pltpu.VMEM((hps, r, HEAD), F32), pltpu.VMEM((hps, r // 8, 16, HEAD), F32)],
        compiler_params=_cparams("parallel", "parallel"),
        name="gdn_sample",
    )(qkv, qkv, qkv, z, beta, gc, gtot, gct, g_onorm.reshape(1, HEAD), state)


def kernel(x_prompt, x_sample, c_prompt, c_sample, state_gdn, state_gdn_conv, state_sconv, w_ada, b_ada,
           g_norm_mix, g_norm_ffn, g_norm_out, gdn_w_in, gdn_conv_w, gdn_a_log, gdn_dt_bias, gdn_g_onorm,
           gdn_w_out, sc_w_in, sc_conv_w, sc_w_out, ffn_w_up, ffn_w_down, moe_w_router, moe_b_router,
           moe_w_up, moe_w_down):
    nbp, t_len, d = x_prompt.shape
    nbs, steps, _ = x_sample.shape
    n_p, n_s = nbp * t_len, nbs * steps
    nvh = 2 * d // HEAD
    chunk_p = min(64, t_len)
    tm = nbs
    assert t_len % tm == 0 and n_p % n_s == 0 and 8 % steps == 0

    x = jnp.concatenate([x_prompt.reshape(n_p, d), x_sample.transpose(1, 0, 2).reshape(n_s, d)], axis=0)
    pad = (-(nbp + nbs)) % 8
    c_all = jnp.concatenate([c_prompt, c_sample, jnp.zeros((pad, d), F32)], axis=0)
    m = _ada(c_all, w_ada, b_ada)
    mods = [_Mod(m[i, :nbp].reshape(nbp, 1, 6 * d), m[i, nbp:nbp + nbs], d, tm, n_p // tm, t_len // tm)
            for i in range(2)]
    SH1, SC1, GA1, SH2, SC2, GA2 = range(6)

    h = _prep(x, g_norm_mix[0], mods[0], SC1, SH1, tm)
    w_in = gdn_w_in[0]
    proj = _matmul(h, w_in, 6 * d, _tile(6 * d, 1024), name="gdn_in_proj")
    ba = _matmul(h, w_in[:, 6 * d:], 2 * nvh, 2 * nvh, name="gdn_in_proj_ba")
    conv_w = gdn_conv_w[0]
    qkv_p = _gconv_prompt(proj, conv_w, nbp, t_len, d)
    st_gconv = state_gdn_conv[0].reshape(nbs, 3 * 4 * d)
    qkv_s = _gconv_sample(proj, st_gconv, conv_w, n_p // n_s, nbs, n_s, d)
    to_seq_major = lambda a: a.reshape(steps, nbs, a.shape[-1]).transpose(1, 0, 2).reshape(n_s, a.shape[-1])
    qkv_s = to_seq_major(qkv_s)
    z_s = to_seq_major(proj[n_p:, 4 * d:6 * d])
    ba_s = to_seq_major(ba[n_p:])
    beta_p, gc_p, gtot_p, gct_p = _gates(ba[:n_p], gdn_a_log[0], gdn_dt_bias[0], chunk_p, 256)
    beta_s, gc_s, gtot_s, gct_s = _gates(ba_s, gdn_a_log[0], gdn_dt_bias[0], steps, 128)
    o_p, gdn_p = _gdn_prompt(qkv_p, proj, beta_p, gc_p, gtot_p, gct_p, gdn_g_onorm[0], nbp, t_len, d, chunk_p)
    o_s, gdn_s = _gdn_sample(qkv_s, z_s, beta_s, gc_s, gtot_s, gct_s, gdn_g_onorm[0], state_gdn[0], d, steps)
    to_time_major = lambda a: a.reshape(nbs, steps, a.shape[-1]).transpose(1, 0, 2).reshape(n_s, a.shape[-1])
    o = jnp.concatenate([o_p, to_time_major(o_s)], axis=0)
    y = _matmul(o, gdn_w_out[0], d, _tile(d, 512), name="gdn_out_proj")
    gconv_p = proj[:n_p, :4 * d].reshape(nbp, t_len, 4 * d)[:, t_len - 3:]
    gconv_s = proj[n_p + nbs:, :4 * d].reshape(3, nbs, 4 * d).transpose(1, 0, 2)

    x, h = _resid_prep(x, y, g_norm_ffn[0], mods[0], GA1, mods[0], SC2, SH2, tm)
    act = _swiglu_up(h, ffn_w_up)[0]
    f = _matmul(act, ffn_w_down[0], d, _tile(d, 512), name="ffn_down")

    x, h = _resid_prep(x, f, g_norm_mix[1], mods[0], GA2, mods[1], SC1, SH1, tm)
    s_in = _matmul(h, sc_w_in[0], 3 * d, _tile(3 * d, 1024), name="sconv_in_proj")
    mix_p, sconv_tail = _sconv_prompt(s_in, sc_conv_w[0], nbp, t_len, d)
    mix_s, u_s = _sconv_sample(s_in, state_sconv[0].reshape(nbs, 2 * d), sc_conv_w[0], n_p // n_s, nbs, n_s, d)
    y = _matmul(jnp.concatenate([mix_p, mix_s], axis=0), sc_w_out[0], d, _tile(d, 512), name="sconv_out_proj")
    sconv_p = sconv_tail[:, 6:8]
    sconv_s = u_s[n_s - 2 * nbs:].reshape(2, nbs, d).transpose(1, 0, 2)

    x, h, gates = _resid_route(x, y, g_norm_ffn[1], mods[1], GA1, mods[1], SC2, SH2,
                               moe_w_router[0], moe_b_router[0], tm)
    act = _swiglu_up(h, moe_w_up[0])
    f = _moe_down(act, moe_w_down[0], gates)
    out = _final(x, f, g_norm_out, mods[1], GA2, tm)

    y_prompt = out[:n_p].reshape(nbp, t_len, d)
    y_sample = out[n_p:].reshape(steps, nbs, d).transpose(1, 0, 2)
    return (y_prompt, y_sample, gdn_p[None], gconv_p[None], sconv_p[None], gdn_s[None], gconv_s[None], sconv_s[None])
```

```python
import functools

import jax
import jax.numpy as jnp
from jax import lax
from jax.experimental import pallas as pl
from jax.experimental.pallas import tpu as pltpu

F32 = jnp.float32
BF16 = jnp.bfloat16
NORM_EPS = 1e-6
HEAD = 128
V7X_VMEM_LIMIT = 56 * 1024 * 1024
HIGHEST = lax.Precision.HIGHEST


def _cparams(*sem):
    return pltpu.CompilerParams(dimension_semantics=sem, vmem_limit_bytes=V7X_VMEM_LIMIT)


def _tile(n, pref):
    return pref if n % pref == 0 else n


def _silu(x):
    return x * jax.nn.sigmoid(x)


def _softplus(x):
    return jnp.maximum(x, 0.0) + jnp.log1p(jnp.exp(-jnp.abs(x)))


def _blk(idx, size):
    assert size & (size - 1) == 0
    return lax.shift_right_logical(idx, size.bit_length() - 1)


def _dot(a, b):
    return jnp.dot(a.astype(BF16), b.astype(BF16), preferred_element_type=F32)


def _dot_nt(a, b):
    return lax.dot_general(a.astype(BF16), b.astype(BF16), (((1,), (1,)), ((), ())), preferred_element_type=F32)


def _dot_tn(a, b):
    return lax.dot_general(a.astype(BF16), b.astype(BF16), (((0,), (0,)), ((), ())), preferred_element_type=F32)


def _ada_kernel(c_ref, w_ref, b_ref, o_ref):
    c = c_ref[...]
    o_ref[0] = _dot(_silu(c), w_ref[0]) + b_ref[0]


def _ada(c_all, w_ada, b_ada):
    nl, d, n6 = w_ada.shape
    r = c_all.shape[0]
    tn = _tile(n6, 1024)
    return pl.pallas_call(
        _ada_kernel,
        grid=(nl, n6 // tn),
        in_specs=[pl.BlockSpec((r, d), lambda l, j: (0, 0)),
                  pl.BlockSpec((1, d, tn), lambda l, j: (l, 0, j)),
                  pl.BlockSpec((1, 1, tn), lambda l, j: (l, 0, j))],
        out_specs=pl.BlockSpec((1, r, tn), lambda l, j: (l, 0, j)),
        out_shape=jax.ShapeDtypeStruct((nl, r, n6), F32),
        compiler_params=_cparams("parallel", "parallel"),
        name="ada_mod",
    )(c_all, w_ada, b_ada.reshape(nl, 1, n6))


class _Mod:
    def __init__(self, mod_p, mod_s, d, tm, n_ptiles, tiles_per_seq):
        self.arrays = (mod_p, mod_s)
        self.n_ptiles = n_ptiles
        nb = mod_p.shape[0]

        def specs(chunk):
            return [pl.BlockSpec((1, 1, d), lambda i, *_: (jnp.minimum(i // tiles_per_seq, nb - 1), 0, chunk)),
                    pl.BlockSpec((tm, d), lambda i, *_: (0, chunk))]
        self.specs = specs

    def pick(self, p_ref, s_ref):
        is_prompt = pl.program_id(0) < self.n_ptiles
        return jnp.where(is_prompt, p_ref[0], s_ref[...])


def _rms(x, gain):
    ms = jnp.mean(x * x, axis=-1, keepdims=True)
    return x * lax.rsqrt(ms + NORM_EPS) * gain


def _prep_kernel(mod, x_ref, g_ref, scp, scs, shp, shs, h_ref):
    y = _rms(x_ref[...], g_ref[...])
    h_ref[...] = (y * (1.0 + mod.pick(scp, scs)) + mod.pick(shp, shs)).astype(h_ref.dtype)


def _prep(x, gain, mod, sc_chunk, sh_chunk, tm):
    n, d = x.shape
    row = pl.BlockSpec((tm, d), lambda i: (i, 0))
    return pl.pallas_call(
        functools.partial(_prep_kernel, mod),
        grid=(n // tm,),
        in_specs=[row, pl.BlockSpec((1, d), lambda i: (0, 0))] + mod.specs(sc_chunk) + mod.specs(sh_chunk),
        out_specs=row,
        out_shape=jax.ShapeDtypeStruct((n, d), BF16),
        compiler_params=_cparams("parallel"),
        name="norm_mod",
    )(x, gain.reshape(1, d), *mod.arrays, *mod.arrays)


def _resid_prep_kernel(mod, x_ref, y_ref, g_ref, gap, gas, scp, scs, shp, shs, xo_ref, h_ref):
    x = x_ref[...] + mod.pick(gap, gas) * y_ref[...]
    xo_ref[...] = x
    y = _rms(x, g_ref[...])
    h_ref[...] = (y * (1.0 + mod.pick(scp, scs)) + mod.pick(shp, shs)).astype(h_ref.dtype)


def _resid_prep(x, y, gain, mod_gate, ga_chunk, mod, sc_chunk, sh_chunk, tm):
    n, d = x.shape
    row = pl.BlockSpec((tm, d), lambda i: (i, 0))
    return pl.pallas_call(
        functools.partial(_resid_prep_kernel, mod),
        grid=(n // tm,),
        in_specs=[row, row, pl.BlockSpec((1, d), lambda i: (0, 0))]
        + mod_gate.specs(ga_chunk) + mod.specs(sc_chunk) + mod.specs(sh_chunk),
        out_specs=[row, row],
        out_shape=[jax.ShapeDtypeStruct((n, d), F32), jax.ShapeDtypeStruct((n, d), BF16)],
        compiler_params=_cparams("parallel"),
        name="resid_norm_mod",
    )(x, y, gain.reshape(1, d), *mod_gate.arrays, *mod.arrays, *mod.arrays)


def _resid_route_kernel(mod, n_exp, x_ref, y_ref, g_ref, gap, gas, scp, scs, shp, shs, wr_ref, br_ref,
                        xo_ref, h_ref, info_ref):
    x = x_ref[...] + mod.pick(gap, gas) * y_ref[...]
    xo_ref[...] = x
    h = _rms(x, g_ref[...]) * (1.0 + mod.pick(scp, scs)) + mod.pick(shp, shs)
    h_ref[...] = h
    logits = jnp.dot(h, wr_ref[...], precision=HIGHEST, preferred_element_type=F32) + br_ref[...]
    lane = lax.broadcasted_iota(jnp.int32, logits.shape, 1).astype(F32)
    m1 = jnp.max(logits, axis=-1, keepdims=True)
    i1 = jnp.min(jnp.where(logits == m1, lane, float(n_exp)), axis=-1, keepdims=True)
    rest = jnp.where(lane == i1, -jnp.inf, logits)
    m2 = jnp.max(rest, axis=-1, keepdims=True)
    i2 = jnp.min(jnp.where(rest == m2, lane, float(n_exp)), axis=-1, keepdims=True)
    e2 = jnp.exp(m2 - m1)
    g1 = 1.0 / (1.0 + e2)
    g2 = e2 / (1.0 + e2)
    info_ref[...] = (jnp.where(lane == 0.0, i1, 0.0) + jnp.where(lane == 1.0, i2, 0.0)
                     + jnp.where(lane == 2.0, g1, 0.0) + jnp.where(lane == 3.0, g2, 0.0))


def _resid_route(x, y, gain, mod_gate, ga_chunk, mod, sc_chunk, sh_chunk, w_router, b_router, tm):
    n, d = x.shape
    n_exp = w_router.shape[1]
    row = pl.BlockSpec((tm, d), lambda i: (i, 0))
    return pl.pallas_call(
        functools.partial(_resid_route_kernel, mod, n_exp),
        grid=(n // tm,),
        in_specs=[row, row, pl.BlockSpec((1, d), lambda i: (0, 0))]
        + mod_gate.specs(ga_chunk) + mod.specs(sc_chunk) + mod.specs(sh_chunk)
        + [pl.BlockSpec((d, n_exp), lambda i: (0, 0)), pl.BlockSpec((1, n_exp), lambda i: (0, 0))],
        out_specs=[row, row, pl.BlockSpec((tm, n_exp), lambda i: (i, 0))],
        out_shape=[jax.ShapeDtypeStruct((n, d), F32), jax.ShapeDtypeStruct((n, d), F32),
                   jax.ShapeDtypeStruct((n, n_exp), F32)],
        compiler_params=_cparams("parallel"),
        name="resid_norm_route",
    )(x, y, gain.reshape(1, d), *mod_gate.arrays, *mod.arrays, *mod.arrays, w_router, b_router.reshape(1, n_exp))


def _row_copy(src_hbm, row, dst, r, sem):
    return pltpu.make_async_copy(src_hbm.at[pl.ds(row, 1)], dst.at[pl.ds(r, 1)], sem)


def _final_kernel(mod, pos_ref, x_ref, y_hbm, info_ref, g_ref, gap, gas, o_ref, buf, sem):
    tm = x_ref.shape[0]
    base = pl.program_id(0) * (2 * tm)

    def issue(r, carry):
        for k in range(2):
            _row_copy(y_hbm, pos_ref[base + 2 * r + k], buf.at[k], r, sem.at[k]).start()
        return carry
    lax.fori_loop(0, tm, issue, 0, unroll=4)
    for k in range(2):
        pltpu.make_async_copy(y_hbm.at[pl.ds(0, tm)], buf.at[k], sem.at[k]).wait()
    info = info_ref[...]
    f = info[:, 2:3] * buf[0] + info[:, 3:4] * buf[1]
    x = x_ref[...] + mod.pick(gap, gas) * f
    o_ref[...] = _rms(x, g_ref[...])


def _final(x, y_sorted, pos, info, gain, mod_gate, ga_chunk, tm):
    n, d = x.shape
    row = pl.BlockSpec((tm, d), lambda i, *_: (i, 0))
    return pl.pallas_call(
        functools.partial(_final_kernel, mod_gate),
        grid_spec=pltpu.PrefetchScalarGridSpec(
            num_scalar_prefetch=1,
            grid=(n // tm,),
            in_specs=[row, pl.BlockSpec(memory_space=pl.ANY),
                      pl.BlockSpec((tm, info.shape[1]), lambda i, *_: (i, 0)),
                      pl.BlockSpec((1, d), lambda i, *_: (0, 0))] + mod_gate.specs(ga_chunk),
            out_specs=row,
            scratch_shapes=[pltpu.VMEM((2, tm, d), F32), pltpu.SemaphoreType.DMA((2,))]),
        out_shape=jax.ShapeDtypeStruct((n, d), F32),
        compiler_params=_cparams("arbitrary"),
        name="moe_combine_final_norm",
    )(pos, x, y_sorted, info, gain.reshape(1, d), *mod_gate.arrays)


def _route_tables(info, n_exp, tmg):
    n = info.shape[0]
    e = info[:, :2].astype(jnp.int32)
    onehot = (e[:, :, None] == jnp.arange(n_exp, dtype=jnp.int32)[None, None, :]).astype(jnp.int32).sum(axis=1)
    counts = onehot.sum(axis=0)
    padded = (counts + tmg - 1) // tmg * tmg
    ends = jnp.cumsum(padded)
    starts = ends - padded
    rank = jnp.cumsum(onehot, axis=0) - onehot
    pos = (starts[e] + jnp.take_along_axis(rank, e, axis=1)).reshape(-1).astype(jnp.int32)
    n_tiles = -(-(2 * n) // tmg) + n_exp
    src = jnp.zeros((n_tiles * tmg,), jnp.int32).at[pos].set(jnp.repeat(jnp.arange(n, dtype=jnp.int32), 2))
    n_valid = (ends[-1] // tmg).astype(jnp.int32)
    tiles = jnp.arange(n_tiles, dtype=jnp.int32)
    te = jnp.minimum(jnp.searchsorted(ends, tiles * tmg, side="right"), n_exp - 1).astype(jnp.int32)
    te = jnp.where(tiles < n_valid, te, te[jnp.maximum(n_valid - 1, 0)])
    return pos, src, te, n_valid.reshape(1)


def _gather_kernel(src_ref, nv_ref, h_hbm, o_ref, buf, sem):
    i = pl.program_id(0)
    tmg = buf.shape[0]

    @pl.when(i < nv_ref[0])
    def _():
        base = i * tmg

        def issue(r, carry):
            _row_copy(h_hbm, src_ref[base + r], buf, r, sem).start()
            return carry
        lax.fori_loop(0, tmg, issue, 0, unroll=8)
        pltpu.make_async_copy(h_hbm.at[pl.ds(0, tmg)], buf, sem).wait()
        o_ref[...] = buf[...].astype(o_ref.dtype)

    @pl.when(i >= nv_ref[0])
    def _():
        o_ref[...] = jnp.zeros_like(o_ref)


def _gather_rows(h, src, n_valid, tmg):
    d = h.shape[1]
    n_tiles = src.shape[0] // tmg
    return pl.pallas_call(
        _gather_kernel,
        grid_spec=pltpu.PrefetchScalarGridSpec(
            num_scalar_prefetch=2,
            grid=(n_tiles,),
            in_specs=[pl.BlockSpec(memory_space=pl.ANY)],
            out_specs=pl.BlockSpec((tmg, d), lambda i, src, nv: (i, 0)),
            scratch_shapes=[pltpu.VMEM((tmg, d), F32), pltpu.SemaphoreType.DMA(())]),
        out_shape=jax.ShapeDtypeStruct((n_tiles * tmg, d), BF16),
        compiler_params=_cparams("arbitrary"),
        name="moe_gather",
    )(src, n_valid, h)


def _mm_kernel(a_ref, w_ref, o_ref, wb_ref):
    @pl.when(pl.program_id(1) == 0)
    def _():
        wb_ref[...] = w_ref[...].astype(BF16)
    o_ref[...] = jnp.dot(a_ref[...], wb_ref[...], preferred_element_type=F32).astype(o_ref.dtype)


def _matmul(a, w, n_out, tn, tm=512, out_dtype=F32, name="matmul"):
    m, k = a.shape
    tm = _tile(m, tm)
    return pl.pallas_call(
        _mm_kernel,
        grid=(n_out // tn, m // tm),
        in_specs=[pl.BlockSpec((tm, k), lambda j, i: (i, 0)),
                  pl.BlockSpec((k, tn), lambda j, i: (0, j))],
        out_specs=pl.BlockSpec((tm, tn), lambda j, i: (i, j)),
        out_shape=jax.ShapeDtypeStruct((m, n_out), out_dtype),
        scratch_shapes=[pltpu.VMEM((k, tn), BF16)],
        compiler_params=_cparams("parallel", "arbitrary"),
        name=name,
    )(a, w)


def _new_expert(te_ref):
    i = pl.program_id(1)
    return (i == 0) | (te_ref[i] != te_ref[jnp.maximum(i - 1, 0)])


def _swiglu_up_kernel(te_ref, nv_ref, a_ref, wg_ref, wu_ref, o_ref, wgb_ref, wub_ref):
    @pl.when(_new_expert(te_ref))
    def _():
        wgb_ref[...] = wg_ref[0].astype(BF16)
        wub_ref[...] = wu_ref[0].astype(BF16)

    @pl.when(pl.program_id(1) < nv_ref[0])
    def _():
        a = a_ref[...]
        gate = jnp.dot(a, wgb_ref[...], preferred_element_type=F32)
        up = jnp.dot(a, wub_ref[...], preferred_element_type=F32)
        o_ref[...] = (_silu(gate) * up).astype(o_ref.dtype)

    @pl.when(pl.program_id(1) >= nv_ref[0])
    def _():
        o_ref[...] = jnp.zeros_like(o_ref)


def _swiglu_up(a, w_up, te, nv, tm, tn, name):
    m, k = a.shape
    f = w_up.shape[2] // 2
    nj = f // tn
    row = lambda j, i, te, nv: jnp.minimum(i, nv[0] - 1)
    return pl.pallas_call(
        _swiglu_up_kernel,
        grid_spec=pltpu.PrefetchScalarGridSpec(
            num_scalar_prefetch=2,
            grid=(nj, m // tm),
            in_specs=[pl.BlockSpec((tm, k), lambda j, i, te, nv: (row(j, i, te, nv), 0)),
                      pl.BlockSpec((1, k, tn), lambda j, i, te, nv: (te[i], 0, j)),
                      pl.BlockSpec((1, k, tn), lambda j, i, te, nv: (te[i], 0, nj + j))],
            out_specs=pl.BlockSpec((tm, tn), lambda j, i, te, nv: (i, j)),
            scratch_shapes=[pltpu.VMEM((k, tn), BF16), pltpu.VMEM((k, tn), BF16)]),
        out_shape=jax.ShapeDtypeStruct((m, f), BF16),
        compiler_params=_cparams("parallel", "arbitrary"),
        name=name,
    )(te, nv, a, w_up, w_up)


def _down_kernel(te_ref, nv_ref, a_ref, w_ref, o_ref, wb_ref):
    @pl.when(_new_expert(te_ref))
    def _():
        wb_ref[...] = w_ref[0].astype(BF16)

    @pl.when(pl.program_id(1) < nv_ref[0])
    def _():
        o_ref[...] = jnp.dot(a_ref[...], wb_ref[...], preferred_element_type=F32)

    @pl.when(pl.program_id(1) >= nv_ref[0])
    def _():
        o_ref[...] = jnp.zeros_like(o_ref)


def _grouped_down(a, w_down, te, nv, tm, tn, name):
    m, f = a.shape
    d = w_down.shape[2]
    row = lambda j, i, te, nv: jnp.minimum(i, nv[0] - 1)
    return pl.pallas_call(
        _down_kernel,
        grid_spec=pltpu.PrefetchScalarGridSpec(
            num_scalar_prefetch=2,
            grid=(d // tn, m // tm),
            in_specs=[pl.BlockSpec((tm, f), lambda j, i, te, nv: (row(j, i, te, nv), 0)),
                      pl.BlockSpec((1, f, tn), lambda j, i, te, nv: (te[i], 0, j))],
            out_specs=pl.BlockSpec((tm, tn), lambda j, i, te, nv: (i, j)),
            scratch_shapes=[pltpu.VMEM((f, tn), BF16)]),
        out_shape=jax.ShapeDtypeStruct((m, d), F32),
        compiler_params=_cparams("parallel", "arbitrary"),
        name=name,
    )(te, nv, a, w_down)


def _shift_rows(carry, u, j):
    ext = jnp.concatenate([carry, u], axis=0)
    return pltpu.roll(ext, j, 0)[8:]


def _sconv_prompt_kernel(bg_ref, cg_ref, xi_ref, w_ref, o_ref, st_ref, carry_ref):
    @pl.when(pl.program_id(1) == 0)
    def _():
        carry_ref[...] = jnp.zeros_like(carry_ref)
    u = cg_ref[...] * xi_ref[...]
    carry = carry_ref[...]
    w = w_ref[...]
    y = w[2:3] * u + w[1:2] * _shift_rows(carry, u, 1) + w[0:1] * _shift_rows(carry, u, 2)
    o_ref[...] = (bg_ref[...] * y).astype(o_ref.dtype)
    tail = u[u.shape[0] - 8:]
    carry_ref[...] = tail
    st_ref[0] = tail


def _sconv_prompt(s_in, conv_w, n_seq, t_len, d, tt=256):
    tt = _tile(t_len, tt)
    nt = t_len // tt
    blk = lambda c: pl.BlockSpec((tt, d), lambda b, t: (b * nt + t, c))
    return pl.pallas_call(
        _sconv_prompt_kernel,
        grid=(n_seq, nt),
        in_specs=[blk(0), blk(1), blk(2), pl.BlockSpec((3, d), lambda b, t: (0, 0))],
        out_specs=[pl.BlockSpec((tt, d), lambda b, t: (b * nt + t, 0)),
                   pl.BlockSpec((1, 8, d), lambda b, t: (b, 0, 0))],
        out_shape=[jax.ShapeDtypeStruct((n_seq * t_len, d), BF16), jax.ShapeDtypeStruct((n_seq, 8, d), F32)],
        scratch_shapes=[pltpu.VMEM((8, d), F32)],
        compiler_params=_cparams("parallel", "arbitrary"),
        name="sconv_prompt",
    )(s_in, s_in, s_in, conv_w)


def _sconv_sample_kernel(nb, bg_ref, cg_ref, xi_ref, s0_ref, s1_ref, w_ref, o_ref, u_ref):
    u = cg_ref[...] * xi_ref[...]
    full = jnp.concatenate([s0_ref[...], s1_ref[...], u], axis=0)
    n = u.shape[0]
    w = w_ref[...]
    y = w[0:1] * full[0:n] + w[1:2] * full[nb:nb + n] + w[2:3] * full[2 * nb:2 * nb + n]
    o_ref[...] = (bg_ref[...] * y).astype(o_ref.dtype)
    u_ref[...] = u


def _sconv_sample(s_in, state2d, conv_w, row_block, nb, ns, d, ct=512):
    ct = _tile(d, ct)
    nj = d // ct
    blk = lambda c: pl.BlockSpec((ns, ct), lambda j: (row_block, c * nj + j))
    st = lambda s: pl.BlockSpec((nb, ct), lambda j: (0, s * nj + j))
    return pl.pallas_call(
        functools.partial(_sconv_sample_kernel, nb),
        grid=(nj,),
        in_specs=[blk(0), blk(1), blk(2), st(0), st(1), pl.BlockSpec((3, ct), lambda j: (0, j))],
        out_specs=[pl.BlockSpec((ns, ct), lambda j: (0, j)), pl.BlockSpec((ns, ct), lambda j: (0, j))],
        out_shape=[jax.ShapeDtypeStruct((ns, d), BF16), jax.ShapeDtypeStruct((ns, d), F32)],
        compiler_params=_cparams("parallel"),
        name="sconv_sample",
    )(s_in, s_in, s_in, state2d, state2d, conv_w)


def _l2norm_heads(x):
    parts = []
    for h in range(x.shape[1] // HEAD):
        seg = x[:, h * HEAD:(h + 1) * HEAD]
        parts.append(seg * lax.rsqrt(jnp.sum(seg * seg, axis=-1, keepdims=True) + NORM_EPS))
    return jnp.concatenate(parts, axis=1) if len(parts) > 1 else parts[0]


def _store_qkv(o_ref, act, is_qk):
    @pl.when(is_qk)
    def _():
        o_ref[...] = _l2norm_heads(act)

    @pl.when(jnp.logical_not(is_qk))
    def _():
        o_ref[...] = act


def _gconv_prompt_kernel(n_qk_tiles, x_ref, w_ref, o_ref, carry_ref):
    @pl.when(pl.program_id(2) == 0)
    def _():
        carry_ref[...] = jnp.zeros_like(carry_ref)
    u = x_ref[...]
    carry = carry_ref[...]
    w = w_ref[...]
    y = (w[3:4] * u + w[2:3] * _shift_rows(carry, u, 1) + w[1:2] * _shift_rows(carry, u, 2)
         + w[0:1] * _shift_rows(carry, u, 3))
    carry_ref[...] = u[u.shape[0] - 8:]
    _store_qkv(o_ref, _silu(y), pl.program_id(1) < n_qk_tiles)


def _gconv_prompt(proj, conv_w, n_seq, t_len, d, tt=512, ct=1024):
    ch = 4 * d
    tt = _tile(t_len, tt)
    ct = _tile(ch, ct)
    nt = t_len // tt
    return pl.pallas_call(
        functools.partial(_gconv_prompt_kernel, 2 * d // ct),
        grid=(n_seq, ch // ct, nt),
        in_specs=[pl.BlockSpec((tt, ct), lambda b, j, t: (b * nt + t, j)),
                  pl.BlockSpec((4, ct), lambda b, j, t: (0, j))],
        out_specs=pl.BlockSpec((tt, ct), lambda b, j, t: (b * nt + t, j)),
        out_shape=jax.ShapeDtypeStruct((n_seq * t_len, ch), F32),
        scratch_shapes=[pltpu.VMEM((8, ct), F32)],
        compiler_params=_cparams("parallel", "parallel", "arbitrary"),
        name="gdn_conv_prompt",
    )(proj, conv_w)


def _gconv_sample_kernel(n_qk_tiles, nb, s0_ref, s1_ref, s2_ref, x_ref, w_ref, o_ref):
    u = x_ref[...]
    full = jnp.concatenate([s0_ref[...], s1_ref[...], s2_ref[...], u], axis=0)
    n = u.shape[0]
    w = w_ref[...]
    y = (w[0:1] * full[0:n] + w[1:2] * full[nb:nb + n] + w[2:3] * full[2 * nb:2 * nb + n]
         + w[3:4] * full[3 * nb:3 * nb + n])
    _store_qkv(o_ref, _silu(y), pl.program_id(0) < n_qk_tiles)


def _gconv_sample(proj, state2d, conv_w, row_block, nb, ns, d, ct=512):
    ch = 4 * d
    ct = _tile(ch, ct)
    nj = ch // ct
    st = lambda s: pl.BlockSpec((nb, ct), lambda j: (0, s * nj + j))
    return pl.pallas_call(
        functools.partial(_gconv_sample_kernel, 2 * d // ct, nb),
        grid=(nj,),
        in_specs=[st(0), st(1), st(2), pl.BlockSpec((ns, ct), lambda j: (row_block, j)),
                  pl.BlockSpec((4, ct), lambda j: (0, j))],
        out_specs=pl.BlockSpec((ns, ct), lambda j: (0, j)),
        out_shape=jax.ShapeDtypeStruct((ns, ch), F32),
        compiler_params=_cparams("parallel"),
        name="gdn_conv_sample",
    )(state2d, state2d, state2d, proj, conv_w)


def _gates_kernel(chunk, ba_ref, alog_ref, dtb_ref, beta_ref, gc_ref, gtot_ref, gct_ref):
    ba = ba_ref[...]
    nh = ba.shape[1] // 2
    r = ba.shape[0]
    beta_ref[...] = jax.nn.sigmoid(ba[:, :nh])
    g = -jnp.exp(alog_ref[...]) * _softplus(ba[:, nh:] + dtb_ref[...])
    ri = lax.broadcasted_iota(jnp.int32, (r, r), 0)
    ci = lax.broadcasted_iota(jnp.int32, (r, r), 1)
    same = _blk(ri, chunk) == _blk(ci, chunk)
    tri = jnp.where(same & (ri >= ci), 1.0, 0.0)
    blk = jnp.where(same, 1.0, 0.0)
    gc = jnp.dot(tri, g, precision=HIGHEST, preferred_element_type=F32)
    gc_ref[...] = gc
    gtot_ref[...] = jnp.dot(blk, g, precision=HIGHEST, preferred_element_type=F32)
    eye = jnp.where(lax.broadcasted_iota(jnp.int32, (nh, nh), 0) == lax.broadcasted_iota(jnp.int32, (nh, nh), 1),
                    1.0, 0.0)
    gct_ref[0] = lax.dot_general(eye, gc, (((1,), (1,)), ((), ())), precision=HIGHEST,
                                 preferred_element_type=F32)


def _gates(ba, a_log, dt_bias, chunk, r):
    n, h2 = ba.shape
    nh = h2 // 2
    row = pl.BlockSpec((r, nh), lambda i: (i, 0))
    vec = pl.BlockSpec((1, nh), lambda i: (0, 0))
    return pl.pallas_call(
        functools.partial(_gates_kernel, chunk),
        grid=(n // r,),
        in_specs=[pl.BlockSpec((r, h2), lambda i: (i, 0)), vec, vec],
        out_specs=[row, row, row, pl.BlockSpec((1, nh, r), lambda i: (i, 0, 0))],
        out_shape=[jax.ShapeDtypeStruct((n, nh), F32)] * 3 + [jax.ShapeDtypeStruct((n // r, nh, r), F32)],
        compiler_params=_cparams("parallel"),
        name="gdn_gates",
    )(ba, a_log.reshape(1, nh), dt_bias.reshape(1, nh))


def _pick_col(x, h):
    lane = lax.broadcasted_iota(jnp.int32, x.shape, 1)
    return jnp.sum(jnp.where(lane == h, x, 0.0), axis=1, keepdims=True)


def _chunk_terms(chunk, q, k, v, beta, gc, gtot, gc_row):
    r = q.shape[0]
    ri = lax.broadcasted_iota(jnp.int32, (r, r), 0)
    ci = lax.broadcasted_iota(jnp.int32, (r, r), 1)
    causal = (_blk(ri, chunk) == _blk(ci, chunk)) & (ri >= ci)
    decay = jnp.exp(jnp.where(causal, gc - gc_row, -jnp.inf))
    kb = k * beta
    low = jnp.where(ri > ci, _dot_nt(kb, k) * decay, 0.0)
    e = -jnp.where(_blk(ri, 2) == _blk(ci, 2), low, 0.0)
    s = 2
    while s < chunk:
        lm = jnp.where((_blk(ri, 2 * s) == _blk(ci, 2 * s)) & (_blk(ri, s) != _blk(ci, s)), low, 0.0)
        x = lm + _dot(e, lm)
        e = e - (x + _dot(x, e))
        s *= 2
    eg = jnp.exp(gc)
    rhs = jnp.concatenate([v * beta, kb * eg], axis=1)
    uw = rhs + _dot(e, rhs)
    dv = v.shape[1]
    qs = q * (HEAD ** -0.5)
    a_intra = _dot_nt(qs, k) * decay
    return uw[:, :dv], uw[:, dv:], a_intra, qs * eg, k * jnp.exp(gtot - gc)


def _gated_norm(o, z, gain):
    return (_rms(o, gain) * _silu(z)).astype(BF16)


def _gdn_prompt_kernel(chunk, hps, q_ref, k_ref, v_ref, z_ref, beta_ref, gc_ref, gtot_ref, gct_ref, gn_ref,
                       o_ref, s_ref, s_scr):
    @pl.when(pl.program_id(2) == 0)
    def _():
        s_scr[...] = jnp.zeros_like(s_scr)
    r = q_ref.shape[0]
    h0 = pl.program_id(1) * hps
    terms = []
    for hl in range(hps):
        kh = hl // 2
        q = q_ref[:, kh * HEAD:(kh + 1) * HEAD]
        k = k_ref[:, kh * HEAD:(kh + 1) * HEAD]
        v = v_ref[:, hl * HEAD:(hl + 1) * HEAD]
        beta = _pick_col(beta_ref[...], h0 + hl)
        gc = _pick_col(gc_ref[...], h0 + hl)
        gtot = _pick_col(gtot_ref[...], h0 + hl)
        gc_row = gct_ref[0, pl.ds(h0 + hl, 1), :]
        terms.append(_chunk_terms(chunk, q, k, v, beta, gc, gtot, gc_row) + (jnp.exp(gtot),))
    states = [s_scr[hl] for hl in range(hps)]
    outs = [[] for _ in range(hps)]
    for c in range(r // chunk):
        lo, hi = c * chunk, (c + 1) * chunk
        for hl in range(hps):
            u, w, a, qd, kd, egt = terms[hl]
            s = states[hl]
            pq = _dot(jnp.concatenate([w[lo:hi], qd[lo:hi]], axis=0), s)
            v_new = u[lo:hi] - pq[:chunk]
            outs[hl].append(pq[chunk:] + _dot(a[lo:hi, lo:hi], v_new))
            states[hl] = s * egt[lo:lo + 1] + _dot_tn(kd[lo:hi], v_new)
    for hl in range(hps):
        s_scr[hl] = states[hl]
        o = jnp.concatenate(outs[hl], axis=0)
        o_ref[:, hl * HEAD:(hl + 1) * HEAD] = _gated_norm(o, z_ref[:, hl * HEAD:(hl + 1) * HEAD], gn_ref[...])
    s_ref[0] = s_scr[...]


def _gdn_prompt(qkv, proj, beta, gc, gtot, gct, g_onorm, n_seq, t_len, d, chunk, r=256, hps=4):
    nvh = 2 * d // HEAD
    ns = t_len // r
    kc = hps // 2 * HEAD
    vc = hps * HEAD
    rows = lambda b, g, s: b * ns + s
    return pl.pallas_call(
        functools.partial(_gdn_prompt_kernel, chunk, hps),
        grid=(n_seq, nvh // hps, ns),
        in_specs=[pl.BlockSpec((r, kc), lambda b, g, s: (rows(b, g, s), g)),
                  pl.BlockSpec((r, kc), lambda b, g, s: (rows(b, g, s), d // kc + g)),
                  pl.BlockSpec((r, vc), lambda b, g, s: (rows(b, g, s), 2 * d // vc + g)),
                  pl.BlockSpec((r, vc), lambda b, g, s: (rows(b, g, s), 4 * d // vc + g)),
                  pl.BlockSpec((r, nvh), lambda b, g, s: (rows(b, g, s), 0)),
                  pl.BlockSpec((r, nvh), lambda b, g, s: (rows(b, g, s), 0)),
                  pl.BlockSpec((r, nvh), lambda b, g, s: (rows(b, g, s), 0)),
                  pl.BlockSpec((1, nvh, r), lambda b, g, s: (rows(b, g, s), 0, 0)),
                  pl.BlockSpec((1, HEAD), lambda b, g, s: (0, 0))],
        out_specs=[pl.BlockSpec((r, vc), lambda b, g, s: (rows(b, g, s), g)),
                   pl.BlockSpec((1, hps, HEAD, HEAD), lambda b, g, s: (b, g, 0, 0))],
        out_shape=[jax.ShapeDtypeStruct((n_seq * t_len, 2 * d), BF16),
                   jax.ShapeDtypeStruct((n_seq, nvh, HEAD, HEAD), F32)],
        scratch_shapes=[pltpu.VMEM((hps, HEAD, HEAD), F32)],
        compiler_params=_cparams("parallel", "parallel", "arbitrary"),
        name="gdn_prompt",
    )(qkv, qkv, qkv, proj, beta, gc, gtot, gct, g_onorm.reshape(1, HEAD))


def _gdn_sample_kernel(chunk, hps, q_ref, k_ref, v_ref, z_ref, beta_ref, gc_ref, gtot_ref, gct_ref, gn_ref, s0_ref,
                       o_ref, s_ref, wq_scr, kd_scr, eg_scr, pq_scr):
    r = q_ref.shape[0]
    h0 = pl.program_id(0) * hps
    per_row = 8 // chunk
    terms = []
    for hl in range(hps):
        kh = hl // 2
        q = q_ref[:, kh * HEAD:(kh + 1) * HEAD]
        k = k_ref[:, kh * HEAD:(kh + 1) * HEAD]
        v = v_ref[:, hl * HEAD:(hl + 1) * HEAD]
        beta = _pick_col(beta_ref[...], h0 + hl)
        gc = _pick_col(gc_ref[...], h0 + hl)
        gtot = _pick_col(gtot_ref[...], h0 + hl)
        gc_row = gct_ref[0, pl.ds(h0 + hl, 1), :]
        u, w, a, qd, kd = _chunk_terms(chunk, q, k, v, beta, gc, gtot, gc_row)
        wq_scr[hl, :, 0:8, :] = w.reshape(r // 8, 8, HEAD)
        wq_scr[hl, :, 8:16, :] = qd.reshape(r // 8, 8, HEAD)
        kd_scr[hl] = kd
        eg_scr[hl] = jnp.broadcast_to(jnp.exp(gtot), (r, HEAD))
        terms.append((u, a))

    row16 = lax.broadcasted_iota(jnp.int32, (16, HEAD), 0)
    row8 = lax.broadcasted_iota(jnp.int32, (8, HEAD), 0)

    def pair_pred(p, carry):
        for hl in range(hps):
            lhs = wq_scr[hl, p]
            acc = jnp.zeros((16, HEAD), F32)
            for e in range(per_row):
                pq = _dot(lhs, s0_ref[p * per_row + e, hl])
                acc = jnp.where(_blk(row16 & 7, chunk) == e, pq, acc)
            pq_scr[hl, p] = acc
        return carry
    lax.fori_loop(0, r // 8, pair_pred, 0)

    v_news = []
    for hl in range(hps):
        u, a = terms[hl]
        pq = pq_scr[hl]
        v_new = u - pq[:, 0:8, :].reshape(r, HEAD)
        o = pq[:, 8:16, :].reshape(r, HEAD) + _dot(a, v_new)
        o_ref[:, hl * HEAD:(hl + 1) * HEAD] = _gated_norm(o, z_ref[:, hl * HEAD:(hl + 1) * HEAD], gn_ref[...])
        pq_scr[hl, :, 0:8, :] = v_new.reshape(r // 8, 8, HEAD)

    def pair_update(p, carry):
        for hl in range(hps):
            v_new = pq_scr[hl, p, 0:8, :]
            kd = kd_scr[hl, pl.ds(pl.multiple_of(p * 8, 8), 8), :]
            for e in range(per_row):
                b = p * per_row + e
                decay = eg_scr[hl, pl.ds(p * 8 + e * chunk, 1), :]
                kd_e = jnp.where(_blk(row8, chunk) == e, kd, 0.0)
                s_ref[b, hl] = s0_ref[b, hl] * decay + _dot_tn(kd_e, v_new)
        return carry
    lax.fori_loop(0, r // 8, pair_update, 0)


def _gdn_sample(qkv, z, beta, gc, gtot, gct, g_onorm, state, d, chunk, r=128, hps=4):
    n = qkv.shape[0]
    nvh = 2 * d // HEAD
    kc = hps // 2 * HEAD
    vc = hps * HEAD
    nb = r // chunk
    return pl.pallas_call(
        functools.partial(_gdn_sample_kernel, chunk, hps),
        grid=(nvh // hps, n // r),
        in_specs=[pl.BlockSpec((r, kc), lambda g, s: (s, g)),
                  pl.BlockSpec((r, kc), lambda g, s: (s, d // kc + g)),
                  pl.BlockSpec((r, vc), lambda g, s: (s, 2 * d // vc + g)),
                  pl.BlockSpec((r, vc), lambda g, s: (s, g)),
                  pl.BlockSpec((r, nvh), lambda g, s: (s, 0)),
                  pl.BlockSpec((r, nvh), lambda g, s: (s, 0)),
                  pl.BlockSpec((r, nvh), lambda g, s: (s, 0)),
                  pl.BlockSpec((1, nvh, r), lambda g, s: (s, 0, 0)),
                  pl.BlockSpec((1, HEAD), lambda g, s: (0, 0)),
                  pl.BlockSpec((nb, hps, HEAD, HEAD), lambda g, s: (s, g, 0, 0))],
        out_specs=[pl.BlockSpec((r, vc), lambda g, s: (s, g)),
                   pl.BlockSpec((nb, hps, HEAD, HEAD), lambda g, s: (s, g, 0, 0))],
        out_shape=[jax.ShapeDtypeStruct((n, 2 * d), BF16),
                   jax.ShapeDtypeStruct(state.shape, F32)],
        scratch_shapes=[pltpu.VMEM((hps, r // 8, 16, HEAD), F32), pltpu.VMEM((hps, r, HEAD), F32),
                        pltpu.VMEM((hps, r, HEAD), F32), pltpu.VMEM((hps, r // 8, 16, HEAD), F32)],
        compiler_params=_cparams("parallel", "parallel"),
        name="gdn_sample",
    )(qkv, qkv, qkv, z, beta, gc, gtot, gct, g_onorm.reshape(1, HEAD), state)


def kernel(x_prompt, x_sample, c_prompt, c_sample, state_gdn, state_gdn_conv, state_sconv, w_ada, b_ada,
           g_norm_mix, g_norm_ffn, g_norm_out, gdn_w_in, gdn_conv_w, gdn_a_log, gdn_dt_bias, gdn_g_onorm,
           gdn_w_out, sc_w_in, sc_conv_w, sc_w_out, ffn_w_up, ffn_w_down, moe_w_router, moe_b_router,
           moe_w_up, moe_w_down):
    nbp, t_len, d = x_prompt.shape
    nbs, steps, _ = x_sample.shape
    n_p, n_s = nbp * t_len, nbs * steps
    nvh = 2 * d // HEAD
    chunk_p = min(64, t_len)
    tm = nbs
    assert t_len % tm == 0 and n_p % n_s == 0 and 8 % steps == 0

    x = jnp.concatenate([x_prompt.reshape(n_p, d), x_sample.transpose(1, 0, 2).reshape(n_s, d)], axis=0)
    pad = (-(nbp + nbs)) % 8
    c_all = jnp.concatenate([c_prompt, c_sample, jnp.zeros((pad, d), F32)], axis=0)
    m = _ada(c_all, w_ada, b_ada)
    mods = [_Mod(m[i, :nbp].reshape(nbp, 1, 6 * d), m[i, nbp:nbp + nbs], d, tm, n_p // tm, t_len // tm)
            for i in range(2)]
    SH1, SC1, GA1, SH2, SC2, GA2 = range(6)

    h = _prep(x, g_norm_mix[0], mods[0], SC1, SH1, tm)
    w_in = gdn_w_in[0]
    proj = _matmul(h, w_in, 6 * d, _tile(6 * d, 1024), name="gdn_in_proj")
    ba = _matmul(h, w_in[:, 6 * d:], 2 * nvh, 2 * nvh, name="gdn_in_proj_ba")
    conv_w = gdn_conv_w[0]
    qkv_p = _gconv_prompt(proj, conv_w, nbp, t_len, d)
    st_gconv = state_gdn_conv[0].reshape(nbs, 3 * 4 * d)
    qkv_s = _gconv_sample(proj, st_gconv, conv_w, n_p // n_s, nbs, n_s, d)
    to_seq_major = lambda a: a.reshape(steps, nbs, a.shape[-1]).transpose(1, 0, 2).reshape(n_s, a.shape[-1])
    qkv_s = to_seq_major(qkv_s)
    z_s = to_seq_major(proj[n_p:, 4 * d:6 * d])
    ba_s = to_seq_major(ba[n_p:])
    beta_p, gc_p, gtot_p, gct_p = _gates(ba[:n_p], gdn_a_log[0], gdn_dt_bias[0], chunk_p, 256)
    beta_s, gc_s, gtot_s, gct_s = _gates(ba_s, gdn_a_log[0], gdn_dt_bias[0], steps, 128)
    o_p, gdn_p = _gdn_prompt(qkv_p, proj, beta_p, gc_p, gtot_p, gct_p, gdn_g_onorm[0], nbp, t_len, d, chunk_p)
    o_s, gdn_s = _gdn_sample(qkv_s, z_s, beta_s, gc_s, gtot_s, gct_s, gdn_g_onorm[0], state_gdn[0], d, steps)
    to_time_major = lambda a: a.reshape(nbs, steps, a.shape[-1]).transpose(1, 0, 2).reshape(n_s, a.shape[-1])
    o = jnp.concatenate([o_p, to_time_major(o_s)], axis=0)
    y = _matmul(o, gdn_w_out[0], d, _tile(d, 512), name="gdn_out_proj")
    gconv_p = jnp.stack([proj[(b + 1) * t_len - 3:(b + 1) * t_len, :4 * d] for b in range(nbp)])
    gconv_s = proj[n_p + nbs:, :4 * d].reshape(3, nbs, 4 * d).transpose(1, 0, 2)

    x, h = _resid_prep(x, y, g_norm_ffn[0], mods[0], GA1, mods[0], SC2, SH2, tm)
    tm_ffn = _tile(n_p + n_s, 512)
    one_expert = jnp.zeros(((n_p + n_s) // tm_ffn,), jnp.int32)
    all_tiles = jnp.full((1,), (n_p + n_s) // tm_ffn, jnp.int32)
    act = _swiglu_up(h, ffn_w_up, one_expert, all_tiles, tm_ffn, _tile(ffn_w_down.shape[1], 512), "ffn_swiglu_up")
    f = _matmul(act, ffn_w_down[0], d, _tile(d, 512), name="ffn_down")

    x, h = _resid_prep(x, f, g_norm_mix[1], mods[0], GA2, mods[1], SC1, SH1, tm)
    s_in = _matmul(h, sc_w_in[0], 3 * d, _tile(3 * d, 1024), name="sconv_in_proj")
    mix_p, sconv_tail = _sconv_prompt(s_in, sc_conv_w[0], nbp, t_len, d)
    mix_s, u_s = _sconv_sample(s_in, state_sconv[0].reshape(nbs, 2 * d), sc_conv_w[0], n_p // n_s, nbs, n_s, d)
    y = _matmul(jnp.concatenate([mix_p, mix_s], axis=0), sc_w_out[0], d, _tile(d, 512), name="sconv_out_proj")
    sconv_p = sconv_tail[:, 6:8]
    sconv_s = u_s[n_s - 2 * nbs:].reshape(2, nbs, d).transpose(1, 0, 2)

    x, h, info = _resid_route(x, y, g_norm_ffn[1], mods[1], GA1, mods[1], SC2, SH2,
                              moe_w_router[0], moe_b_router[0], tm)
    _, n_exp, f_moe, _ = moe_w_down.shape
    tmg = 256
    pos, src, te, nv = _route_tables(info, n_exp, tmg)
    h_sorted = _gather_rows(h, src, nv, tmg)
    act = _swiglu_up(h_sorted, moe_w_up[0], te, nv, tmg, _tile(f_moe, 1024), "moe_swiglu_up")
    y_sorted = _grouped_down(act, moe_w_down[0], te, nv, tmg, _tile(d, 512), "moe_down")
    out = _final(x, y_sorted, pos, info, g_norm_out, mods[1], GA2, tm)

    y_prompt = out[:n_p].reshape(nbp, t_len, d)
    y_sample = out[n_p:].reshape(steps, nbs, d).transpose(1, 0, 2)
    return (y_prompt, y_sample, gdn_p[None], gconv_p[None], sconv_p[None], gdn_s[None], gconv_s[None], sconv_s[None])
```

```python
import functools

import jax
import jax.numpy as jnp
from jax import lax
from jax.experimental import pallas as pl
from jax.experimental.pallas import tpu as pltpu

F32 = jnp.float32
BF16 = jnp.bfloat16
NORM_EPS = 1e-6
HEAD = 128
V7X_VMEM_LIMIT = 56 * 1024 * 1024
HIGHEST = lax.Precision.HIGHEST


def _cparams(*sem):
    return pltpu.CompilerParams(dimension_semantics=sem, vmem_limit_bytes=V7X_VMEM_LIMIT)


def _tile(n, pref):
    return pref if n % pref == 0 else n


def _silu(x):
    return x * jax.nn.sigmoid(x)


def _softplus(x):
    return jnp.maximum(x, 0.0) + jnp.log1p(jnp.exp(-jnp.abs(x)))


def _blk(idx, size):
    assert size & (size - 1) == 0
    return lax.shift_right_logical(idx, size.bit_length() - 1)


def _dot(a, b):
    return jnp.dot(a.astype(BF16), b.astype(BF16), preferred_element_type=F32)


def _dot_nt(a, b):
    return lax.dot_general(a.astype(BF16), b.astype(BF16), (((1,), (1,)), ((), ())), preferred_element_type=F32)


def _dot_tn(a, b):
    return lax.dot_general(a.astype(BF16), b.astype(BF16), (((0,), (0,)), ((), ())), preferred_element_type=F32)


def _ada_kernel(c_ref, w_ref, b_ref, o_ref):
    c = c_ref[...]
    o_ref[0] = _dot(_silu(c), w_ref[0]) + b_ref[0]


def _ada(c_all, w_ada, b_ada):
    nl, d, n6 = w_ada.shape
    r = c_all.shape[0]
    tn = _tile(n6, 1024)
    return pl.pallas_call(
        _ada_kernel,
        grid=(nl, n6 // tn),
        in_specs=[pl.BlockSpec((r, d), lambda l, j: (0, 0)),
                  pl.BlockSpec((1, d, tn), lambda l, j: (l, 0, j)),
                  pl.BlockSpec((1, 1, tn), lambda l, j: (l, 0, j))],
        out_specs=pl.BlockSpec((1, r, tn), lambda l, j: (l, 0, j)),
        out_shape=jax.ShapeDtypeStruct((nl, r, n6), F32),
        compiler_params=_cparams("parallel", "parallel"),
        name="ada_mod",
    )(c_all, w_ada, b_ada.reshape(nl, 1, n6))


class _Mod:
    def __init__(self, mod_p, mod_s, d, tm, n_ptiles, tiles_per_seq):
        self.arrays = (mod_p, mod_s)
        self.n_ptiles = n_ptiles
        nb = mod_p.shape[0]

        def specs(chunk):
            return [pl.BlockSpec((1, 1, d), lambda i, *_: (jnp.minimum(i // tiles_per_seq, nb - 1), 0, chunk)),
                    pl.BlockSpec((tm, d), lambda i, *_: (0, chunk))]
        self.specs = specs

    def pick(self, p_ref, s_ref):
        is_prompt = pl.program_id(0) < self.n_ptiles
        return jnp.where(is_prompt, p_ref[0], s_ref[...])


def _rms(x, gain):
    ms = jnp.mean(x * x, axis=-1, keepdims=True)
    return x * lax.rsqrt(ms + NORM_EPS) * gain


def _prep_kernel(mod, x_ref, g_ref, scp, scs, shp, shs, h_ref):
    y = _rms(x_ref[...], g_ref[...])
    h_ref[...] = (y * (1.0 + mod.pick(scp, scs)) + mod.pick(shp, shs)).astype(h_ref.dtype)


def _prep(x, gain, mod, sc_chunk, sh_chunk, tm):
    n, d = x.shape
    row = pl.BlockSpec((tm, d), lambda i: (i, 0))
    return pl.pallas_call(
        functools.partial(_prep_kernel, mod),
        grid=(n // tm,),
        in_specs=[row, pl.BlockSpec((1, d), lambda i: (0, 0))] + mod.specs(sc_chunk) + mod.specs(sh_chunk),
        out_specs=row,
        out_shape=jax.ShapeDtypeStruct((n, d), BF16),
        compiler_params=_cparams("parallel"),
        name="norm_mod",
    )(x, gain.reshape(1, d), *mod.arrays, *mod.arrays)


def _resid_prep_kernel(mod, x_ref, y_ref, g_ref, gap, gas, scp, scs, shp, shs, xo_ref, h_ref):
    x = x_ref[...] + mod.pick(gap, gas) * y_ref[...]
    xo_ref[...] = x
    y = _rms(x, g_ref[...])
    h_ref[...] = (y * (1.0 + mod.pick(scp, scs)) + mod.pick(shp, shs)).astype(h_ref.dtype)


def _resid_prep(x, y, gain, mod_gate, ga_chunk, mod, sc_chunk, sh_chunk, tm):
    n, d = x.shape
    row = pl.BlockSpec((tm, d), lambda i: (i, 0))
    return pl.pallas_call(
        functools.partial(_resid_prep_kernel, mod),
        grid=(n // tm,),
        in_specs=[row, row, pl.BlockSpec((1, d), lambda i: (0, 0))]
        + mod_gate.specs(ga_chunk) + mod.specs(sc_chunk) + mod.specs(sh_chunk),
        out_specs=[row, row],
        out_shape=[jax.ShapeDtypeStruct((n, d), F32), jax.ShapeDtypeStruct((n, d), BF16)],
        compiler_params=_cparams("parallel"),
        name="resid_norm_mod",
    )(x, y, gain.reshape(1, d), *mod_gate.arrays, *mod.arrays, *mod.arrays)


def _resid_route_kernel(mod, n_exp, x_ref, y_ref, g_ref, gap, gas, scp, scs, shp, shs, wr_ref, br_ref,
                        xo_ref, h_ref, info_ref):
    x = x_ref[...] + mod.pick(gap, gas) * y_ref[...]
    xo_ref[...] = x
    h = _rms(x, g_ref[...]) * (1.0 + mod.pick(scp, scs)) + mod.pick(shp, shs)
    h_ref[...] = h
    logits = jnp.dot(h, wr_ref[...], precision=HIGHEST, preferred_element_type=F32) + br_ref[...]
    lane = lax.broadcasted_iota(jnp.int32, logits.shape, 1).astype(F32)
    m1 = jnp.max(logits, axis=-1, keepdims=True)
    i1 = jnp.min(jnp.where(logits == m1, lane, float(n_exp)), axis=-1, keepdims=True)
    rest = jnp.where(lane == i1, -jnp.inf, logits)
    m2 = jnp.max(rest, axis=-1, keepdims=True)
    i2 = jnp.min(jnp.where(rest == m2, lane, float(n_exp)), axis=-1, keepdims=True)
    e2 = jnp.exp(m2 - m1)
    g1 = 1.0 / (1.0 + e2)
    g2 = e2 / (1.0 + e2)
    info_ref[...] = (jnp.where(lane == 0.0, i1, 0.0) + jnp.where(lane == 1.0, i2, 0.0)
                     + jnp.where(lane == 2.0, g1, 0.0) + jnp.where(lane == 3.0, g2, 0.0))


def _resid_route(x, y, gain, mod_gate, ga_chunk, mod, sc_chunk, sh_chunk, w_router, b_router, tm):
    n, d = x.shape
    n_exp = w_router.shape[1]
    row = pl.BlockSpec((tm, d), lambda i: (i, 0))
    return pl.pallas_call(
        functools.partial(_resid_route_kernel, mod, n_exp),
        grid=(n // tm,),
        in_specs=[row, row, pl.BlockSpec((1, d), lambda i: (0, 0))]
        + mod_gate.specs(ga_chunk) + mod.specs(sc_chunk) + mod.specs(sh_chunk)
        + [pl.BlockSpec((d, n_exp), lambda i: (0, 0)), pl.BlockSpec((1, n_exp), lambda i: (0, 0))],
        out_specs=[row, row, pl.BlockSpec((tm, n_exp), lambda i: (i, 0))],
        out_shape=[jax.ShapeDtypeStruct((n, d), F32), jax.ShapeDtypeStruct((n, d), F32),
                   jax.ShapeDtypeStruct((n, n_exp), F32)],
        compiler_params=_cparams("parallel"),
        name="resid_norm_route",
    )(x, y, gain.reshape(1, d), *mod_gate.arrays, *mod.arrays, *mod.arrays, w_router, b_router.reshape(1, n_exp))


def _row_copy(src_hbm, row, dst, r, sem):
    return pltpu.make_async_copy(src_hbm.at[pl.ds(row, 1)], dst.at[pl.ds(r, 1)], sem)


def _final_kernel(mod, pos_ref, x_ref, y_hbm, info_ref, g_ref, gap, gas, o_ref, buf, sem):
    tm = x_ref.shape[0]
    base = pl.program_id(0) * (2 * tm)

    def issue(r, carry):
        for k in range(2):
            _row_copy(y_hbm, pos_ref[base + 2 * r + k], buf.at[k], r, sem.at[k]).start()
        return carry
    lax.fori_loop(0, tm, issue, 0, unroll=4)
    for k in range(2):
        pltpu.make_async_copy(y_hbm.at[pl.ds(0, tm)], buf.at[k], sem.at[k]).wait()
    info = info_ref[...]
    f = info[:, 2:3] * buf[0] + info[:, 3:4] * buf[1]
    x = x_ref[...] + mod.pick(gap, gas) * f
    o_ref[...] = _rms(x, g_ref[...])


def _final(x, y_sorted, pos, info, gain, mod_gate, ga_chunk, tm):
    n, d = x.shape
    row = pl.BlockSpec((tm, d), lambda i, *_: (i, 0))
    return pl.pallas_call(
        functools.partial(_final_kernel, mod_gate),
        grid_spec=pltpu.PrefetchScalarGridSpec(
            num_scalar_prefetch=1,
            grid=(n // tm,),
            in_specs=[row, pl.BlockSpec(memory_space=pl.ANY),
                      pl.BlockSpec((tm, info.shape[1]), lambda i, *_: (i, 0)),
                      pl.BlockSpec((1, d), lambda i, *_: (0, 0))] + mod_gate.specs(ga_chunk),
            out_specs=row,
            scratch_shapes=[pltpu.VMEM((2, tm, d), F32), pltpu.SemaphoreType.DMA((2,))]),
        out_shape=jax.ShapeDtypeStruct((n, d), F32),
        compiler_params=_cparams("arbitrary"),
        name="moe_combine_final_norm",
    )(pos, x, y_sorted, info, gain.reshape(1, d), *mod_gate.arrays)


def _route_tables(info, n_exp, tmg):
    n = info.shape[0]
    e = info[:, :2].astype(jnp.int32)
    onehot = (e[:, :, None] == jnp.arange(n_exp, dtype=jnp.int32)[None, None, :]).astype(jnp.int32).sum(axis=1)
    counts = onehot.sum(axis=0)
    padded = (counts + tmg - 1) // tmg * tmg
    ends = jnp.cumsum(padded)
    starts = ends - padded
    rank = jnp.cumsum(onehot, axis=0) - onehot
    pos = (starts[e] + jnp.take_along_axis(rank, e, axis=1)).reshape(-1).astype(jnp.int32)
    n_tiles = -(-(2 * n) // tmg) + n_exp
    src = jnp.zeros((n_tiles * tmg,), jnp.int32).at[pos].set(jnp.repeat(jnp.arange(n, dtype=jnp.int32), 2))
    n_valid = (ends[-1] // tmg).astype(jnp.int32)
    tiles = jnp.arange(n_tiles, dtype=jnp.int32)
    te = jnp.minimum((tiles[:, None] * tmg >= ends[None, :]).astype(jnp.int32).sum(axis=1), n_exp - 1)
    te = jnp.where(tiles < n_valid, te, te[jnp.maximum(n_valid - 1, 0)])
    return pos, src, te, n_valid.reshape(1)


def _gather_kernel(src_ref, nv_ref, h_hbm, o_ref, buf, sem):
    i = pl.program_id(0)
    tmg = buf.shape[0]

    @pl.when(i < nv_ref[0])
    def _():
        base = i * tmg

        def issue(r, carry):
            _row_copy(h_hbm, src_ref[base + r], buf, r, sem).start()
            return carry
        lax.fori_loop(0, tmg, issue, 0, unroll=8)
        pltpu.make_async_copy(h_hbm.at[pl.ds(0, tmg)], buf, sem).wait()
        o_ref[...] = buf[...].astype(o_ref.dtype)

    @pl.when(i >= nv_ref[0])
    def _():
        o_ref[...] = jnp.zeros_like(o_ref)


def _gather_rows(h, src, n_valid, tmg):
    d = h.shape[1]
    n_tiles = src.shape[0] // tmg
    return pl.pallas_call(
        _gather_kernel,
        grid_spec=pltpu.PrefetchScalarGridSpec(
            num_scalar_prefetch=2,
            grid=(n_tiles,),
            in_specs=[pl.BlockSpec(memory_space=pl.ANY)],
            out_specs=pl.BlockSpec((tmg, d), lambda i, src, nv: (i, 0)),
            scratch_shapes=[pltpu.VMEM((tmg, d), F32), pltpu.SemaphoreType.DMA(())]),
        out_shape=jax.ShapeDtypeStruct((n_tiles * tmg, d), BF16),
        compiler_params=_cparams("arbitrary"),
        name="moe_gather",
    )(src, n_valid, h)


def _mm_kernel(a_ref, w_ref, o_ref, wb_ref):
    @pl.when(pl.program_id(1) == 0)
    def _():
        wb_ref[...] = w_ref[...].astype(BF16)
    o_ref[...] = jnp.dot(a_ref[...], wb_ref[...], preferred_element_type=F32).astype(o_ref.dtype)


def _matmul(a, w, n_out, tn, tm=512, out_dtype=F32, name="matmul"):
    m, k = a.shape
    tm = _tile(m, tm)
    return pl.pallas_call(
        _mm_kernel,
        grid=(n_out // tn, m // tm),
        in_specs=[pl.BlockSpec((tm, k), lambda j, i: (i, 0)),
                  pl.BlockSpec((k, tn), lambda j, i: (0, j))],
        out_specs=pl.BlockSpec((tm, tn), lambda j, i: (i, j)),
        out_shape=jax.ShapeDtypeStruct((m, n_out), out_dtype),
        scratch_shapes=[pltpu.VMEM((k, tn), BF16)],
        compiler_params=_cparams("parallel", "arbitrary"),
        name=name,
    )(a, w)


def _new_expert(te_ref):
    i = pl.program_id(1)
    return (i == 0) | (te_ref[i] != te_ref[jnp.maximum(i - 1, 0)])


def _swiglu_up_kernel(te_ref, nv_ref, a_ref, wg_ref, wu_ref, o_ref, wgb_ref, wub_ref):
    @pl.when(_new_expert(te_ref))
    def _():
        wgb_ref[...] = wg_ref[0].astype(BF16)
        wub_ref[...] = wu_ref[0].astype(BF16)

    @pl.when(pl.program_id(1) < nv_ref[0])
    def _():
        a = a_ref[...]
        gate = jnp.dot(a, wgb_ref[...], preferred_element_type=F32)
        up = jnp.dot(a, wub_ref[...], preferred_element_type=F32)
        o_ref[...] = (_silu(gate) * up).astype(o_ref.dtype)

    @pl.when(pl.program_id(1) >= nv_ref[0])
    def _():
        o_ref[...] = jnp.zeros_like(o_ref)


def _swiglu_up(a, w_up, te, nv, tm, tn, name):
    m, k = a.shape
    f = w_up.shape[2] // 2
    nj = f // tn
    row = lambda j, i, te, nv: jnp.minimum(i, nv[0] - 1)
    return pl.pallas_call(
        _swiglu_up_kernel,
        grid_spec=pltpu.PrefetchScalarGridSpec(
            num_scalar_prefetch=2,
            grid=(nj, m // tm),
            in_specs=[pl.BlockSpec((tm, k), lambda j, i, te, nv: (row(j, i, te, nv), 0)),
                      pl.BlockSpec((1, k, tn), lambda j, i, te, nv: (te[i], 0, j)),
                      pl.BlockSpec((1, k, tn), lambda j, i, te, nv: (te[i], 0, nj + j))],
            out_specs=pl.BlockSpec((tm, tn), lambda j, i, te, nv: (i, j)),
            scratch_shapes=[pltpu.VMEM((k, tn), BF16), pltpu.VMEM((k, tn), BF16)]),
        out_shape=jax.ShapeDtypeStruct((m, f), BF16),
        compiler_params=_cparams("parallel", "arbitrary"),
        name=name,
    )(te, nv, a, w_up, w_up)


def _down_kernel(te_ref, nv_ref, a_ref, w_ref, o_ref, wb_ref):
    @pl.when(_new_expert(te_ref))
    def _():
        wb_ref[...] = w_ref[0].astype(BF16)

    @pl.when(pl.program_id(1) < nv_ref[0])
    def _():
        o_ref[...] = jnp.dot(a_ref[...], wb_ref[...], preferred_element_type=F32)

    @pl.when(pl.program_id(1) >= nv_ref[0])
    def _():
        o_ref[...] = jnp.zeros_like(o_ref)


def _grouped_down(a, w_down, te, nv, tm, tn, name):
    m, f = a.shape
    d = w_down.shape[2]
    row = lambda j, i, te, nv: jnp.minimum(i, nv[0] - 1)
    return pl.pallas_call(
        _down_kernel,
        grid_spec=pltpu.PrefetchScalarGridSpec(
            num_scalar_prefetch=2,
            grid=(d // tn, m // tm),
            in_specs=[pl.BlockSpec((tm, f), lambda j, i, te, nv: (row(j, i, te, nv), 0)),
                      pl.BlockSpec((1, f, tn), lambda j, i, te, nv: (te[i], 0, j))],
            out_specs=pl.BlockSpec((tm, tn), lambda j, i, te, nv: (i, j)),
            scratch_shapes=[pltpu.VMEM((f, tn), BF16)]),
        out_shape=jax.ShapeDtypeStruct((m, d), F32),
        compiler_params=_cparams("parallel", "arbitrary"),
        name=name,
    )(te, nv, a, w_down)


def _shift_rows(carry, u, j):
    ext = jnp.concatenate([carry, u], axis=0)
    return pltpu.roll(ext, j, 0)[8:]


def _sconv_prompt_kernel(bg_ref, cg_ref, xi_ref, w_ref, o_ref, st_ref, carry_ref):
    @pl.when(pl.program_id(1) == 0)
    def _():
        carry_ref[...] = jnp.zeros_like(carry_ref)
    u = cg_ref[...] * xi_ref[...]
    carry = carry_ref[...]
    w = w_ref[...]
    y = w[2:3] * u + w[1:2] * _shift_rows(carry, u, 1) + w[0:1] * _shift_rows(carry, u, 2)
    o_ref[...] = (bg_ref[...] * y).astype(o_ref.dtype)
    tail = u[u.shape[0] - 8:]
    carry_ref[...] = tail
    st_ref[0] = tail


def _sconv_prompt(s_in, conv_w, n_seq, t_len, d, tt=256):
    tt = _tile(t_len, tt)
    nt = t_len // tt
    blk = lambda c: pl.BlockSpec((tt, d), lambda b, t: (b * nt + t, c))
    return pl.pallas_call(
        _sconv_prompt_kernel,
        grid=(n_seq, nt),
        in_specs=[blk(0), blk(1), blk(2), pl.BlockSpec((3, d), lambda b, t: (0, 0))],
        out_specs=[pl.BlockSpec((tt, d), lambda b, t: (b * nt + t, 0)),
                   pl.BlockSpec((1, 8, d), lambda b, t: (b, 0, 0))],
        out_shape=[jax.ShapeDtypeStruct((n_seq * t_len, d), BF16), jax.ShapeDtypeStruct((n_seq, 8, d), F32)],
        scratch_shapes=[pltpu.VMEM((8, d), F32)],
        compiler_params=_cparams("parallel", "arbitrary"),
        name="sconv_prompt",
    )(s_in, s_in, s_in, conv_w)


def _sconv_sample_kernel(nb, bg_ref, cg_ref, xi_ref, s0_ref, s1_ref, w_ref, o_ref, u_ref):
    u = cg_ref[...] * xi_ref[...]
    full = jnp.concatenate([s0_ref[...], s1_ref[...], u], axis=0)
    n = u.shape[0]
    w = w_ref[...]
    y = w[0:1] * full[0:n] + w[1:2] * full[nb:nb + n] + w[2:3] * full[2 * nb:2 * nb + n]
    o_ref[...] = (bg_ref[...] * y).astype(o_ref.dtype)
    u_ref[...] = u


def _sconv_sample(s_in, state2d, conv_w, row_block, nb, ns, d, ct=512):
    ct = _tile(d, ct)
    nj = d // ct
    blk = lambda c: pl.BlockSpec((ns, ct), lambda j: (row_block, c * nj + j))
    st = lambda s: pl.BlockSpec((nb, ct), lambda j: (0, s * nj + j))
    return pl.pallas_call(
        functools.partial(_sconv_sample_kernel, nb),
        grid=(nj,),
        in_specs=[blk(0), blk(1), blk(2), st(0), st(1), pl.BlockSpec((3, ct), lambda j: (0, j))],
        out_specs=[pl.BlockSpec((ns, ct), lambda j: (0, j)), pl.BlockSpec((ns, ct), lambda j: (0, j))],
        out_shape=[jax.ShapeDtypeStruct((ns, d), BF16), jax.ShapeDtypeStruct((ns, d), F32)],
        compiler_params=_cparams("parallel"),
        name="sconv_sample",
    )(s_in, s_in, s_in, state2d, state2d, conv_w)


def _l2norm_heads(x):
    parts = []
    for h in range(x.shape[1] // HEAD):
        seg = x[:, h * HEAD:(h + 1) * HEAD]
        parts.append(seg * lax.rsqrt(jnp.sum(seg * seg, axis=-1, keepdims=True) + NORM_EPS))
    return jnp.concatenate(parts, axis=1) if len(parts) > 1 else parts[0]


def _store_qkv(o_ref, act, is_qk):
    @pl.when(is_qk)
    def _():
        o_ref[...] = _l2norm_heads(act)

    @pl.when(jnp.logical_not(is_qk))
    def _():
        o_ref[...] = act


def _gconv_prompt_kernel(n_qk_tiles, x_ref, w_ref, o_ref, carry_ref):
    @pl.when(pl.program_id(2) == 0)
    def _():
        carry_ref[...] = jnp.zeros_like(carry_ref)
    u = x_ref[...]
    carry = carry_ref[...]
    w = w_ref[...]
    y = (w[3:4] * u + w[2:3] * _shift_rows(carry, u, 1) + w[1:2] * _shift_rows(carry, u, 2)
         + w[0:1] * _shift_rows(carry, u, 3))
    carry_ref[...] = u[u.shape[0] - 8:]
    _store_qkv(o_ref, _silu(y), pl.program_id(1) < n_qk_tiles)


def _gconv_prompt(proj, conv_w, n_seq, t_len, d, tt=512, ct=1024):
    ch = 4 * d
    tt = _tile(t_len, tt)
    ct = _tile(ch, ct)
    nt = t_len // tt
    return pl.pallas_call(
        functools.partial(_gconv_prompt_kernel, 2 * d // ct),
        grid=(n_seq, ch // ct, nt),
        in_specs=[pl.BlockSpec((tt, ct), lambda b, j, t: (b * nt + t, j)),
                  pl.BlockSpec((4, ct), lambda b, j, t: (0, j))],
        out_specs=pl.BlockSpec((tt, ct), lambda b, j, t: (b * nt + t, j)),
        out_shape=jax.ShapeDtypeStruct((n_seq * t_len, ch), F32),
        scratch_shapes=[pltpu.VMEM((8, ct), F32)],
        compiler_params=_cparams("parallel", "parallel", "arbitrary"),
        name="gdn_conv_prompt",
    )(proj, conv_w)


def _gconv_sample_kernel(n_qk_tiles, nb, s0_ref, s1_ref, s2_ref, x_ref, w_ref, o_ref):
    u = x_ref[...]
    full = jnp.concatenate([s0_ref[...], s1_ref[...], s2_ref[...], u], axis=0)
    n = u.shape[0]
    w = w_ref[...]
    y = (w[0:1] * full[0:n] + w[1:2] * full[nb:nb + n] + w[2:3] * full[2 * nb:2 * nb + n]
         + w[3:4] * full[3 * nb:3 * nb + n])
    _store_qkv(o_ref, _silu(y), pl.program_id(0) < n_qk_tiles)


def _gconv_sample(proj, state2d, conv_w, row_block, nb, ns, d, ct=512):
    ch = 4 * d
    ct = _tile(ch, ct)
    nj = ch // ct
    st = lambda s: pl.BlockSpec((nb, ct), lambda j: (0, s * nj + j))
    return pl.pallas_call(
        functools.partial(_gconv_sample_kernel, 2 * d // ct, nb),
        grid=(nj,),
        in_specs=[st(0), st(1), st(2), pl.BlockSpec((ns, ct), lambda j: (row_block, j)),
                  pl.BlockSpec((4, ct), lambda j: (0, j))],
        out_specs=pl.BlockSpec((ns, ct), lambda j: (0, j)),
        out_shape=jax.ShapeDtypeStruct((ns, ch), F32),
        compiler_params=_cparams("parallel"),
        name="gdn_conv_sample",
    )(state2d, state2d, state2d, proj, conv_w)


def _gates_kernel(chunk, ba_ref, alog_ref, dtb_ref, beta_ref, gc_ref, gtot_ref, gct_ref):
    ba = ba_ref[...]
    nh = ba.shape[1] // 2
    r = ba.shape[0]
    beta_ref[...] = jax.nn.sigmoid(ba[:, :nh])
    g = -jnp.exp(alog_ref[...]) * _softplus(ba[:, nh:] + dtb_ref[...])
    ri = lax.broadcasted_iota(jnp.int32, (r, r), 0)
    ci = lax.broadcasted_iota(jnp.int32, (r, r), 1)
    same = _blk(ri, chunk) == _blk(ci, chunk)
    tri = jnp.where(same & (ri >= ci), 1.0, 0.0)
    blk = jnp.where(same, 1.0, 0.0)
    gc = jnp.dot(tri, g, precision=HIGHEST, preferred_element_type=F32)
    gc_ref[...] = gc
    gtot_ref[...] = jnp.dot(blk, g, precision=HIGHEST, preferred_element_type=F32)
    eye = jnp.where(lax.broadcasted_iota(jnp.int32, (nh, nh), 0) == lax.broadcasted_iota(jnp.int32, (nh, nh), 1),
                    1.0, 0.0)
    gct_ref[0] = lax.dot_general(eye, gc, (((1,), (1,)), ((), ())), precision=HIGHEST,
                                 preferred_element_type=F32)


def _gates(ba, a_log, dt_bias, chunk, r):
    n, h2 = ba.shape
    nh = h2 // 2
    row = pl.BlockSpec((r, nh), lambda i: (i, 0))
    vec = pl.BlockSpec((1, nh), lambda i: (0, 0))
    return pl.pallas_call(
        functools.partial(_gates_kernel, chunk),
        grid=(n // r,),
        in_specs=[pl.BlockSpec((r, h2), lambda i: (i, 0)), vec, vec],
        out_specs=[row, row, row, pl.BlockSpec((1, nh, r), lambda i: (i, 0, 0))],
        out_shape=[jax.ShapeDtypeStruct((n, nh), F32)] * 3 + [jax.ShapeDtypeStruct((n // r, nh, r), F32)],
        compiler_params=_cparams("parallel"),
        name="gdn_gates",
    )(ba, a_log.reshape(1, nh), dt_bias.reshape(1, nh))


def _pick_col(x, h):
    lane = lax.broadcasted_iota(jnp.int32, x.shape, 1)
    return jnp.sum(jnp.where(lane == h, x, 0.0), axis=1, keepdims=True)


def _chunk_terms(chunk, q, k, v, beta, gc, gtot, gc_row):
    r = q.shape[0]
    ri = lax.broadcasted_iota(jnp.int32, (r, r), 0)
    ci = lax.broadcasted_iota(jnp.int32, (r, r), 1)
    causal = (_blk(ri, chunk) == _blk(ci, chunk)) & (ri >= ci)
    decay = jnp.exp(jnp.where(causal, gc - gc_row, -jnp.inf))
    kb = k * beta
    low = jnp.where(ri > ci, _dot_nt(kb, k) * decay, 0.0)
    e = -jnp.where(_blk(ri, 2) == _blk(ci, 2), low, 0.0)
    s = 2
    while s < chunk:
        lm = jnp.where((_blk(ri, 2 * s) == _blk(ci, 2 * s)) & (_blk(ri, s) != _blk(ci, s)), low, 0.0)
        x = lm + _dot(e, lm)
        e = e - (x + _dot(x, e))
        s *= 2
    eg = jnp.exp(gc)
    rhs = jnp.concatenate([v * beta, kb * eg], axis=1)
    uw = rhs + _dot(e, rhs)
    dv = v.shape[1]
    qs = q * (HEAD ** -0.5)
    a_intra = _dot_nt(qs, k) * decay
    return uw[:, :dv], uw[:, dv:], a_intra, qs * eg, k * jnp.exp(gtot - gc)


def _paired_chunk_terms(chunk, heads):
    r, pair = heads[0][0].shape[0], 2 * chunk
    ii = lax.broadcasted_iota(jnp.int32, (chunk, pair), 0)
    lane = lax.broadcasted_iota(jnp.int32, (chunk, pair), 1)
    right = lane >= chunk
    jj = lane & (chunk - 1)
    diag = lambda full: jnp.where(right, full[chunk:], full[:chunk])
    spread = lambda y: jnp.concatenate([jnp.where(right, 0.0, y), jnp.where(right, y, 0.0)], axis=0)
    slices = [slice(p * pair, (p + 1) * pair) for p in range(r // pair)]
    kbs = [k * beta for (q, k, v, beta, gc, gtot, gc_row) in heads]
    egs = [jnp.exp(gc) for (q, k, v, beta, gc, gtot, gc_row) in heads]
    qss = [q * (HEAD ** -0.5) for (q, k, v, beta, gc, gtot, gc_row) in heads]
    rhss = [jnp.concatenate([h[2] * h[3], kb * eg], axis=1) for h, kb, eg in zip(heads, kbs, egs)]
    probs = [(i, sl) for i in range(len(heads)) for sl in slices]
    decays = []
    for i, sl in probs:
        gcp, gc_row = heads[i][4][sl], heads[i][6]
        decays.append(jnp.exp(jnp.where(ii >= jj, jnp.where(right, gcp[chunk:], gcp[:chunk]) - gc_row[:, sl],
                                        -jnp.inf)))
    kks = [_dot_nt(kbs[i][sl], heads[i][1][sl]) for i, sl in probs]
    lows = [jnp.where(ii > jj, diag(kk) * decay, 0.0) for kk, decay in zip(kks, decays)]
    es = [-jnp.where(_blk(ii, 2) == _blk(jj, 2), low, 0.0) for low in lows]
    s = 2
    while s < chunk:
        level = (_blk(ii, 2 * s) == _blk(jj, 2 * s)) & (_blk(ii, s) != _blk(jj, s))
        lms = [jnp.where(level, low, 0.0) for low in lows]
        lm_sp = [spread(lm).astype(BF16) for lm in lms]
        xs = [lm + _dot(e, sp) for lm, e, sp in zip(lms, es, lm_sp)]
        e_sp = [spread(e).astype(BF16) for e in es]
        es = [e - (x + _dot(x, sp)) for e, x, sp in zip(es, xs, e_sp)]
        s *= 2
    e_sp = [spread(e).astype(BF16) for e in es]
    uws = [rhss[i][sl] + _dot(sp, rhss[i][sl]) for (i, sl), sp in zip(probs, e_sp)]
    qks = [_dot_nt(qss[i][sl], heads[i][1][sl]) for i, sl in probs]
    atts = [diag(qk) * decay for qk, decay in zip(qks, decays)]
    out = []
    n_pairs = len(slices)
    for i, (q, k, v, beta, gc, gtot, gc_row) in enumerate(heads):
        uw = jnp.concatenate(uws[i * n_pairs:(i + 1) * n_pairs], axis=0)
        a_chunks = []
        for a in atts[i * n_pairs:(i + 1) * n_pairs]:
            a_chunks += [jnp.where(right, 0.0, a), jnp.where(right, a, 0.0)]
        dv = v.shape[1]
        out.append((uw[:, :dv], uw[:, dv:], a_chunks, qss[i] * egs[i], k * jnp.exp(gtot - gc)))
    return out


def _gated_norm(o, z, gain):
    return (_rms(o, gain) * _silu(z)).astype(BF16)


def _gdn_prompt_kernel(chunk, hps, q_ref, k_ref, v_ref, z_ref, beta_ref, gc_ref, gtot_ref, gct_ref, gn_ref,
                       o_ref, s_ref, s_scr):
    @pl.when(pl.program_id(2) == 0)
    def _():
        s_scr[...] = jnp.zeros_like(s_scr)
    r = q_ref.shape[0]
    h0 = pl.program_id(1) * hps
    paired = 2 * chunk == HEAD and r % HEAD == 0
    heads = []
    for hl in range(hps):
        kh = hl // 2
        heads.append((q_ref[:, kh * HEAD:(kh + 1) * HEAD], k_ref[:, kh * HEAD:(kh + 1) * HEAD],
                      v_ref[:, hl * HEAD:(hl + 1) * HEAD], _pick_col(beta_ref[...], h0 + hl),
                      _pick_col(gc_ref[...], h0 + hl), _pick_col(gtot_ref[...], h0 + hl),
                      gct_ref[0, pl.ds(h0 + hl, 1), :]))
    if paired:
        terms = _paired_chunk_terms(chunk, heads)
    else:
        terms = [_chunk_terms(chunk, *head) for head in heads]
    terms = [t + (jnp.exp(head[5]),) for t, head in zip(terms, heads)]
    states = [s_scr[hl] for hl in range(hps)]
    outs = [[] for _ in range(hps)]
    for c in range(r // chunk):
        lo, hi = c * chunk, (c + 1) * chunk
        for hl in range(hps):
            u, w, a, qd, kd, egt = terms[hl]
            s = states[hl]
            pq = _dot(jnp.concatenate([w[lo:hi], qd[lo:hi]], axis=0), s)
            v_new = u[lo:hi] - pq[:chunk]
            if paired:
                intra = _dot(a[c], jnp.concatenate([v_new, v_new], axis=0))
            else:
                intra = _dot(a[lo:hi, lo:hi], v_new)
            outs[hl].append(pq[chunk:] + intra)
            states[hl] = s * egt[lo:lo + 1] + _dot_tn(kd[lo:hi], v_new)
    for hl in range(hps):
        s_scr[hl] = states[hl]
        o = jnp.concatenate(outs[hl], axis=0)
        o_ref[:, hl * HEAD:(hl + 1) * HEAD] = _gated_norm(o, z_ref[:, hl * HEAD:(hl + 1) * HEAD], gn_ref[...])
    s_ref[0] = s_scr[...]


def _gdn_prompt(qkv, proj, beta, gc, gtot, gct, g_onorm, n_seq, t_len, d, chunk, r=256, hps=8):
    nvh = 2 * d // HEAD
    ns = t_len // r
    kc = hps // 2 * HEAD
    vc = hps * HEAD
    rows = lambda b, g, s: b * ns + s
    return pl.pallas_call(
        functools.partial(_gdn_prompt_kernel, chunk, hps),
        grid=(n_seq, nvh // hps, ns),
        in_specs=[pl.BlockSpec((r, kc), lambda b, g, s: (rows(b, g, s), g)),
                  pl.BlockSpec((r, kc), lambda b, g, s: (rows(b, g, s), d // kc + g)),
                  pl.BlockSpec((r, vc), lambda b, g, s: (rows(b, g, s), 2 * d // vc + g)),
                  pl.BlockSpec((r, vc), lambda b, g, s: (rows(b, g, s), 4 * d // vc + g)),
                  pl.BlockSpec((r, nvh), lambda b, g, s: (rows(b, g, s), 0)),
                  pl.BlockSpec((r, nvh), lambda b, g, s: (rows(b, g, s), 0)),
                  pl.BlockSpec((r, nvh), lambda b, g, s: (rows(b, g, s), 0)),
                  pl.BlockSpec((1, nvh, r), lambda b, g, s: (rows(b, g, s), 0, 0)),
                  pl.BlockSpec((1, HEAD), lambda b, g, s: (0, 0))],
        out_specs=[pl.BlockSpec((r, vc), lambda b, g, s: (rows(b, g, s), g)),
                   pl.BlockSpec((1, hps, HEAD, HEAD), lambda b, g, s: (b, g, 0, 0))],
        out_shape=[jax.ShapeDtypeStruct((n_seq * t_len, 2 * d), BF16),
                   jax.ShapeDtypeStruct((n_seq, nvh, HEAD, HEAD), F32)],
        scratch_shapes=[pltpu.VMEM((hps, HEAD, HEAD), F32)],
        compiler_params=_cparams("parallel", "parallel", "arbitrary"),
        name="gdn_prompt",
    )(qkv, qkv, qkv, proj, beta, gc, gtot, gct, g_onorm.reshape(1, HEAD))


def _gdn_sample_kernel(chunk, hps, q_ref, k_ref, v_ref, z_ref, beta_ref, gc_ref, gtot_ref, gct_ref, gn_ref, s0_ref,
                       o_ref, s_ref, wq_scr, kd_scr, eg_scr, pq_scr):
    r = q_ref.shape[0]
    h0 = pl.program_id(0) * hps
    per_row = 8 // chunk
    terms = []
    for hl in range(hps):
        kh = hl // 2
        q = q_ref[:, kh * HEAD:(kh + 1) * HEAD]
        k = k_ref[:, kh * HEAD:(kh + 1) * HEAD]
        v = v_ref[:, hl * HEAD:(hl + 1) * HEAD]
        beta = _pick_col(beta_ref[...], h0 + hl)
        gc = _pick_col(gc_ref[...], h0 + hl)
        gtot = _pick_col(gtot_ref[...], h0 + hl)
        gc_row = gct_ref[0, pl.ds(h0 + hl, 1), :]
        u, w, a, qd, kd = _chunk_terms(chunk, q, k, v, beta, gc, gtot, gc_row)
        wq_scr[hl, :, 0:8, :] = w.reshape(r // 8, 8, HEAD)
        wq_scr[hl, :, 8:16, :] = qd.reshape(r // 8, 8, HEAD)
        kd_scr[hl] = kd
        eg_scr[hl] = jnp.broadcast_to(jnp.exp(gtot), (r, HEAD))
        terms.append((u, a))

    row16 = lax.broadcasted_iota(jnp.int32, (16, HEAD), 0)
    row8 = lax.broadcasted_iota(jnp.int32, (8, HEAD), 0)

    def pair_pred(p, carry):
        for hl in range(hps):
            lhs = wq_scr[hl, p]
            acc = jnp.zeros((16, HEAD), F32)
            for e in range(per_row):
                pq = _dot(lhs, s0_ref[p * per_row + e, hl])
                acc = jnp.where(_blk(row16 & 7, chunk) == e, pq, acc)
            pq_scr[hl, p] = acc
        return carry
    lax.fori_loop(0, r // 8, pair_pred, 0)

    v_news = []
    for hl in range(hps):
        u, a = terms[hl]
        pq = pq_scr[hl]
        v_new = u - pq[:, 0:8, :].reshape(r, HEAD)
        o = pq[:, 8:16, :].reshape(r, HEAD) + _dot(a, v_new)
        o_ref[:, hl * HEAD:(hl + 1) * HEAD] = _gated_norm(o, z_ref[:, hl * HEAD:(hl + 1) * HEAD], gn_ref[...])
        pq_scr[hl, :, 0:8, :] = v_new.reshape(r // 8, 8, HEAD)

    def pair_update(p, carry):
        for hl in range(hps):
            v_new = pq_scr[hl, p, 0:8, :]
            kd = kd_scr[hl, pl.ds(pl.multiple_of(p * 8, 8), 8), :]
            for e in range(per_row):
                b = p * per_row + e
                decay = eg_scr[hl, pl.ds(p * 8 + e * chunk, 1), :]
                kd_e = jnp.where(_blk(row8, chunk) == e, kd, 0.0)
                s_ref[b, hl] = s0_ref[b, hl] * decay + _dot_tn(kd_e, v_new)
        return carry
    lax.fori_loop(0, r // 8, pair_update, 0)


def _gdn_sample(qkv, z, beta, gc, gtot, gct, g_onorm, state, d, chunk, r=128, hps=4):
    n = qkv.shape[0]
    nvh = 2 * d // HEAD
    kc = hps // 2 * HEAD
    vc = hps * HEAD
    nb = r // chunk
    return pl.pallas_call(
        functools.partial(_gdn_sample_kernel, chunk, hps),
        grid=(nvh // hps, n // r),
        in_specs=[pl.BlockSpec((r, kc), lambda g, s: (s, g)),
                  pl.BlockSpec((r, kc), lambda g, s: (s, d // kc + g)),
                  pl.BlockSpec((r, vc), lambda g, s: (s, 2 * d // vc + g)),
                  pl.BlockSpec((r, vc), lambda g, s: (s, g)),
                  pl.BlockSpec((r, nvh), lambda g, s: (s, 0)),
                  pl.BlockSpec((r, nvh), lambda g, s: (s, 0)),
                  pl.BlockSpec((r, nvh), lambda g, s: (s, 0)),
                  pl.BlockSpec((1, nvh, r), lambda g, s: (s, 0, 0)),
                  pl.BlockSpec((1, HEAD), lambda g, s: (0, 0)),
                  pl.BlockSpec((nb, hps, HEAD, HEAD), lambda g, s: (s, g, 0, 0))],
        out_specs=[pl.BlockSpec((r, vc), lambda g, s: (s, g)),
                   pl.BlockSpec((nb, hps, HEAD, HEAD), lambda g, s: (s, g, 0, 0))],
        out_shape=[jax.ShapeDtypeStruct((n, 2 * d), BF16),
                   jax.ShapeDtypeStruct(state.shape, F32)],
        scratch_shapes=[pltpu.VMEM((hps, r // 8, 16, HEAD), F32), pltpu.VMEM((hps, r, HEAD), F32),
                        pltpu.VMEM((hps, r, HEAD), F32), pltpu.VMEM((hps, r // 8, 16, HEAD), F32)],
        compiler_params=_cparams("parallel", "parallel"),
        name="gdn_sample",
    )(qkv, qkv, qkv, z, beta, gc, gtot, gct, g_onorm.reshape(1, HEAD), state)


def kernel(x_prompt, x_sample, c_prompt, c_sample, state_gdn, state_gdn_conv, state_sconv, w_ada, b_ada,
           g_norm_mix, g_norm_ffn, g_norm_out, gdn_w_in, gdn_conv_w, gdn_a_log, gdn_dt_bias, gdn_g_onorm,
           gdn_w_out, sc_w_in, sc_conv_w, sc_w_out, ffn_w_up, ffn_w_down, moe_w_router, moe_b_router,
           moe_w_up, moe_w_down):
    nbp, t_len, d = x_prompt.shape
    nbs, steps, _ = x_sample.shape
    n_p, n_s = nbp * t_len, nbs * steps
    nvh = 2 * d // HEAD
    chunk_p = min(64, t_len)
    tm = nbs
    assert t_len % tm == 0 and n_p % n_s == 0 and 8 % steps == 0

    x = jnp.concatenate([x_prompt.reshape(n_p, d), x_sample.transpose(1, 0, 2).reshape(n_s, d)], axis=0)
    pad = (-(nbp + nbs)) % 8
    c_all = jnp.concatenate([c_prompt, c_sample, jnp.zeros((pad, d), F32)], axis=0)
    m = _ada(c_all, w_ada, b_ada)
    mods = [_Mod(m[i, :nbp].reshape(nbp, 1, 6 * d), m[i, nbp:nbp + nbs], d, tm, n_p // tm, t_len // tm)
            for i in range(2)]
    SH1, SC1, GA1, SH2, SC2, GA2 = range(6)

    h = _prep(x, g_norm_mix[0], mods[0], SC1, SH1, tm)
    w_in = gdn_w_in[0]
    proj = _matmul(h, w_in, 6 * d, _tile(6 * d, 1024), name="gdn_in_proj")
    ba = _matmul(h, w_in[:, 6 * d:], 2 * nvh, 2 * nvh, name="gdn_in_proj_ba")
    conv_w = gdn_conv_w[0]
    qkv_p = _gconv_prompt(proj, conv_w, nbp, t_len, d)
    st_gconv = state_gdn_conv[0].reshape(nbs, 3 * 4 * d)
    qkv_s = _gconv_sample(proj, st_gconv, conv_w, n_p // n_s, nbs, n_s, d)
    to_seq_major = lambda a: a.reshape(steps, nbs, a.shape[-1]).transpose(1, 0, 2).reshape(n_s, a.shape[-1])
    qkv_s = to_seq_major(qkv_s)
    z_s = to_seq_major(proj[n_p:, 4 * d:6 * d])
    ba_s = to_seq_major(ba[n_p:])
    beta_p, gc_p, gtot_p, gct_p = _gates(ba[:n_p], gdn_a_log[0], gdn_dt_bias[0], chunk_p, 256)
    beta_s, gc_s, gtot_s, gct_s = _gates(ba_s, gdn_a_log[0], gdn_dt_bias[0], steps, 128)
    o_p, gdn_p = _gdn_prompt(qkv_p, proj, beta_p, gc_p, gtot_p, gct_p, gdn_g_onorm[0], nbp, t_len, d, chunk_p)
    o_s, gdn_s = _gdn_sample(qkv_s, z_s, beta_s, gc_s, gtot_s, gct_s, gdn_g_onorm[0], state_gdn[0], d, steps)
    to_time_major = lambda a: a.reshape(nbs, steps, a.shape[-1]).transpose(1, 0, 2).reshape(n_s, a.shape[-1])
    o = jnp.concatenate([o_p, to_time_major(o_s)], axis=0)
    y = _matmul(o, gdn_w_out[0], d, _tile(d, 512), name="gdn_out_proj")
    gconv_p = jnp.stack([proj[(b + 1) * t_len - 3:(b + 1) * t_len, :4 * d] for b in range(nbp)])
    gconv_s = proj[n_p + nbs:, :4 * d].reshape(3, nbs, 4 * d).transpose(1, 0, 2)

    x, h = _resid_prep(x, y, g_norm_ffn[0], mods[0], GA1, mods[0], SC2, SH2, tm)
    tm_ffn = _tile(n_p + n_s, 512)
    one_expert = jnp.zeros(((n_p + n_s) // tm_ffn,), jnp.int32)
    all_tiles = jnp.full((1,), (n_p + n_s) // tm_ffn, jnp.int32)
    act = _swiglu_up(h, ffn_w_up, one_expert, all_tiles, tm_ffn, _tile(ffn_w_down.shape[1], 512), "ffn_swiglu_up")
    f = _matmul(act, ffn_w_down[0], d, _tile(d, 512), name="ffn_down")

    x, h = _resid_prep(x, f, g_norm_mix[1], mods[0], GA2, mods[1], SC1, SH1, tm)
    s_in = _matmul(h, sc_w_in[0], 3 * d, _tile(3 * d, 1024), name="sconv_in_proj")
    mix_p, sconv_tail = _sconv_prompt(s_in, sc_conv_w[0], nbp, t_len, d)
    mix_s, u_s = _sconv_sample(s_in, state_sconv[0].reshape(nbs, 2 * d), sc_conv_w[0], n_p // n_s, nbs, n_s, d)
    y = _matmul(jnp.concatenate([mix_p, mix_s], axis=0), sc_w_out[0], d, _tile(d, 512), name="sconv_out_proj")
    sconv_p = sconv_tail[:, 6:8]
    sconv_s = u_s[n_s - 2 * nbs:].reshape(2, nbs, d).transpose(1, 0, 2)

    x, h, info = _resid_route(x, y, g_norm_ffn[1], mods[1], GA1, mods[1], SC2, SH2,
                              moe_w_router[0], moe_b_router[0], tm)
    _, n_exp, f_moe, _ = moe_w_down.shape
    tmg = 256
    pos, src, te, nv = _route_tables(info, n_exp, tmg)
    h_sorted = _gather_rows(h, src, nv, tmg)
    act = _swiglu_up(h_sorted, moe_w_up[0], te, nv, tmg, _tile(f_moe, 1024), "moe_swiglu_up")
    y_sorted = _grouped_down(act, moe_w_down[0], te, nv, tmg, _tile(d, 512), "moe_down")
    out = _final(x, y_sorted, pos, info, g_norm_out, mods[1], GA2, tm)

    y_prompt = out[:n_p].reshape(nbp, t_len, d)
    y_sample = out[n_p:].reshape(steps, nbs, d).transpose(1, 0, 2)
    return (y_prompt, y_sample, gdn_p[None], gconv_p[None], sconv_p[None], gdn_s[None], gconv_s[None], sconv_s[None])
```

```python
import functools

import jax
import jax.numpy as jnp
from jax import lax
from jax.experimental import pallas as pl
from jax.experimental.pallas import tpu as pltpu

F32 = jnp.float32
BF16 = jnp.bfloat16
NORM_EPS = 1e-6
HEAD = 128
V7X_VMEM_LIMIT = 56 * 1024 * 1024
HIGHEST = lax.Precision.HIGHEST


def _cparams(*sem):
    return pltpu.CompilerParams(dimension_semantics=sem, vmem_limit_bytes=V7X_VMEM_LIMIT)


def _tile(n, pref):
    return pref if n % pref == 0 else n


def _silu(x):
    return x * jax.nn.sigmoid(x)


def _softplus(x):
    return jnp.maximum(x, 0.0) + jnp.log1p(jnp.exp(-jnp.abs(x)))


def _blk(idx, size):
    assert size & (size - 1) == 0
    return lax.shift_right_logical(idx, size.bit_length() - 1)


def _dot(a, b):
    return jnp.dot(a.astype(BF16), b.astype(BF16), preferred_element_type=F32)


def _dot_nt(a, b):
    return lax.dot_general(a.astype(BF16), b.astype(BF16), (((1,), (1,)), ((), ())), preferred_element_type=F32)


def _dot_tn(a, b):
    return lax.dot_general(a.astype(BF16), b.astype(BF16), (((0,), (0,)), ((), ())), preferred_element_type=F32)


def _ada_kernel(c_ref, w_ref, b_ref, o_ref):
    c = c_ref[...]
    o_ref[0] = _dot(_silu(c), w_ref[0]) + b_ref[0]


def _ada(c_all, w_ada, b_ada):
    nl, d, n6 = w_ada.shape
    r = c_all.shape[0]
    tn = _tile(n6, 1024)
    return pl.pallas_call(
        _ada_kernel,
        grid=(nl, n6 // tn),
        in_specs=[pl.BlockSpec((r, d), lambda l, j: (0, 0)),
                  pl.BlockSpec((1, d, tn), lambda l, j: (l, 0, j)),
                  pl.BlockSpec((1, 1, tn), lambda l, j: (l, 0, j))],
        out_specs=pl.BlockSpec((1, r, tn), lambda l, j: (l, 0, j)),
        out_shape=jax.ShapeDtypeStruct((nl, r, n6), F32),
        compiler_params=_cparams("parallel", "parallel"),
        name="ada_mod",
    )(c_all, w_ada, b_ada.reshape(nl, 1, n6))


class _Mod:
    def __init__(self, mod_p, mod_s, d, tm, n_ptiles, tiles_per_seq):
        self.arrays = (mod_p, mod_s)
        self.n_ptiles = n_ptiles
        nb = mod_p.shape[0]

        def specs(chunk):
            return [pl.BlockSpec((1, 1, d), lambda i, *_: (jnp.minimum(i // tiles_per_seq, nb - 1), 0, chunk)),
                    pl.BlockSpec((tm, d), lambda i, *_: (0, chunk))]
        self.specs = specs
        self.row_specs = [pl.BlockSpec((tm, d), lambda i, *_: (jnp.minimum(i, n_ptiles - 1), 0)),
                          pl.BlockSpec((tm, d), lambda i, *_: (0, jnp.maximum(i - n_ptiles, 0)))]

    def is_prompt(self):
        return pl.program_id(0) < self.n_ptiles

    def pick(self, p_ref, s_ref):
        return jnp.where(self.is_prompt(), p_ref[0], s_ref[...])

    def pick_rows(self, p_ref, s_ref):
        return jnp.where(self.is_prompt(), p_ref[...], s_ref[...])


def _rms(x, gain):
    ms = jnp.mean(x * x, axis=-1, keepdims=True)
    return x * lax.rsqrt(ms + NORM_EPS) * gain


def _prep_kernel(mod, xp_ref, xs_ref, g_ref, scp, scs, shp, shs, h_ref):
    y = _rms(mod.pick_rows(xp_ref, xs_ref), g_ref[...])
    h_ref[...] = (y * (1.0 + mod.pick(scp, scs)) + mod.pick(shp, shs)).astype(h_ref.dtype)


def _prep(x_pair, gain, mod, sc_chunk, sh_chunk, n, tm):
    d = gain.shape[0]
    return pl.pallas_call(
        functools.partial(_prep_kernel, mod),
        grid=(n // tm,),
        in_specs=mod.row_specs + [pl.BlockSpec((1, d), lambda i: (0, 0))] + mod.specs(sc_chunk) + mod.specs(sh_chunk),
        out_specs=pl.BlockSpec((tm, d), lambda i: (i, 0)),
        out_shape=jax.ShapeDtypeStruct((n, d), BF16),
        compiler_params=_cparams("parallel"),
        name="norm_mod",
    )(*x_pair, gain.reshape(1, d), *mod.arrays, *mod.arrays)


def _resid_prep_kernel(mod, paired, *refs):
    if paired:
        xp_ref, xs_ref, y_ref, g_ref, gap, gas, scp, scs, shp, shs, xo_ref, h_ref = refs
        x = mod.pick_rows(xp_ref, xs_ref)
    else:
        x_ref, y_ref, g_ref, gap, gas, scp, scs, shp, shs, xo_ref, h_ref = refs
        x = x_ref[...]
    x = x + mod.pick(gap, gas) * y_ref[...]
    xo_ref[...] = x
    y = _rms(x, g_ref[...])
    h_ref[...] = (y * (1.0 + mod.pick(scp, scs)) + mod.pick(shp, shs)).astype(h_ref.dtype)


def _resid_prep(x, y, gain, mod_gate, ga_chunk, mod, sc_chunk, sh_chunk, tm):
    n, d = y.shape
    row = pl.BlockSpec((tm, d), lambda i: (i, 0))
    paired = isinstance(x, tuple)
    return pl.pallas_call(
        functools.partial(_resid_prep_kernel, mod, paired),
        grid=(n // tm,),
        in_specs=(mod.row_specs if paired else [row]) + [row, pl.BlockSpec((1, d), lambda i: (0, 0))]
        + mod_gate.specs(ga_chunk) + mod.specs(sc_chunk) + mod.specs(sh_chunk),
        out_specs=[row, row],
        out_shape=[jax.ShapeDtypeStruct((n, d), F32), jax.ShapeDtypeStruct((n, d), BF16)],
        compiler_params=_cparams("parallel"),
        name="resid_norm_mod",
    )(*(x if paired else (x,)), y, gain.reshape(1, d), *mod_gate.arrays, *mod.arrays, *mod.arrays)


def _resid_route_kernel(mod, n_exp, x_ref, y_ref, g_ref, gap, gas, scp, scs, shp, shs, wr_ref, br_ref,
                        xo_ref, h_ref, info_ref):
    x = x_ref[...] + mod.pick(gap, gas) * y_ref[...]
    xo_ref[...] = x
    h = _rms(x, g_ref[...]) * (1.0 + mod.pick(scp, scs)) + mod.pick(shp, shs)
    h_ref[...] = h
    logits = jnp.dot(h, wr_ref[...], precision=HIGHEST, preferred_element_type=F32) + br_ref[...]
    lane = lax.broadcasted_iota(jnp.int32, logits.shape, 1).astype(F32)
    m1 = jnp.max(logits, axis=-1, keepdims=True)
    i1 = jnp.min(jnp.where(logits == m1, lane, float(n_exp)), axis=-1, keepdims=True)
    rest = jnp.where(lane == i1, -jnp.inf, logits)
    m2 = jnp.max(rest, axis=-1, keepdims=True)
    i2 = jnp.min(jnp.where(rest == m2, lane, float(n_exp)), axis=-1, keepdims=True)
    e2 = jnp.exp(m2 - m1)
    g1 = 1.0 / (1.0 + e2)
    g2 = e2 / (1.0 + e2)
    info_ref[...] = (jnp.where(lane == 0.0, i1, 0.0) + jnp.where(lane == 1.0, i2, 0.0)
                     + jnp.where(lane == 2.0, g1, 0.0) + jnp.where(lane == 3.0, g2, 0.0))


def _resid_route(x, y, gain, mod_gate, ga_chunk, mod, sc_chunk, sh_chunk, w_router, b_router, tm):
    n, d = x.shape
    n_exp = w_router.shape[1]
    row = pl.BlockSpec((tm, d), lambda i: (i, 0))
    return pl.pallas_call(
        functools.partial(_resid_route_kernel, mod, n_exp),
        grid=(n // tm,),
        in_specs=[row, row, pl.BlockSpec((1, d), lambda i: (0, 0))]
        + mod_gate.specs(ga_chunk) + mod.specs(sc_chunk) + mod.specs(sh_chunk)
        + [pl.BlockSpec((d, n_exp), lambda i: (0, 0)), pl.BlockSpec((1, n_exp), lambda i: (0, 0))],
        out_specs=[row, row, pl.BlockSpec((tm, n_exp), lambda i: (i, 0))],
        out_shape=[jax.ShapeDtypeStruct((n, d), F32), jax.ShapeDtypeStruct((n, d), F32),
                   jax.ShapeDtypeStruct((n, n_exp), F32)],
        compiler_params=_cparams("parallel"),
        name="resid_norm_route",
    )(x, y, gain.reshape(1, d), *mod_gate.arrays, *mod.arrays, *mod.arrays, w_router, b_router.reshape(1, n_exp))


def _row_copy(src_hbm, row, dst, r, sem):
    return pltpu.make_async_copy(src_hbm.at[pl.ds(row, 1)], dst.at[pl.ds(r, 1)], sem)


def _final_kernel(mod, pos_ref, x_ref, y_hbm, info_ref, g_ref, gap, gas, op_ref, os_ref, buf, sem):
    tm = x_ref.shape[0]
    i = pl.program_id(0)
    n_tiles = pl.num_programs(0)

    def issue(tile, slot):
        def body(r, carry):
            for k in range(2):
                _row_copy(y_hbm, pos_ref[tile * (2 * tm) + 2 * r + k], buf.at[slot, k], r, sem.at[slot, k]).start()
            return carry
        lax.fori_loop(0, tm, body, 0, unroll=4)

    @pl.when(i == 0)
    def _():
        issue(0, 0)

    for slot in range(2):
        @pl.when((i + 1 < n_tiles) & ((i + 1) % 2 == slot))
        def _():
            issue(i + 1, slot)

    for slot in range(2):
        @pl.when(i % 2 == slot)
        def _():
            for k in range(2):
                pltpu.make_async_copy(y_hbm.at[pl.ds(0, tm)], buf.at[slot, k], sem.at[slot, k]).wait()
            info = info_ref[...]
            f = info[:, 2:3] * buf[slot, 0] + info[:, 3:4] * buf[slot, 1]
            x = x_ref[...] + mod.pick(gap, gas) * f
            out = _rms(x, g_ref[...])

            @pl.when(mod.is_prompt())
            def _():
                op_ref[...] = out

            @pl.when(jnp.logical_not(mod.is_prompt()))
            def _():
                os_ref[...] = out


def _final(x, y_sorted, pos, info, gain, mod_gate, ga_chunk, n_p, nbs, tm):
    n, d = x.shape
    row = pl.BlockSpec((tm, d), lambda i, *_: (i, 0))
    return pl.pallas_call(
        functools.partial(_final_kernel, mod_gate),
        grid_spec=pltpu.PrefetchScalarGridSpec(
            num_scalar_prefetch=1,
            grid=(n // tm,),
            in_specs=[row, pl.BlockSpec(memory_space=pl.ANY),
                      pl.BlockSpec((tm, info.shape[1]), lambda i, *_: (i, 0)),
                      pl.BlockSpec((1, d), lambda i, *_: (0, 0))] + mod_gate.specs(ga_chunk),
            out_specs=mod_gate.row_specs,
            scratch_shapes=[pltpu.VMEM((2, 2, tm, d), F32), pltpu.SemaphoreType.DMA((2, 2))]),
        out_shape=[jax.ShapeDtypeStruct((n_p, d), F32), jax.ShapeDtypeStruct((nbs, (n - n_p) // nbs * d), F32)],
        compiler_params=_cparams("arbitrary"),
        name="moe_combine_final_norm",
    )(pos, x, y_sorted, info, gain.reshape(1, d), *mod_gate.arrays)


def _route_tables(info, n_exp, tmg):
    n = info.shape[0]
    e = info[:, :2].astype(jnp.int32)
    onehot = (e[:, :, None] == jnp.arange(n_exp, dtype=jnp.int32)[None, None, :]).astype(jnp.int32).sum(axis=1)
    counts = onehot.sum(axis=0)
    padded = (counts + tmg - 1) // tmg * tmg
    ends = jnp.cumsum(padded)
    starts = ends - padded
    rank = jnp.cumsum(onehot, axis=0) - onehot
    pos = (starts[e] + jnp.take_along_axis(rank, e, axis=1)).reshape(-1).astype(jnp.int32)
    n_tiles = -(-(2 * n) // tmg) + n_exp
    src = jnp.zeros((n_tiles * tmg,), jnp.int32).at[pos].set(jnp.repeat(jnp.arange(n, dtype=jnp.int32), 2))
    n_valid = (ends[-1] // tmg).astype(jnp.int32)
    tiles = jnp.arange(n_tiles, dtype=jnp.int32)
    te = jnp.minimum((tiles[:, None] * tmg >= ends[None, :]).astype(jnp.int32).sum(axis=1), n_exp - 1)
    te = jnp.where(tiles < n_valid, te, te[jnp.maximum(n_valid - 1, 0)])
    return pos, src, te, n_valid.reshape(1)


def _gather_kernel(src_ref, nv_ref, h_hbm, o_ref, buf, sem):
    i = pl.program_id(0)
    tmg = buf.shape[1]
    nv = nv_ref[0]

    def issue(tile, slot):
        def body(r, carry):
            _row_copy(h_hbm, src_ref[tile * tmg + r], buf.at[slot], r, sem.at[slot]).start()
            return carry
        lax.fori_loop(0, tmg, body, 0, unroll=8)

    @pl.when((i == 0) & (nv > 0))
    def _():
        issue(0, 0)

    for slot in range(2):
        @pl.when((i + 1 < nv) & ((i + 1) % 2 == slot))
        def _():
            issue(i + 1, slot)

    for slot in range(2):
        @pl.when((i < nv) & (i % 2 == slot))
        def _():
            pltpu.make_async_copy(h_hbm.at[pl.ds(0, tmg)], buf.at[slot], sem.at[slot]).wait()
            o_ref[...] = buf[slot].astype(o_ref.dtype)

    @pl.when(i >= nv)
    def _():
        o_ref[...] = jnp.zeros_like(o_ref)


def _gather_rows(h, src, n_valid, tmg):
    d = h.shape[1]
    n_tiles = src.shape[0] // tmg
    return pl.pallas_call(
        _gather_kernel,
        grid_spec=pltpu.PrefetchScalarGridSpec(
            num_scalar_prefetch=2,
            grid=(n_tiles,),
            in_specs=[pl.BlockSpec(memory_space=pl.ANY)],
            out_specs=pl.BlockSpec((tmg, d), lambda i, src, nv: (i, 0)),
            scratch_shapes=[pltpu.VMEM((2, tmg, d), F32), pltpu.SemaphoreType.DMA((2,))]),
        out_shape=jax.ShapeDtypeStruct((n_tiles * tmg, d), BF16),
        compiler_params=_cparams("arbitrary"),
        name="moe_gather",
    )(src, n_valid, h)


def _mm_kernel(a_ref, w_ref, o_ref, wb_ref):
    @pl.when(pl.program_id(1) == 0)
    def _():
        wb_ref[...] = w_ref[...].astype(BF16)
    o_ref[...] = jnp.dot(a_ref[...], wb_ref[...], preferred_element_type=F32).astype(o_ref.dtype)


def _matmul(a, w, n_out, tn, tm=512, out_dtype=F32, name="matmul"):
    m, k = a.shape
    tm = _tile(m, tm)
    return pl.pallas_call(
        _mm_kernel,
        grid=(n_out // tn, m // tm),
        in_specs=[pl.BlockSpec((tm, k), lambda j, i: (i, 0)),
                  pl.BlockSpec((k, tn), lambda j, i: (0, j))],
        out_specs=pl.BlockSpec((tm, tn), lambda j, i: (i, j)),
        out_shape=jax.ShapeDtypeStruct((m, n_out), out_dtype),
        scratch_shapes=[pltpu.VMEM((k, tn), BF16)],
        compiler_params=_cparams("parallel", "arbitrary"),
        name=name,
    )(a, w)


def _new_expert(te_ref):
    i = pl.program_id(1)
    return (i == 0) | (te_ref[i] != te_ref[jnp.maximum(i - 1, 0)])


def _swiglu_up_kernel(te_ref, nv_ref, a_ref, wg_ref, wu_ref, o_ref, wgb_ref, wub_ref):
    @pl.when(_new_expert(te_ref))
    def _():
        wgb_ref[...] = wg_ref[0].astype(BF16)
        wub_ref[...] = wu_ref[0].astype(BF16)

    @pl.when(pl.program_id(1) < nv_ref[0])
    def _():
        a = a_ref[...]
        gate = jnp.dot(a, wgb_ref[...], preferred_element_type=F32)
        up = jnp.dot(a, wub_ref[...], preferred_element_type=F32)
        o_ref[...] = (_silu(gate) * up).astype(o_ref.dtype)

    @pl.when(pl.program_id(1) >= nv_ref[0])
    def _():
        o_ref[...] = jnp.zeros_like(o_ref)


def _swiglu_up(a, w_up, te, nv, tm, tn, name):
    m, k = a.shape
    f = w_up.shape[2] // 2
    nj = f // tn
    row = lambda j, i, te, nv: jnp.minimum(i, nv[0] - 1)
    return pl.pallas_call(
        _swiglu_up_kernel,
        grid_spec=pltpu.PrefetchScalarGridSpec(
            num_scalar_prefetch=2,
            grid=(nj, m // tm),
            in_specs=[pl.BlockSpec((tm, k), lambda j, i, te, nv: (row(j, i, te, nv), 0)),
                      pl.BlockSpec((1, k, tn), lambda j, i, te, nv: (te[i], 0, j)),
                      pl.BlockSpec((1, k, tn), lambda j, i, te, nv: (te[i], 0, nj + j))],
            out_specs=pl.BlockSpec((tm, tn), lambda j, i, te, nv: (i, j)),
            scratch_shapes=[pltpu.VMEM((k, tn), BF16), pltpu.VMEM((k, tn), BF16)]),
        out_shape=jax.ShapeDtypeStruct((m, f), BF16),
        compiler_params=_cparams("parallel", "arbitrary"),
        name=name,
    )(te, nv, a, w_up, w_up)


def _down_kernel(te_ref, nv_ref, a_ref, w_ref, o_ref, wb_ref):
    @pl.when(_new_expert(te_ref))
    def _():
        wb_ref[...] = w_ref[0].astype(BF16)

    @pl.when(pl.program_id(1) < nv_ref[0])
    def _():
        o_ref[...] = jnp.dot(a_ref[...], wb_ref[...], preferred_element_type=F32)

    @pl.when(pl.program_id(1) >= nv_ref[0])
    def _():
        o_ref[...] = jnp.zeros_like(o_ref)


def _grouped_down(a, w_down, te, nv, tm, tn, name):
    m, f = a.shape
    d = w_down.shape[2]
    row = lambda j, i, te, nv: jnp.minimum(i, nv[0] - 1)
    return pl.pallas_call(
        _down_kernel,
        grid_spec=pltpu.PrefetchScalarGridSpec(
            num_scalar_prefetch=2,
            grid=(d // tn, m // tm),
            in_specs=[pl.BlockSpec((tm, f), lambda j, i, te, nv: (row(j, i, te, nv), 0)),
                      pl.BlockSpec((1, f, tn), lambda j, i, te, nv: (te[i], 0, j))],
            out_specs=pl.BlockSpec((tm, tn), lambda j, i, te, nv: (i, j)),
            scratch_shapes=[pltpu.VMEM((f, tn), BF16)]),
        out_shape=jax.ShapeDtypeStruct((m, d), F32),
        compiler_params=_cparams("parallel", "arbitrary"),
        name=name,
    )(te, nv, a, w_down)


def _shift_rows(carry, u, j):
    ext = jnp.concatenate([carry, u], axis=0)
    return pltpu.roll(ext, j, 0)[8:]


def _sconv_prompt_kernel(bg_ref, cg_ref, xi_ref, w_ref, o_ref, st_ref, carry_ref):
    @pl.when(pl.program_id(1) == 0)
    def _():
        carry_ref[...] = jnp.zeros_like(carry_ref)
    u = cg_ref[...] * xi_ref[...]
    carry = carry_ref[...]
    w = w_ref[...]
    y = w[2:3] * u + w[1:2] * _shift_rows(carry, u, 1) + w[0:1] * _shift_rows(carry, u, 2)
    o_ref[...] = (bg_ref[...] * y).astype(o_ref.dtype)
    tail = u[u.shape[0] - 8:]
    carry_ref[...] = tail
    st_ref[0] = tail


def _sconv_prompt(s_in, conv_w, n_seq, t_len, d, tt=256):
    tt = _tile(t_len, tt)
    nt = t_len // tt
    blk = lambda c: pl.BlockSpec((tt, d), lambda b, t: (b * nt + t, c))
    return pl.pallas_call(
        _sconv_prompt_kernel,
        grid=(n_seq, nt),
        in_specs=[blk(0), blk(1), blk(2), pl.BlockSpec((3, d), lambda b, t: (0, 0))],
        out_specs=[pl.BlockSpec((tt, d), lambda b, t: (b * nt + t, 0)),
                   pl.BlockSpec((1, 8, d), lambda b, t: (b, 0, 0))],
        out_shape=[jax.ShapeDtypeStruct((n_seq * t_len, d), BF16), jax.ShapeDtypeStruct((n_seq, 8, d), F32)],
        scratch_shapes=[pltpu.VMEM((8, d), F32)],
        compiler_params=_cparams("parallel", "arbitrary"),
        name="sconv_prompt",
    )(s_in, s_in, s_in, conv_w)


def _sconv_sample_kernel(nb, bg_ref, cg_ref, xi_ref, s0_ref, s1_ref, w_ref, o_ref, u_ref):
    u = cg_ref[...] * xi_ref[...]
    full = jnp.concatenate([s0_ref[...], s1_ref[...], u], axis=0)
    n = u.shape[0]
    w = w_ref[...]
    y = w[0:1] * full[0:n] + w[1:2] * full[nb:nb + n] + w[2:3] * full[2 * nb:2 * nb + n]
    o_ref[...] = (bg_ref[...] * y).astype(o_ref.dtype)
    u_ref[...] = u


def _sconv_sample(s_in, state2d, conv_w, row_block, nb, ns, d, ct=512):
    ct = _tile(d, ct)
    nj = d // ct
    blk = lambda c: pl.BlockSpec((ns, ct), lambda j: (row_block, c * nj + j))
    st = lambda s: pl.BlockSpec((nb, ct), lambda j: (0, s * nj + j))
    return pl.pallas_call(
        functools.partial(_sconv_sample_kernel, nb),
        grid=(nj,),
        in_specs=[blk(0), blk(1), blk(2), st(0), st(1), pl.BlockSpec((3, ct), lambda j: (0, j))],
        out_specs=[pl.BlockSpec((ns, ct), lambda j: (0, j)), pl.BlockSpec((ns, ct), lambda j: (0, j))],
        out_shape=[jax.ShapeDtypeStruct((ns, d), BF16), jax.ShapeDtypeStruct((ns, d), F32)],
        compiler_params=_cparams("parallel"),
        name="sconv_sample",
    )(s_in, s_in, s_in, state2d, state2d, conv_w)


def _l2norm_heads(x):
    parts = []
    for h in range(x.shape[1] // HEAD):
        seg = x[:, h * HEAD:(h + 1) * HEAD]
        parts.append(seg * lax.rsqrt(jnp.sum(seg * seg, axis=-1, keepdims=True) + NORM_EPS))
    return jnp.concatenate(parts, axis=1) if len(parts) > 1 else parts[0]


def _store_qkv(o_ref, act, is_qk):
    @pl.when(is_qk)
    def _():
        o_ref[...] = _l2norm_heads(act)

    @pl.when(jnp.logical_not(is_qk))
    def _():
        o_ref[...] = act


def _gconv_prompt_kernel(n_qk_tiles, x_ref, w_ref, o_ref, carry_ref):
    @pl.when(pl.program_id(2) == 0)
    def _():
        carry_ref[...] = jnp.zeros_like(carry_ref)
    u = x_ref[...]
    carry = carry_ref[...]
    w = w_ref[...]
    y = (w[3:4] * u + w[2:3] * _shift_rows(carry, u, 1) + w[1:2] * _shift_rows(carry, u, 2)
         + w[0:1] * _shift_rows(carry, u, 3))
    carry_ref[...] = u[u.shape[0] - 8:]
    _store_qkv(o_ref, _silu(y), pl.program_id(1) < n_qk_tiles)


def _gconv_prompt(proj, conv_w, n_seq, t_len, d, tt=512, ct=1024):
    ch = 4 * d
    tt = _tile(t_len, tt)
    ct = _tile(ch, ct)
    nt = t_len // tt
    return pl.pallas_call(
        functools.partial(_gconv_prompt_kernel, 2 * d // ct),
        grid=(n_seq, ch // ct, nt),
        in_specs=[pl.BlockSpec((tt, ct), lambda b, j, t: (b * nt + t, j)),
                  pl.BlockSpec((4, ct), lambda b, j, t: (0, j))],
        out_specs=pl.BlockSpec((tt, ct), lambda b, j, t: (b * nt + t, j)),
        out_shape=jax.ShapeDtypeStruct((n_seq * t_len, ch), F32),
        scratch_shapes=[pltpu.VMEM((8, ct), F32)],
        compiler_params=_cparams("parallel", "parallel", "arbitrary"),
        name="gdn_conv_prompt",
    )(proj, conv_w)


def _gconv_sample_kernel(n_qk_tiles, nb, s0_ref, s1_ref, s2_ref, x_ref, w_ref, o_ref):
    u = x_ref[...]
    full = jnp.concatenate([s0_ref[...], s1_ref[...], s2_ref[...], u], axis=0)
    n = u.shape[0]
    w = w_ref[...]
    y = (w[0:1] * full[0:n] + w[1:2] * full[nb:nb + n] + w[2:3] * full[2 * nb:2 * nb + n]
         + w[3:4] * full[3 * nb:3 * nb + n])
    _store_qkv(o_ref, _silu(y), pl.program_id(0) < n_qk_tiles)


def _gconv_sample(proj, state2d, conv_w, row_block, nb, ns, d, ct=512):
    ch = 4 * d
    ct = _tile(ch, ct)
    nj = ch // ct
    st = lambda s: pl.BlockSpec((nb, ct), lambda j: (0, s * nj + j))
    return pl.pallas_call(
        functools.partial(_gconv_sample_kernel, 2 * d // ct, nb),
        grid=(nj,),
        in_specs=[st(0), st(1), st(2), pl.BlockSpec((ns, ct), lambda j: (row_block, j)),
                  pl.BlockSpec((4, ct), lambda j: (0, j))],
        out_specs=pl.BlockSpec((ns, ct), lambda j: (0, j)),
        out_shape=jax.ShapeDtypeStruct((ns, ch), F32),
        compiler_params=_cparams("parallel"),
        name="gdn_conv_sample",
    )(state2d, state2d, state2d, proj, conv_w)


def _gates_kernel(chunk, ba_ref, alog_ref, dtb_ref, beta_ref, gc_ref, gtot_ref, gct_ref):
    ba = ba_ref[...]
    nh = ba.shape[1] // 2
    r = ba.shape[0]
    hps = beta_ref.shape[2]

    def store_by_group(ref, x):
        for grp in range(nh // hps):
            ref[grp] = x[:, grp * hps:(grp + 1) * hps]
    store_by_group(beta_ref, jax.nn.sigmoid(ba[:, :nh]))
    g = -jnp.exp(alog_ref[...]) * _softplus(ba[:, nh:] + dtb_ref[...])
    ri = lax.broadcasted_iota(jnp.int32, (r, r), 0)
    ci = lax.broadcasted_iota(jnp.int32, (r, r), 1)
    same = _blk(ri, chunk) == _blk(ci, chunk)
    tri = jnp.where(same & (ri >= ci), 1.0, 0.0)
    blk = jnp.where(same, 1.0, 0.0)
    gc = jnp.dot(tri, g, precision=HIGHEST, preferred_element_type=F32)
    store_by_group(gc_ref, gc)
    store_by_group(gtot_ref, jnp.dot(blk, g, precision=HIGHEST, preferred_element_type=F32))
    eye = jnp.where(lax.broadcasted_iota(jnp.int32, (nh, nh), 0) == lax.broadcasted_iota(jnp.int32, (nh, nh), 1),
                    1.0, 0.0)
    gct_ref[0] = lax.dot_general(eye, gc, (((1,), (1,)), ((), ())), precision=HIGHEST,
                                 preferred_element_type=F32)


def _gates(ba, a_log, dt_bias, chunk, r, hps):
    n, h2 = ba.shape
    nh = h2 // 2
    row = pl.BlockSpec((nh // hps, r, hps), lambda i: (0, i, 0))
    vec = pl.BlockSpec((1, nh), lambda i: (0, 0))
    return pl.pallas_call(
        functools.partial(_gates_kernel, chunk),
        grid=(n // r,),
        in_specs=[pl.BlockSpec((r, h2), lambda i: (i, 0)), vec, vec],
        out_specs=[row, row, row, pl.BlockSpec((1, nh, r), lambda i: (i, 0, 0))],
        out_shape=[jax.ShapeDtypeStruct((nh // hps, n, hps), F32)] * 3 + [jax.ShapeDtypeStruct((n // r, nh, r), F32)],
        compiler_params=_cparams("parallel"),
        name="gdn_gates",
    )(ba, a_log.reshape(1, nh), dt_bias.reshape(1, nh))


def _chunk_terms(chunk, q, k, v, beta, gc, gtot, gc_row):
    r = q.shape[0]
    ri = lax.broadcasted_iota(jnp.int32, (r, r), 0)
    ci = lax.broadcasted_iota(jnp.int32, (r, r), 1)
    causal = (_blk(ri, chunk) == _blk(ci, chunk)) & (ri >= ci)
    decay = jnp.exp(jnp.where(causal, gc - gc_row, -jnp.inf))
    kb = k * beta
    low = jnp.where(ri > ci, _dot_nt(kb, k) * decay, 0.0)
    e = -jnp.where(_blk(ri, 2) == _blk(ci, 2), low, 0.0)
    s = 2
    while s < chunk:
        lm = jnp.where((_blk(ri, 2 * s) == _blk(ci, 2 * s)) & (_blk(ri, s) != _blk(ci, s)), low, 0.0)
        x = lm + _dot(e, lm)
        e = e - (x + _dot(x, e))
        s *= 2
    eg = jnp.exp(gc)
    rhs = jnp.concatenate([v * beta, kb * eg], axis=1)
    uw = rhs + _dot(e, rhs)
    dv = v.shape[1]
    qs = q * (HEAD ** -0.5)
    a_intra = _dot_nt(qs, k) * decay
    return uw[:, :dv], uw[:, dv:], a_intra, qs * eg, k * jnp.exp(gtot - gc)


def _paired_chunk_terms(chunk, heads):
    r, pair = heads[0][0].shape[0], 2 * chunk
    ii = lax.broadcasted_iota(jnp.int32, (chunk, pair), 0)
    lane = lax.broadcasted_iota(jnp.int32, (chunk, pair), 1)
    right = lane >= chunk
    jj = lane & (chunk - 1)
    diag = lambda full: jnp.where(right, full[chunk:], full[:chunk])
    spread = lambda y: jnp.concatenate([jnp.where(right, 0.0, y), jnp.where(right, y, 0.0)], axis=0)
    slices = [slice(p * pair, (p + 1) * pair) for p in range(r // pair)]
    kbs = [k * beta for (q, k, v, beta, gc, gtot, gc_row) in heads]
    egs = [jnp.exp(gc) for (q, k, v, beta, gc, gtot, gc_row) in heads]
    qss = [q * (HEAD ** -0.5) for (q, k, v, beta, gc, gtot, gc_row) in heads]
    rhss = [jnp.concatenate([h[2] * h[3], kb * eg], axis=1) for h, kb, eg in zip(heads, kbs, egs)]
    probs = [(i, sl) for i in range(len(heads)) for sl in slices]
    decays = []
    for i, sl in probs:
        gcp, gc_row = heads[i][4][sl], heads[i][6]
        decays.append(jnp.exp(jnp.where(ii >= jj, jnp.where(right, gcp[chunk:], gcp[:chunk]) - gc_row[:, sl],
                                        -jnp.inf)))
    kks = [_dot_nt(kbs[i][sl], heads[i][1][sl]) for i, sl in probs]
    lows = [jnp.where(ii > jj, diag(kk) * decay, 0.0) for kk, decay in zip(kks, decays)]
    es = [-jnp.where(_blk(ii, 2) == _blk(jj, 2), low, 0.0) for low in lows]
    s = 2
    while s < chunk:
        level = (_blk(ii, 2 * s) == _blk(jj, 2 * s)) & (_blk(ii, s) != _blk(jj, s))
        lms = [jnp.where(level, low, 0.0) for low in lows]
        lm_sp = [spread(lm).astype(BF16) for lm in lms]
        xs = [lm + _dot(e, sp) for lm, e, sp in zip(lms, es, lm_sp)]
        e_sp = [spread(e).astype(BF16) for e in es]
        es = [e - (x + _dot(x, sp)) for e, x, sp in zip(es, xs, e_sp)]
        s *= 2
    e_sp = [spread(e).astype(BF16) for e in es]
    uws = [rhss[i][sl] + _dot(sp, rhss[i][sl]) for (i, sl), sp in zip(probs, e_sp)]
    qks = [_dot_nt(qss[i][sl], heads[i][1][sl]) for i, sl in probs]
    atts = [diag(qk) * decay for qk, decay in zip(qks, decays)]
    out = []
    n_pairs = len(slices)
    for i, (q, k, v, beta, gc, gtot, gc_row) in enumerate(heads):
        uw = jnp.concatenate(uws[i * n_pairs:(i + 1) * n_pairs], axis=0)
        a_chunks = []
        for a in atts[i * n_pairs:(i + 1) * n_pairs]:
            a_chunks += [jnp.where(right, 0.0, a), jnp.where(right, a, 0.0)]
        dv = v.shape[1]
        out.append((uw[:, :dv], uw[:, dv:], a_chunks, qss[i] * egs[i], k * jnp.exp(gtot - gc)))
    return out


def _gated_norm(o, z, gain):
    return (_rms(o, gain) * _silu(z)).astype(BF16)


def _gdn_prompt_kernel(chunk, hps, q_ref, k_ref, v_ref, z_ref, beta_ref, gc_ref, gtot_ref, gct_ref, gn_ref,
                       o_ref, s_ref, s_scr):
    @pl.when(pl.program_id(2) == 0)
    def _():
        s_scr[...] = jnp.zeros_like(s_scr)
    r = q_ref.shape[0]
    h0 = pl.program_id(1) * hps
    paired = 2 * chunk == HEAD and r % HEAD == 0
    heads = []
    for hl in range(hps):
        kh = hl // 2
        heads.append((q_ref[:, kh * HEAD:(kh + 1) * HEAD], k_ref[:, kh * HEAD:(kh + 1) * HEAD],
                      v_ref[:, hl * HEAD:(hl + 1) * HEAD], beta_ref[0][:, hl:hl + 1],
                      gc_ref[0][:, hl:hl + 1], gtot_ref[0][:, hl:hl + 1],
                      gct_ref[0, pl.ds(h0 + hl, 1), :]))
    if paired:
        terms = _paired_chunk_terms(chunk, heads)
    else:
        terms = [_chunk_terms(chunk, *head) for head in heads]
    terms = [t + (jnp.exp(head[5]),) for t, head in zip(terms, heads)]
    states = [s_scr[hl] for hl in range(hps)]
    outs = [[] for _ in range(hps)]
    for c in range(r // chunk):
        lo, hi = c * chunk, (c + 1) * chunk
        for hl in range(hps):
            u, w, a, qd, kd, egt = terms[hl]
            s = states[hl]
            pq = _dot(jnp.concatenate([w[lo:hi], qd[lo:hi]], axis=0), s)
            v_new = u[lo:hi] - pq[:chunk]
            if paired:
                intra = _dot(a[c], jnp.concatenate([v_new, v_new], axis=0))
            else:
                intra = _dot(a[lo:hi, lo:hi], v_new)
            outs[hl].append(pq[chunk:] + intra)
            states[hl] = s * egt[lo:lo + 1] + _dot_tn(kd[lo:hi], v_new)
    for hl in range(hps):
        s_scr[hl] = states[hl]
        o = jnp.concatenate(outs[hl], axis=0)
        o_ref[:, hl * HEAD:(hl + 1) * HEAD] = _gated_norm(o, z_ref[:, hl * HEAD:(hl + 1) * HEAD], gn_ref[...])
    s_ref[0] = s_scr[...]


def _gdn_prompt(qkv, proj, beta, gc, gtot, gct, g_onorm, n_seq, t_len, d, chunk, r=256, hps=8):
    nvh = 2 * d // HEAD
    ns = t_len // r
    kc = hps // 2 * HEAD
    vc = hps * HEAD
    rows = lambda b, g, s: b * ns + s
    return pl.pallas_call(
        functools.partial(_gdn_prompt_kernel, chunk, hps),
        grid=(n_seq, nvh // hps, ns),
        in_specs=[pl.BlockSpec((r, kc), lambda b, g, s: (rows(b, g, s), g)),
                  pl.BlockSpec((r, kc), lambda b, g, s: (rows(b, g, s), d // kc + g)),
                  pl.BlockSpec((r, vc), lambda b, g, s: (rows(b, g, s), 2 * d // vc + g)),
                  pl.BlockSpec((r, vc), lambda b, g, s: (rows(b, g, s), 4 * d // vc + g)),
                  pl.BlockSpec((1, r, hps), lambda b, g, s: (g, rows(b, g, s), 0)),
                  pl.BlockSpec((1, r, hps), lambda b, g, s: (g, rows(b, g, s), 0)),
                  pl.BlockSpec((1, r, hps), lambda b, g, s: (g, rows(b, g, s), 0)),
                  pl.BlockSpec((1, nvh, r), lambda b, g, s: (rows(b, g, s), 0, 0)),
                  pl.BlockSpec((1, HEAD), lambda b, g, s: (0, 0))],
        out_specs=[pl.BlockSpec((r, vc), lambda b, g, s: (rows(b, g, s), g)),
                   pl.BlockSpec((1, hps, HEAD, HEAD), lambda b, g, s: (b, g, 0, 0))],
        out_shape=[jax.ShapeDtypeStruct((n_seq * t_len, 2 * d), BF16),
                   jax.ShapeDtypeStruct((n_seq, nvh, HEAD, HEAD), F32)],
        scratch_shapes=[pltpu.VMEM((hps, HEAD, HEAD), F32)],
        compiler_params=_cparams("parallel", "parallel", "arbitrary"),
        name="gdn_prompt",
    )(qkv, qkv, qkv, proj, beta, gc, gtot, gct, g_onorm.reshape(1, HEAD))


def _gdn_sample_kernel(chunk, hps, q_ref, k_ref, v_ref, z_ref, beta_ref, gc_ref, gtot_ref, gct_ref, gn_ref, s0_ref,
                       o_ref, s_ref, wq_scr, kd_scr, eg_scr, pq_scr):
    r = q_ref.shape[0]
    h0 = pl.program_id(0) * hps
    per_row = 8 // chunk
    terms = []
    for hl in range(hps):
        kh = hl // 2
        q = q_ref[:, kh * HEAD:(kh + 1) * HEAD]
        k = k_ref[:, kh * HEAD:(kh + 1) * HEAD]
        v = v_ref[:, hl * HEAD:(hl + 1) * HEAD]
        beta = beta_ref[0][:, hl:hl + 1]
        gc = gc_ref[0][:, hl:hl + 1]
        gtot = gtot_ref[0][:, hl:hl + 1]
        gc_row = gct_ref[0, pl.ds(h0 + hl, 1), :]
        u, w, a, qd, kd = _chunk_terms(chunk, q, k, v, beta, gc, gtot, gc_row)
        wq_scr[hl, :, 0:8, :] = w.reshape(r // 8, 8, HEAD)
        wq_scr[hl, :, 8:16, :] = qd.reshape(r // 8, 8, HEAD)
        kd_scr[hl] = kd
        eg_scr[hl] = jnp.broadcast_to(jnp.exp(gtot), (r, HEAD))
        terms.append((u, a))

    row16 = lax.broadcasted_iota(jnp.int32, (16, HEAD), 0)
    row8 = lax.broadcasted_iota(jnp.int32, (8, HEAD), 0)

    def pair_pred(p, carry):
        for hl in range(hps):
            lhs = wq_scr[hl, p]
            acc = jnp.zeros((16, HEAD), F32)
            for e in range(per_row):
                pq = _dot(lhs, s0_ref[p * per_row + e, hl])
                acc = jnp.where(_blk(row16 & 7, chunk) == e, pq, acc)
            pq_scr[hl, p] = acc
        return carry
    lax.fori_loop(0, r // 8, pair_pred, 0)

    v_news = []
    for hl in range(hps):
        u, a = terms[hl]
        pq = pq_scr[hl]
        v_new = u - pq[:, 0:8, :].reshape(r, HEAD)
        o = pq[:, 8:16, :].reshape(r, HEAD) + _dot(a, v_new)
        o_ref[:, hl * HEAD:(hl + 1) * HEAD] = _gated_norm(o, z_ref[:, hl * HEAD:(hl + 1) * HEAD], gn_ref[...])
        pq_scr[hl, :, 0:8, :] = v_new.reshape(r // 8, 8, HEAD)

    def pair_update(p, carry):
        for hl in range(hps):
            v_new = pq_scr[hl, p, 0:8, :]
            kd = kd_scr[hl, pl.ds(pl.multiple_of(p * 8, 8), 8), :]
            for e in range(per_row):
                b = p * per_row + e
                decay = eg_scr[hl, pl.ds(p * 8 + e * chunk, 1), :]
                kd_e = jnp.where(_blk(row8, chunk) == e, kd, 0.0)
                s_ref[b, hl] = s0_ref[b, hl] * decay + _dot_tn(kd_e, v_new)
        return carry
    lax.fori_loop(0, r // 8, pair_update, 0)


def _gdn_sample(qkv, z, beta, gc, gtot, gct, g_onorm, state, d, chunk, r=128, hps=4):
    n = qkv.shape[0]
    nvh = 2 * d // HEAD
    kc = hps // 2 * HEAD
    vc = hps * HEAD
    nb = r // chunk
    return pl.pallas_call(
        functools.partial(_gdn_sample_kernel, chunk, hps),
        grid=(nvh // hps, n // r),
        in_specs=[pl.BlockSpec((r, kc), lambda g, s: (s, g)),
                  pl.BlockSpec((r, kc), lambda g, s: (s, d // kc + g)),
                  pl.BlockSpec((r, vc), lambda g, s: (s, 2 * d // vc + g)),
                  pl.BlockSpec((r, vc), lambda g, s: (s, g)),
                  pl.BlockSpec((1, r, hps), lambda g, s: (g, s, 0)),
                  pl.BlockSpec((1, r, hps), lambda g, s: (g, s, 0)),
                  pl.BlockSpec((1, r, hps), lambda g, s: (g, s, 0)),
                  pl.BlockSpec((1, nvh, r), lambda g, s: (s, 0, 0)),
                  pl.BlockSpec((1, HEAD), lambda g, s: (0, 0)),
                  pl.BlockSpec((nb, hps, HEAD, HEAD), lambda g, s: (s, g, 0, 0))],
        out_specs=[pl.BlockSpec((r, vc), lambda g, s: (s, g)),
                   pl.BlockSpec((nb, hps, HEAD, HEAD), lambda g, s: (s, g, 0, 0))],
        out_shape=[jax.ShapeDtypeStruct((n, 2 * d), BF16),
                   jax.ShapeDtypeStruct(state.shape, F32)],
        scratch_shapes=[pltpu.VMEM((hps, r // 8, 16, HEAD), F32), pltpu.VMEM((hps, r, HEAD), F32),
                        pltpu.VMEM((hps, r, HEAD), F32), pltpu.VMEM((hps, r // 8, 16, HEAD), F32)],
        compiler_params=_cparams("parallel", "parallel"),
        name="gdn_sample",
    )(qkv, qkv, qkv, z, beta, gc, gtot, gct, g_onorm.reshape(1, HEAD), state)


def kernel(x_prompt, x_sample, c_prompt, c_sample, state_gdn, state_gdn_conv, state_sconv, w_ada, b_ada,
           g_norm_mix, g_norm_ffn, g_norm_out, gdn_w_in, gdn_conv_w, gdn_a_log, gdn_dt_bias, gdn_g_onorm,
           gdn_w_out, sc_w_in, sc_conv_w, sc_w_out, ffn_w_up, ffn_w_down, moe_w_router, moe_b_router,
           moe_w_up, moe_w_down):
    nbp, t_len, d = x_prompt.shape
    nbs, steps, _ = x_sample.shape
    n_p, n_s = nbp * t_len, nbs * steps
    nvh = 2 * d // HEAD
    chunk_p = min(64, t_len)
    tm = nbs
    assert t_len % tm == 0 and n_p % n_s == 0 and 8 % steps == 0

    n = n_p + n_s
    tm_mm = _tile(n, 1088)
    x_in = (x_prompt.reshape(n_p, d), x_sample.reshape(nbs, steps * d))
    pad = (-(nbp + nbs)) % 8
    c_all = jnp.concatenate([c_prompt, c_sample, jnp.zeros((pad, d), F32)], axis=0)
    m = _ada(c_all, w_ada, b_ada)
    mods = [_Mod(m[i, :nbp].reshape(nbp, 1, 6 * d), m[i, nbp:nbp + nbs], d, tm, n_p // tm, t_len // tm)
            for i in range(2)]
    SH1, SC1, GA1, SH2, SC2, GA2 = range(6)

    h = _prep(x_in, g_norm_mix[0], mods[0], SC1, SH1, n, tm)
    w_in = gdn_w_in[0]
    proj = _matmul(h, w_in, 6 * d, _tile(6 * d, 1024), tm_mm, name="gdn_in_proj")
    ba = _matmul(h, w_in[:, 6 * d:], 2 * nvh, 2 * nvh, tm_mm, name="gdn_in_proj_ba")
    conv_w = gdn_conv_w[0]
    qkv_p = _gconv_prompt(proj, conv_w, nbp, t_len, d)
    st_gconv = state_gdn_conv[0].reshape(nbs, 3 * 4 * d)
    qkv_s = _gconv_sample(proj, st_gconv, conv_w, n_p // n_s, nbs, n_s, d)
    to_seq_major = lambda a: a.reshape(steps, nbs, a.shape[-1]).transpose(1, 0, 2).reshape(n_s, a.shape[-1])
    qkv_s = to_seq_major(qkv_s)
    z_s = to_seq_major(proj[n_p:, 4 * d:6 * d])
    ba_s = to_seq_major(ba[n_p:])
    hps_p, hps_s = min(8, nvh), min(4, nvh)
    beta_p, gc_p, gtot_p, gct_p = _gates(ba[:n_p], gdn_a_log[0], gdn_dt_bias[0], chunk_p, 256, hps_p)
    beta_s, gc_s, gtot_s, gct_s = _gates(ba_s, gdn_a_log[0], gdn_dt_bias[0], steps, 128, hps_s)
    o_p, gdn_p = _gdn_prompt(qkv_p, proj, beta_p, gc_p, gtot_p, gct_p, gdn_g_onorm[0], nbp, t_len, d, chunk_p,
                             hps=hps_p)
    o_s, gdn_s = _gdn_sample(qkv_s, z_s, beta_s, gc_s, gtot_s, gct_s, gdn_g_onorm[0], state_gdn[0], d, steps,
                             hps=hps_s)
    to_time_major = lambda a: a.reshape(nbs, steps, a.shape[-1]).transpose(1, 0, 2).reshape(n_s, a.shape[-1])
    o = jnp.concatenate([o_p, to_time_major(o_s)], axis=0)
    y = _matmul(o, gdn_w_out[0], d, _tile(d, 512), tm_mm, name="gdn_out_proj")
    gconv_p = jnp.stack([proj[(b + 1) * t_len - 3:(b + 1) * t_len, :4 * d] for b in range(nbp)])
    gconv_s = proj[n_p + nbs:, :4 * d].reshape(3, nbs, 4 * d).transpose(1, 0, 2)

    x, h = _resid_prep(x_in, y, g_norm_ffn[0], mods[0], GA1, mods[0], SC2, SH2, tm)
    one_expert = jnp.zeros((n // tm_mm,), jnp.int32)
    all_tiles = jnp.full((1,), n // tm_mm, jnp.int32)
    act = _swiglu_up(h, ffn_w_up, one_expert, all_tiles, tm_mm, _tile(ffn_w_down.shape[1], 512), "ffn_swiglu_up")
    f = _matmul(act, ffn_w_down[0], d, _tile(d, 512), name="ffn_down")

    x, h = _resid_prep(x, f, g_norm_mix[1], mods[0], GA2, mods[1], SC1, SH1, tm)
    s_in = _matmul(h, sc_w_in[0], 3 * d, _tile(3 * d, 1024), tm_mm, name="sconv_in_proj")
    mix_p, sconv_tail = _sconv_prompt(s_in, sc_conv_w[0], nbp, t_len, d)
    mix_s, u_s = _sconv_sample(s_in, state_sconv[0].reshape(nbs, 2 * d), sc_conv_w[0], n_p // n_s, nbs, n_s, d)
    y = _matmul(jnp.concatenate([mix_p, mix_s], axis=0), sc_w_out[0], d, _tile(d, 1024), tm_mm,
                name="sconv_out_proj")
    sconv_p = sconv_tail[:, 6:8]
    sconv_s = u_s[n_s - 2 * nbs:].reshape(2, nbs, d).transpose(1, 0, 2)

    x, h, info = _resid_route(x, y, g_norm_ffn[1], mods[1], GA1, mods[1], SC2, SH2,
                              moe_w_router[0], moe_b_router[0], tm)
    _, n_exp, f_moe, _ = moe_w_down.shape
    tmg = 256
    pos, src, te, nv = _route_tables(info, n_exp, tmg)
    h_sorted = _gather_rows(h, src, nv, tmg)
    act = _swiglu_up(h_sorted, moe_w_up[0], te, nv, tmg, _tile(f_moe, 1024), "moe_swiglu_up")
    y_sorted = _grouped_down(act, moe_w_down[0], te, nv, tmg, _tile(d, 512), "moe_down")
    out_p, out_s = _final(x, y_sorted, pos, info, g_norm_out, mods[1], GA2, n_p, nbs, tm)

    y_prompt = out_p.reshape(nbp, t_len, d)
    y_sample = out_s.reshape(nbs, steps, d)
    return (y_prompt, y_sample, gdn_p[None], gconv_p[None], sconv_p[None], gdn_s[None], gconv_s[None], sconv_s[None])
```

```python
import functools

import jax
import jax.numpy as jnp
from jax import lax
from jax.experimental import pallas as pl
from jax.experimental.pallas import tpu as pltpu

F32 = jnp.float32
BF16 = jnp.bfloat16
NORM_EPS = 1e-6
HEAD = 128
V7X_VMEM_LIMIT = 56 * 1024 * 1024
HIGHEST = lax.Precision.HIGHEST


def _cparams(*sem):
    return pltpu.CompilerParams(dimension_semantics=sem, vmem_limit_bytes=V7X_VMEM_LIMIT)


def _tile(n, pref):
    return pref if n % pref == 0 else n


def _silu(x):
    return x * jax.nn.sigmoid(x)


def _softplus(x):
    return jnp.maximum(x, 0.0) + jnp.log1p(jnp.exp(-jnp.abs(x)))


def _blk(idx, size):
    assert size & (size - 1) == 0
    return lax.shift_right_logical(idx, size.bit_length() - 1)


def _dot(a, b):
    return jnp.dot(a.astype(BF16), b.astype(BF16), preferred_element_type=F32)


def _dot_nt(a, b):
    return lax.dot_general(a.astype(BF16), b.astype(BF16), (((1,), (1,)), ((), ())), preferred_element_type=F32)


def _dot_tn(a, b):
    return lax.dot_general(a.astype(BF16), b.astype(BF16), (((0,), (0,)), ((), ())), preferred_element_type=F32)


def _ada_kernel(c_ref, w_ref, b_ref, o_ref):
    c = c_ref[...]
    o_ref[0] = _dot(_silu(c), w_ref[0]) + b_ref[0]


def _ada(c_all, w_ada, b_ada):
    nl, d, n6 = w_ada.shape
    r = c_all.shape[0]
    tn = _tile(n6, 1024)
    return pl.pallas_call(
        _ada_kernel,
        grid=(nl, n6 // tn),
        in_specs=[pl.BlockSpec((r, d), lambda l, j: (0, 0)),
                  pl.BlockSpec((1, d, tn), lambda l, j: (l, 0, j)),
                  pl.BlockSpec((1, 1, tn), lambda l, j: (l, 0, j))],
        out_specs=pl.BlockSpec((1, r, tn), lambda l, j: (l, 0, j)),
        out_shape=jax.ShapeDtypeStruct((nl, r, n6), F32),
        compiler_params=_cparams("parallel", "parallel"),
        name="ada_mod",
    )(c_all, w_ada, b_ada.reshape(nl, 1, n6))


class _Mod:
    def __init__(self, mod_p, mod_s, d, tm, n_ptiles, tiles_per_seq):
        self.arrays = (mod_p, mod_s)
        self.n_ptiles = n_ptiles
        nb = mod_p.shape[0]

        def specs(chunk):
            return [pl.BlockSpec((1, 1, d), lambda i, *_: (jnp.minimum(i // tiles_per_seq, nb - 1), 0, chunk)),
                    pl.BlockSpec((tm, d), lambda i, *_: (0, chunk))]
        self.specs = specs
        self.row_specs = [pl.BlockSpec((tm, d), lambda i, *_: (jnp.minimum(i, n_ptiles - 1), 0)),
                          pl.BlockSpec((tm, d), lambda i, *_: (0, jnp.maximum(i - n_ptiles, 0)))]

    def is_prompt(self):
        return pl.program_id(0) < self.n_ptiles

    def pick(self, p_ref, s_ref):
        return jnp.where(self.is_prompt(), p_ref[0], s_ref[...])

    def pick_rows(self, p_ref, s_ref):
        return jnp.where(self.is_prompt(), p_ref[...], s_ref[...])


def _rms(x, gain):
    ms = jnp.mean(x * x, axis=-1, keepdims=True)
    return x * lax.rsqrt(ms + NORM_EPS) * gain


def _prep_kernel(mod, xp_ref, xs_ref, g_ref, scp, scs, shp, shs, h_ref):
    y = _rms(mod.pick_rows(xp_ref, xs_ref), g_ref[...])
    h_ref[...] = (y * (1.0 + mod.pick(scp, scs)) + mod.pick(shp, shs)).astype(h_ref.dtype)


def _prep(x_pair, gain, mod, sc_chunk, sh_chunk, n, tm):
    d = gain.shape[0]
    return pl.pallas_call(
        functools.partial(_prep_kernel, mod),
        grid=(n // tm,),
        in_specs=mod.row_specs + [pl.BlockSpec((1, d), lambda i: (0, 0))] + mod.specs(sc_chunk) + mod.specs(sh_chunk),
        out_specs=pl.BlockSpec((tm, d), lambda i: (i, 0)),
        out_shape=jax.ShapeDtypeStruct((n, d), BF16),
        compiler_params=_cparams("parallel"),
        name="norm_mod",
    )(*x_pair, gain.reshape(1, d), *mod.arrays, *mod.arrays)


def _row_inputs(mod, a, tm, d):
    if isinstance(a, tuple):
        return mod.row_specs, list(a)
    return [pl.BlockSpec((tm, d), lambda i, *_: (i, 0))], [a]


def _read_rows(mod, refs):
    return mod.pick_rows(*refs) if len(refs) == 2 else refs[0][...]


def _resid_prep_kernel(mod, nx, ny, *refs):
    x_refs, y_refs = refs[:nx], refs[nx:nx + ny]
    g_ref, gap, gas, scp, scs, shp, shs, xo_ref, h_ref = refs[nx + ny:]
    x = _read_rows(mod, x_refs) + mod.pick(gap, gas) * _read_rows(mod, y_refs)
    xo_ref[...] = x
    y = _rms(x, g_ref[...])
    h_ref[...] = (y * (1.0 + mod.pick(scp, scs)) + mod.pick(shp, shs)).astype(h_ref.dtype)


def _resid_prep(x, y, gain, mod_gate, ga_chunk, mod, sc_chunk, sh_chunk, n, tm):
    d = gain.shape[0]
    row = pl.BlockSpec((tm, d), lambda i: (i, 0))
    x_specs, x_ops = _row_inputs(mod, x, tm, d)
    y_specs, y_ops = _row_inputs(mod, y, tm, d)
    return pl.pallas_call(
        functools.partial(_resid_prep_kernel, mod, len(x_ops), len(y_ops)),
        grid=(n // tm,),
        in_specs=x_specs + y_specs + [pl.BlockSpec((1, d), lambda i: (0, 0))]
        + mod_gate.specs(ga_chunk) + mod.specs(sc_chunk) + mod.specs(sh_chunk),
        out_specs=[row, row],
        out_shape=[jax.ShapeDtypeStruct((n, d), F32), jax.ShapeDtypeStruct((n, d), BF16)],
        compiler_params=_cparams("parallel"),
        name="resid_norm_mod",
    )(*x_ops, *y_ops, gain.reshape(1, d), *mod_gate.arrays, *mod.arrays, *mod.arrays)


def _resid_route_kernel(mod, n_exp, ny, x_ref, *refs):
    y_refs = refs[:ny]
    g_ref, gap, gas, scp, scs, shp, shs, wr_ref, br_ref, xo_ref, h_ref, info_ref = refs[ny:]
    x = x_ref[...] + mod.pick(gap, gas) * _read_rows(mod, y_refs)
    xo_ref[...] = x
    h = _rms(x, g_ref[...]) * (1.0 + mod.pick(scp, scs)) + mod.pick(shp, shs)
    h_ref[...] = h
    logits = jnp.dot(h, wr_ref[...], precision=HIGHEST, preferred_element_type=F32) + br_ref[...]
    lane = lax.broadcasted_iota(jnp.int32, logits.shape, 1).astype(F32)
    m1 = jnp.max(logits, axis=-1, keepdims=True)
    i1 = jnp.min(jnp.where(logits == m1, lane, float(n_exp)), axis=-1, keepdims=True)
    rest = jnp.where(lane == i1, -jnp.inf, logits)
    m2 = jnp.max(rest, axis=-1, keepdims=True)
    i2 = jnp.min(jnp.where(rest == m2, lane, float(n_exp)), axis=-1, keepdims=True)
    e2 = jnp.exp(m2 - m1)
    g1 = 1.0 / (1.0 + e2)
    g2 = e2 / (1.0 + e2)
    info_ref[...] = (jnp.where(lane == 0.0, i1, 0.0) + jnp.where(lane == 1.0, i2, 0.0)
                     + jnp.where(lane == 2.0, g1, 0.0) + jnp.where(lane == 3.0, g2, 0.0))


def _resid_route(x, y, gain, mod_gate, ga_chunk, mod, sc_chunk, sh_chunk, w_router, b_router, tm):
    n, d = x.shape
    n_exp = w_router.shape[1]
    row = pl.BlockSpec((tm, d), lambda i: (i, 0))
    y_specs, y_ops = _row_inputs(mod, y, tm, d)
    return pl.pallas_call(
        functools.partial(_resid_route_kernel, mod, n_exp, len(y_ops)),
        grid=(n // tm,),
        in_specs=[row] + y_specs + [pl.BlockSpec((1, d), lambda i: (0, 0))]
        + mod_gate.specs(ga_chunk) + mod.specs(sc_chunk) + mod.specs(sh_chunk)
        + [pl.BlockSpec((d, n_exp), lambda i: (0, 0)), pl.BlockSpec((1, n_exp), lambda i: (0, 0))],
        out_specs=[row, row, pl.BlockSpec((tm, n_exp), lambda i: (i, 0))],
        out_shape=[jax.ShapeDtypeStruct((n, d), F32), jax.ShapeDtypeStruct((n, d), F32),
                   jax.ShapeDtypeStruct((n, n_exp), F32)],
        compiler_params=_cparams("parallel"),
        name="resid_norm_route",
    )(x, *y_ops, gain.reshape(1, d), *mod_gate.arrays, *mod.arrays, *mod.arrays, w_router,
      b_router.reshape(1, n_exp))


def _row_copy(src_hbm, row, dst, r, sem):
    return pltpu.make_async_copy(src_hbm.at[pl.ds(row, 1)], dst.at[pl.ds(r, 1)], sem)


def _final_kernel(mod, pos_ref, x_ref, y_hbm, info_ref, g_ref, gap, gas, op_ref, os_ref, buf, sem):
    tm = x_ref.shape[0]
    i = pl.program_id(0)
    n_tiles = pl.num_programs(0)

    def issue(tile, slot):
        def body(r, carry):
            for k in range(2):
                _row_copy(y_hbm, pos_ref[tile * (2 * tm) + 2 * r + k], buf.at[slot, k], r, sem.at[slot, k]).start()
            return carry
        lax.fori_loop(0, tm, body, 0, unroll=4)

    @pl.when(i == 0)
    def _():
        issue(0, 0)

    for slot in range(2):
        @pl.when((i + 1 < n_tiles) & ((i + 1) % 2 == slot))
        def _():
            issue(i + 1, slot)

    for slot in range(2):
        @pl.when(i % 2 == slot)
        def _():
            for k in range(2):
                pltpu.make_async_copy(y_hbm.at[pl.ds(0, tm)], buf.at[slot, k], sem.at[slot, k]).wait()
            info = info_ref[...]
            f = info[:, 2:3] * buf[slot, 0] + info[:, 3:4] * buf[slot, 1]
            x = x_ref[...] + mod.pick(gap, gas) * f
            out = _rms(x, g_ref[...])

            @pl.when(mod.is_prompt())
            def _():
                op_ref[...] = out

            @pl.when(jnp.logical_not(mod.is_prompt()))
            def _():
                os_ref[...] = out


def _final(x, y_sorted, pos, info, gain, mod_gate, ga_chunk, n_p, nbs, tm):
    n, d = x.shape
    row = pl.BlockSpec((tm, d), lambda i, *_: (i, 0))
    return pl.pallas_call(
        functools.partial(_final_kernel, mod_gate),
        grid_spec=pltpu.PrefetchScalarGridSpec(
            num_scalar_prefetch=1,
            grid=(n // tm,),
            in_specs=[row, pl.BlockSpec(memory_space=pl.ANY),
                      pl.BlockSpec((tm, info.shape[1]), lambda i, *_: (i, 0)),
                      pl.BlockSpec((1, d), lambda i, *_: (0, 0))] + mod_gate.specs(ga_chunk),
            out_specs=mod_gate.row_specs,
            scratch_shapes=[pltpu.VMEM((2, 2, tm, d), F32), pltpu.SemaphoreType.DMA((2, 2))]),
        out_shape=[jax.ShapeDtypeStruct((n_p, d), F32), jax.ShapeDtypeStruct((nbs, (n - n_p) // nbs * d), F32)],
        compiler_params=_cparams("arbitrary"),
        name="moe_combine_final_norm",
    )(pos, x, y_sorted, info, gain.reshape(1, d), *mod_gate.arrays)


def _route_tables(info, n_exp, tmg):
    n = info.shape[0]
    e = info[:, :2].astype(jnp.int32)
    onehot = (e[:, :, None] == jnp.arange(n_exp, dtype=jnp.int32)[None, None, :]).astype(jnp.int32).sum(axis=1)
    counts = onehot.sum(axis=0)
    padded = (counts + tmg - 1) // tmg * tmg
    ends = jnp.cumsum(padded)
    starts = ends - padded
    rank = jnp.cumsum(onehot, axis=0) - onehot
    pos = (starts[e] + jnp.take_along_axis(rank, e, axis=1)).reshape(-1).astype(jnp.int32)
    n_tiles = -(-(2 * n) // tmg) + n_exp
    src = jnp.zeros((n_tiles * tmg,), jnp.int32).at[pos].set(jnp.repeat(jnp.arange(n, dtype=jnp.int32), 2))
    n_valid = (ends[-1] // tmg).astype(jnp.int32)
    tiles = jnp.arange(n_tiles, dtype=jnp.int32)
    te = jnp.minimum((tiles[:, None] * tmg >= ends[None, :]).astype(jnp.int32).sum(axis=1), n_exp - 1)
    te = jnp.where(tiles < n_valid, te, te[jnp.maximum(n_valid - 1, 0)])
    return pos, src, te, n_valid.reshape(1)


def _gather_kernel(src_ref, nv_ref, h_hbm, o_ref, buf, sem):
    i = pl.program_id(0)
    tmg = buf.shape[1]
    nv = nv_ref[0]

    def issue(tile, slot):
        def body(r, carry):
            _row_copy(h_hbm, src_ref[tile * tmg + r], buf.at[slot], r, sem.at[slot]).start()
            return carry
        lax.fori_loop(0, tmg, body, 0, unroll=8)

    @pl.when((i == 0) & (nv > 0))
    def _():
        issue(0, 0)

    for slot in range(2):
        @pl.when((i + 1 < nv) & ((i + 1) % 2 == slot))
        def _():
            issue(i + 1, slot)

    for slot in range(2):
        @pl.when((i < nv) & (i % 2 == slot))
        def _():
            pltpu.make_async_copy(h_hbm.at[pl.ds(0, tmg)], buf.at[slot], sem.at[slot]).wait()
            o_ref[...] = buf[slot].astype(o_ref.dtype)

    @pl.when(i >= nv)
    def _():
        o_ref[...] = jnp.zeros_like(o_ref)


def _gather_rows(h, src, n_valid, tmg):
    d = h.shape[1]
    n_tiles = src.shape[0] // tmg
    return pl.pallas_call(
        _gather_kernel,
        grid_spec=pltpu.PrefetchScalarGridSpec(
            num_scalar_prefetch=2,
            grid=(n_tiles,),
            in_specs=[pl.BlockSpec(memory_space=pl.ANY)],
            out_specs=pl.BlockSpec((tmg, d), lambda i, src, nv: (i, 0)),
            scratch_shapes=[pltpu.VMEM((2, tmg, d), F32), pltpu.SemaphoreType.DMA((2,))]),
        out_shape=jax.ShapeDtypeStruct((n_tiles * tmg, d), BF16),
        compiler_params=_cparams("arbitrary"),
        name="moe_gather",
    )(src, n_valid, h)


def _mm_kernel(w_is_transposed, a_ref, w_ref, o_ref, wb_ref):
    @pl.when(pl.program_id(1) == 0)
    def _():
        wb_ref[...] = w_ref[...].astype(BF16)
    if w_is_transposed:
        acc = lax.dot_general(a_ref[...], wb_ref[...], (((1,), (1,)), ((), ())), preferred_element_type=F32)
    else:
        acc = jnp.dot(a_ref[...], wb_ref[...], preferred_element_type=F32)
    o_ref[...] = acc.astype(o_ref.dtype)


def _matmul(a, w, n_out, tn, tm=512, out_dtype=F32, name="matmul", w_is_transposed=False, col_block0=0):
    m, k = a.shape
    tm = _tile(m, tm)
    if w_is_transposed:
        w_spec = pl.BlockSpec((tn, k), lambda j, i: (col_block0 + j, 0))
        wb_shape = (tn, k)
    else:
        w_spec = pl.BlockSpec((k, tn), lambda j, i: (0, col_block0 + j))
        wb_shape = (k, tn)
    return pl.pallas_call(
        functools.partial(_mm_kernel, w_is_transposed),
        grid=(n_out // tn, m // tm),
        in_specs=[pl.BlockSpec((tm, k), lambda j, i: (i, 0)), w_spec],
        out_specs=pl.BlockSpec((tm, tn), lambda j, i: (i, j)),
        out_shape=jax.ShapeDtypeStruct((m, n_out), out_dtype),
        scratch_shapes=[pltpu.VMEM(wb_shape, BF16)],
        compiler_params=_cparams("parallel", "arbitrary"),
        name=name,
    )(a, w)


def _new_expert(te_ref):
    i = pl.program_id(1)
    return (i == 0) | (te_ref[i] != te_ref[jnp.maximum(i - 1, 0)])


def _swiglu_up_kernel(te_ref, nv_ref, a_ref, wg_ref, wu_ref, o_ref, wgb_ref, wub_ref):
    @pl.when(_new_expert(te_ref))
    def _():
        wgb_ref[...] = wg_ref[0].astype(BF16)
        wub_ref[...] = wu_ref[0].astype(BF16)

    @pl.when(pl.program_id(1) < nv_ref[0])
    def _():
        a = a_ref[...]
        gate = jnp.dot(a, wgb_ref[...], preferred_element_type=F32)
        up = jnp.dot(a, wub_ref[...], preferred_element_type=F32)
        o_ref[...] = (_silu(gate) * up).astype(o_ref.dtype)

    @pl.when(pl.program_id(1) >= nv_ref[0])
    def _():
        o_ref[...] = jnp.zeros_like(o_ref)


def _swiglu_up(a, w_up, te, nv, tm, tn, name):
    m, k = a.shape
    f = w_up.shape[2] // 2
    nj = f // tn
    row = lambda j, i, te, nv: jnp.minimum(i, nv[0] - 1)
    return pl.pallas_call(
        _swiglu_up_kernel,
        grid_spec=pltpu.PrefetchScalarGridSpec(
            num_scalar_prefetch=2,
            grid=(nj, m // tm),
            in_specs=[pl.BlockSpec((tm, k), lambda j, i, te, nv: (row(j, i, te, nv), 0)),
                      pl.BlockSpec((1, k, tn), lambda j, i, te, nv: (te[i], 0, j)),
                      pl.BlockSpec((1, k, tn), lambda j, i, te, nv: (te[i], 0, nj + j))],
            out_specs=pl.BlockSpec((tm, tn), lambda j, i, te, nv: (i, j)),
            scratch_shapes=[pltpu.VMEM((k, tn), BF16), pltpu.VMEM((k, tn), BF16)]),
        out_shape=jax.ShapeDtypeStruct((m, f), BF16),
        compiler_params=_cparams("parallel", "arbitrary"),
        name=name,
    )(te, nv, a, w_up, w_up)


def _down_kernel(te_ref, nv_ref, a_ref, w_ref, o_ref, wb_ref):
    @pl.when(_new_expert(te_ref))
    def _():
        wb_ref[...] = w_ref[0].astype(BF16)

    @pl.when(pl.program_id(1) < nv_ref[0])
    def _():
        o_ref[...] = jnp.dot(a_ref[...], wb_ref[...], preferred_element_type=F32)

    @pl.when(pl.program_id(1) >= nv_ref[0])
    def _():
        o_ref[...] = jnp.zeros_like(o_ref)


def _grouped_down(a, w_down, te, nv, tm, tn, name):
    m, f = a.shape
    d = w_down.shape[2]
    row = lambda j, i, te, nv: jnp.minimum(i, nv[0] - 1)
    return pl.pallas_call(
        _down_kernel,
        grid_spec=pltpu.PrefetchScalarGridSpec(
            num_scalar_prefetch=2,
            grid=(d // tn, m // tm),
            in_specs=[pl.BlockSpec((tm, f), lambda j, i, te, nv: (row(j, i, te, nv), 0)),
                      pl.BlockSpec((1, f, tn), lambda j, i, te, nv: (te[i], 0, j))],
            out_specs=pl.BlockSpec((tm, tn), lambda j, i, te, nv: (i, j)),
            scratch_shapes=[pltpu.VMEM((f, tn), BF16)]),
        out_shape=jax.ShapeDtypeStruct((m, d), F32),
        compiler_params=_cparams("parallel", "arbitrary"),
        name=name,
    )(te, nv, a, w_down)


def _shift_rows(carry, u, j):
    ext = jnp.concatenate([carry, u], axis=0)
    return pltpu.roll(ext, j, 0)[8:]


def _sconv_prompt_kernel(bg_ref, cg_ref, xi_ref, w_ref, o_ref, st_ref, carry_ref):
    @pl.when(pl.program_id(1) == 0)
    def _():
        carry_ref[...] = jnp.zeros_like(carry_ref)
    u = cg_ref[...] * xi_ref[...]
    carry = carry_ref[...]
    w = w_ref[...]
    y = w[2:3] * u + w[1:2] * _shift_rows(carry, u, 1) + w[0:1] * _shift_rows(carry, u, 2)
    o_ref[...] = (bg_ref[...] * y).astype(o_ref.dtype)
    tail = u[u.shape[0] - 8:]
    carry_ref[...] = tail
    st_ref[0] = tail


def _sconv_prompt(s_in, conv_w, n_seq, t_len, d, tt=256):
    tt = _tile(t_len, tt)
    nt = t_len // tt
    blk = lambda c: pl.BlockSpec((tt, d), lambda b, t: (b * nt + t, c))
    return pl.pallas_call(
        _sconv_prompt_kernel,
        grid=(n_seq, nt),
        in_specs=[blk(0), blk(1), blk(2), pl.BlockSpec((3, d), lambda b, t: (0, 0))],
        out_specs=[pl.BlockSpec((tt, d), lambda b, t: (b * nt + t, 0)),
                   pl.BlockSpec((1, 8, d), lambda b, t: (b, 0, 0))],
        out_shape=[jax.ShapeDtypeStruct((n_seq * t_len, d), BF16), jax.ShapeDtypeStruct((n_seq, 8, d), F32)],
        scratch_shapes=[pltpu.VMEM((8, d), F32)],
        compiler_params=_cparams("parallel", "arbitrary"),
        name="sconv_prompt",
    )(s_in, s_in, s_in, conv_w)


def _sconv_sample_kernel(nb, bg_ref, cg_ref, xi_ref, s0_ref, s1_ref, w_ref, o_ref, u_ref):
    u = cg_ref[...] * xi_ref[...]
    full = jnp.concatenate([s0_ref[...], s1_ref[...], u], axis=0)
    n = u.shape[0]
    w = w_ref[...]
    y = w[0:1] * full[0:n] + w[1:2] * full[nb:nb + n] + w[2:3] * full[2 * nb:2 * nb + n]
    o_ref[...] = (bg_ref[...] * y).astype(o_ref.dtype)
    u_ref[...] = u


def _sconv_sample(s_in, state2d, conv_w, row_block, nb, ns, d, ct=512):
    ct = _tile(d, ct)
    nj = d // ct
    blk = lambda c: pl.BlockSpec((ns, ct), lambda j: (row_block, c * nj + j))
    st = lambda s: pl.BlockSpec((nb, ct), lambda j: (0, s * nj + j))
    return pl.pallas_call(
        functools.partial(_sconv_sample_kernel, nb),
        grid=(nj,),
        in_specs=[blk(0), blk(1), blk(2), st(0), st(1), pl.BlockSpec((3, ct), lambda j: (0, j))],
        out_specs=[pl.BlockSpec((ns, ct), lambda j: (0, j)), pl.BlockSpec((ns, ct), lambda j: (0, j))],
        out_shape=[jax.ShapeDtypeStruct((ns, d), BF16), jax.ShapeDtypeStruct((ns, d), F32)],
        compiler_params=_cparams("parallel"),
        name="sconv_sample",
    )(s_in, s_in, s_in, state2d, state2d, conv_w)


def _l2norm_heads(x):
    parts = []
    for h in range(x.shape[1] // HEAD):
        seg = x[:, h * HEAD:(h + 1) * HEAD]
        parts.append(seg * lax.rsqrt(jnp.sum(seg * seg, axis=-1, keepdims=True) + NORM_EPS))
    return jnp.concatenate(parts, axis=1) if len(parts) > 1 else parts[0]


def _store_qkv(o_ref, act, is_qk):
    @pl.when(is_qk)
    def _():
        o_ref[...] = _l2norm_heads(act)

    @pl.when(jnp.logical_not(is_qk))
    def _():
        o_ref[...] = act


def _gconv_prompt_kernel(n_qk_tiles, x_ref, w_ref, o_ref, carry_ref):
    @pl.when(pl.program_id(2) == 0)
    def _():
        carry_ref[...] = jnp.zeros_like(carry_ref)
    u = x_ref[...]
    carry = carry_ref[...]
    w = w_ref[...]
    y = (w[3:4] * u + w[2:3] * _shift_rows(carry, u, 1) + w[1:2] * _shift_rows(carry, u, 2)
         + w[0:1] * _shift_rows(carry, u, 3))
    carry_ref[...] = u[u.shape[0] - 8:]
    _store_qkv(o_ref, _silu(y), pl.program_id(1) < n_qk_tiles)


def _gconv_prompt(proj, conv_w, n_seq, t_len, d, tt=512, ct=1024):
    ch = 4 * d
    tt = _tile(t_len, tt)
    ct = _tile(ch, ct)
    nt = t_len // tt
    return pl.pallas_call(
        functools.partial(_gconv_prompt_kernel, 2 * d // ct),
        grid=(n_seq, ch // ct, nt),
        in_specs=[pl.BlockSpec((tt, ct), lambda b, j, t: (b * nt + t, j)),
                  pl.BlockSpec((4, ct), lambda b, j, t: (0, j))],
        out_specs=pl.BlockSpec((tt, ct), lambda b, j, t: (b * nt + t, j)),
        out_shape=jax.ShapeDtypeStruct((n_seq * t_len, ch), F32),
        scratch_shapes=[pltpu.VMEM((8, ct), F32)],
        compiler_params=_cparams("parallel", "parallel", "arbitrary"),
        name="gdn_conv_prompt",
    )(proj, conv_w)


def _gconv_sample_kernel(n_qk_tiles, nb, s0_ref, s1_ref, s2_ref, x_ref, w_ref, o_ref):
    u = x_ref[...]
    full = jnp.concatenate([s0_ref[0], s1_ref[0], s2_ref[0], u], axis=0)
    n = u.shape[0]
    w = w_ref[...]
    y = (w[0:1] * full[0:n] + w[1:2] * full[nb:nb + n] + w[2:3] * full[2 * nb:2 * nb + n]
         + w[3:4] * full[3 * nb:3 * nb + n])
    _store_qkv(o_ref, _silu(y), pl.program_id(0) < n_qk_tiles)


def _gconv_sample(proj, state_tm, conv_w, row_block, nb, ns, d, ct=512):
    ch = 4 * d
    ct = _tile(ch, ct)
    nj = ch // ct
    st = lambda s: pl.BlockSpec((1, nb, ct), lambda j: (s, 0, j))
    return pl.pallas_call(
        functools.partial(_gconv_sample_kernel, 2 * d // ct, nb),
        grid=(nj,),
        in_specs=[st(0), st(1), st(2), pl.BlockSpec((ns, ct), lambda j: (row_block, j)),
                  pl.BlockSpec((4, ct), lambda j: (0, j))],
        out_specs=pl.BlockSpec((ns, ct), lambda j: (0, j)),
        out_shape=jax.ShapeDtypeStruct((ns, ch), F32),
        compiler_params=_cparams("parallel"),
        name="gdn_conv_sample",
    )(state_tm, state_tm, state_tm, proj, conv_w)


def _gates_kernel(chunk, ba_ref, alog_ref, dtb_ref, beta_ref, gc_ref, gtot_ref, gct_ref):
    ba = ba_ref[...]
    nh = ba.shape[1] // 2
    r = ba.shape[0]
    hps = beta_ref.shape[2]

    def store_by_group(ref, x):
        for grp in range(nh // hps):
            ref[grp] = x[:, grp * hps:(grp + 1) * hps]
    store_by_group(beta_ref, jax.nn.sigmoid(ba[:, :nh]))
    g = -jnp.exp(alog_ref[...]) * _softplus(ba[:, nh:] + dtb_ref[...])
    ri = lax.broadcasted_iota(jnp.int32, (r, r), 0)
    ci = lax.broadcasted_iota(jnp.int32, (r, r), 1)
    same = _blk(ri, chunk) == _blk(ci, chunk)
    tri = jnp.where(same & (ri >= ci), 1.0, 0.0)
    blk = jnp.where(same, 1.0, 0.0)
    gc = jnp.dot(tri, g, precision=HIGHEST, preferred_element_type=F32)
    store_by_group(gc_ref, gc)
    store_by_group(gtot_ref, jnp.dot(blk, g, precision=HIGHEST, preferred_element_type=F32))
    eye = jnp.where(lax.broadcasted_iota(jnp.int32, (nh, nh), 0) == lax.broadcasted_iota(jnp.int32, (nh, nh), 1),
                    1.0, 0.0)
    gct_ref[0] = lax.dot_general(eye, gc, (((1,), (1,)), ((), ())), precision=HIGHEST,
                                 preferred_element_type=F32)


def _gates(ba, a_log, dt_bias, chunk, r, hps):
    n, h2 = ba.shape
    nh = h2 // 2
    row = pl.BlockSpec((nh // hps, r, hps), lambda i: (0, i, 0))
    vec = pl.BlockSpec((1, nh), lambda i: (0, 0))
    return pl.pallas_call(
        functools.partial(_gates_kernel, chunk),
        grid=(n // r,),
        in_specs=[pl.BlockSpec((r, h2), lambda i: (i, 0)), vec, vec],
        out_specs=[row, row, row, pl.BlockSpec((1, nh, r), lambda i: (i, 0, 0))],
        out_shape=[jax.ShapeDtypeStruct((nh // hps, n, hps), F32)] * 3 + [jax.ShapeDtypeStruct((n // r, nh, r), F32)],
        compiler_params=_cparams("parallel"),
        name="gdn_gates",
    )(ba, a_log.reshape(1, nh), dt_bias.reshape(1, nh))


def _chunk_terms(chunk, q, k, v, beta, gc, gtot, gc_row):
    r = q.shape[0]
    ri = lax.broadcasted_iota(jnp.int32, (r, r), 0)
    ci = lax.broadcasted_iota(jnp.int32, (r, r), 1)
    causal = (_blk(ri, chunk) == _blk(ci, chunk)) & (ri >= ci)
    decay = jnp.exp(jnp.where(causal, gc - gc_row, -jnp.inf))
    kb = k * beta
    low = jnp.where(ri > ci, _dot_nt(kb, k) * decay, 0.0)
    e = -jnp.where(_blk(ri, 2) == _blk(ci, 2), low, 0.0)
    s = 2
    while s < chunk:
        lm = jnp.where((_blk(ri, 2 * s) == _blk(ci, 2 * s)) & (_blk(ri, s) != _blk(ci, s)), low, 0.0)
        x = lm + _dot(e, lm)
        e = e - (x + _dot(x, e))
        s *= 2
    eg = jnp.exp(gc)
    rhs = jnp.concatenate([v * beta, kb * eg], axis=1)
    uw = rhs + _dot(e, rhs)
    dv = v.shape[1]
    qs = q * (HEAD ** -0.5)
    a_intra = _dot_nt(qs, k) * decay
    return uw[:, :dv], uw[:, dv:], a_intra, qs * eg, k * jnp.exp(gtot - gc)


def _paired_chunk_terms(chunk, heads):
    r, pair = heads[0][0].shape[0], 2 * chunk
    ii = lax.broadcasted_iota(jnp.int32, (chunk, pair), 0)
    lane = lax.broadcasted_iota(jnp.int32, (chunk, pair), 1)
    right = lane >= chunk
    jj = lane & (chunk - 1)
    diag = lambda full: jnp.where(right, full[chunk:], full[:chunk])
    spread = lambda y: jnp.concatenate([jnp.where(right, 0.0, y), jnp.where(right, y, 0.0)], axis=0)
    slices = [slice(p * pair, (p + 1) * pair) for p in range(r // pair)]
    kbs = [k * beta for (q, k, v, beta, gc, gtot, gc_row) in heads]
    egs = [jnp.exp(gc) for (q, k, v, beta, gc, gtot, gc_row) in heads]
    qss = [q * (HEAD ** -0.5) for (q, k, v, beta, gc, gtot, gc_row) in heads]
    rhss = [jnp.concatenate([h[2] * h[3], kb * eg], axis=1) for h, kb, eg in zip(heads, kbs, egs)]
    probs = [(i, sl) for i in range(len(heads)) for sl in slices]
    decays = []
    for i, sl in probs:
        gcp, gc_row = heads[i][4][sl], heads[i][6]
        decays.append(jnp.exp(jnp.where(ii >= jj, jnp.where(right, gcp[chunk:], gcp[:chunk]) - gc_row[:, sl],
                                        -jnp.inf)))
    kks = [_dot_nt(kbs[i][sl], heads[i][1][sl]) for i, sl in probs]
    lows = [jnp.where(ii > jj, diag(kk) * decay, 0.0) for kk, decay in zip(kks, decays)]
    es = [-jnp.where(_blk(ii, 2) == _blk(jj, 2), low, 0.0) for low in lows]
    s = 2
    while s < chunk:
        level = (_blk(ii, 2 * s) == _blk(jj, 2 * s)) & (_blk(ii, s) != _blk(jj, s))
        lms = [jnp.where(level, low, 0.0) for low in lows]
        lm_sp = [spread(lm).astype(BF16) for lm in lms]
        xs = [lm + _dot(e, sp) for lm, e, sp in zip(lms, es, lm_sp)]
        e_sp = [spread(e).astype(BF16) for e in es]
        es = [e - (x + _dot(x, sp)) for e, x, sp in zip(es, xs, e_sp)]
        s *= 2
    e_sp = [spread(e).astype(BF16) for e in es]
    uws = [rhss[i][sl] + _dot(sp, rhss[i][sl]) for (i, sl), sp in zip(probs, e_sp)]
    qks = [_dot_nt(qss[i][sl], heads[i][1][sl]) for i, sl in probs]
    atts = [diag(qk) * decay for qk, decay in zip(qks, decays)]
    out = []
    n_pairs = len(slices)
    for i, (q, k, v, beta, gc, gtot, gc_row) in enumerate(heads):
        uw = jnp.concatenate(uws[i * n_pairs:(i + 1) * n_pairs], axis=0)
        a_chunks = []
        for a in atts[i * n_pairs:(i + 1) * n_pairs]:
            a_chunks += [jnp.where(right, 0.0, a), jnp.where(right, a, 0.0)]
        dv = v.shape[1]
        out.append((uw[:, :dv], uw[:, dv:], a_chunks, qss[i] * egs[i], k * jnp.exp(gtot - gc)))
    return out


def _gated_norm(o, z, gain):
    return (_rms(o, gain) * _silu(z)).astype(BF16)


def _gdn_prompt_kernel(chunk, hps, q_ref, k_ref, v_ref, z_ref, beta_ref, gc_ref, gtot_ref, gct_ref, gn_ref,
                       o_ref, s_ref, s_scr):
    @pl.when(pl.program_id(2) == 0)
    def _():
        s_scr[...] = jnp.zeros_like(s_scr)
    r = q_ref.shape[0]
    h0 = pl.program_id(1) * hps
    paired = 2 * chunk == HEAD and r % HEAD == 0
    heads = []
    for hl in range(hps):
        kh = hl // 2
        heads.append((q_ref[:, kh * HEAD:(kh + 1) * HEAD], k_ref[:, kh * HEAD:(kh + 1) * HEAD],
                      v_ref[:, hl * HEAD:(hl + 1) * HEAD], beta_ref[0][:, hl:hl + 1],
                      gc_ref[0][:, hl:hl + 1], gtot_ref[0][:, hl:hl + 1],
                      gct_ref[0, pl.ds(h0 + hl, 1), :]))
    if paired:
        terms = _paired_chunk_terms(chunk, heads)
    else:
        terms = [_chunk_terms(chunk, *head) for head in heads]
    terms = [t + (jnp.exp(head[5]),) for t, head in zip(terms, heads)]
    states = [s_scr[hl] for hl in range(hps)]
    outs = [[] for _ in range(hps)]
    for c in range(r // chunk):
        lo, hi = c * chunk, (c + 1) * chunk
        for hl in range(hps):
            u, w, a, qd, kd, egt = terms[hl]
            s = states[hl]
            pq = _dot(jnp.concatenate([w[lo:hi], qd[lo:hi]], axis=0), s)
            v_new = u[lo:hi] - pq[:chunk]
            if paired:
                intra = _dot(a[c], jnp.concatenate([v_new, v_new], axis=0))
            else:
                intra = _dot(a[lo:hi, lo:hi], v_new)
            outs[hl].append(pq[chunk:] + intra)
            states[hl] = s * egt[lo:lo + 1] + _dot_tn(kd[lo:hi], v_new)
    for hl in range(hps):
        s_scr[hl] = states[hl]
        o = jnp.concatenate(outs[hl], axis=0)
        o_ref[:, hl * HEAD:(hl + 1) * HEAD] = _gated_norm(o, z_ref[:, hl * HEAD:(hl + 1) * HEAD], gn_ref[...])
    s_ref[0] = s_scr[...]


def _gdn_prompt(qkv, proj, beta, gc, gtot, gct, g_onorm, n_seq, t_len, d, chunk, r=256, hps=8):
    nvh = 2 * d // HEAD
    ns = t_len // r
    kc = hps // 2 * HEAD
    vc = hps * HEAD
    rows = lambda b, g, s: b * ns + s
    return pl.pallas_call(
        functools.partial(_gdn_prompt_kernel, chunk, hps),
        grid=(n_seq, nvh // hps, ns),
        in_specs=[pl.BlockSpec((r, kc), lambda b, g, s: (rows(b, g, s), g)),
                  pl.BlockSpec((r, kc), lambda b, g, s: (rows(b, g, s), d // kc + g)),
                  pl.BlockSpec((r, vc), lambda b, g, s: (rows(b, g, s), 2 * d // vc + g)),
                  pl.BlockSpec((r, vc), lambda b, g, s: (rows(b, g, s), 4 * d // vc + g)),
                  pl.BlockSpec((1, r, hps), lambda b, g, s: (g, rows(b, g, s), 0)),
                  pl.BlockSpec((1, r, hps), lambda b, g, s: (g, rows(b, g, s), 0)),
                  pl.BlockSpec((1, r, hps), lambda b, g, s: (g, rows(b, g, s), 0)),
                  pl.BlockSpec((1, nvh, r), lambda b, g, s: (rows(b, g, s), 0, 0)),
                  pl.BlockSpec((1, HEAD), lambda b, g, s: (0, 0))],
        out_specs=[pl.BlockSpec((r, vc), lambda b, g, s: (rows(b, g, s), g)),
                   pl.BlockSpec((1, hps, HEAD, HEAD), lambda b, g, s: (b, g, 0, 0))],
        out_shape=[jax.ShapeDtypeStruct((n_seq * t_len, 2 * d), BF16),
                   jax.ShapeDtypeStruct((n_seq, nvh, HEAD, HEAD), F32)],
        scratch_shapes=[pltpu.VMEM((hps, HEAD, HEAD), F32)],
        compiler_params=_cparams("parallel", "parallel", "arbitrary"),
        name="gdn_prompt",
    )(qkv, qkv, qkv, proj, beta, gc, gtot, gct, g_onorm.reshape(1, HEAD))


def _gdn_sample_kernel(chunk, hps, q_ref, k_ref, v_ref, z_ref, beta_ref, gc_ref, gtot_ref, gct_ref, gn_ref, s0_ref,
                       o_ref, s_ref, wq_scr, kd_scr, eg_scr, pq_scr):
    r = q_ref.shape[0]
    h0 = pl.program_id(0) * hps
    per_row = 8 // chunk
    terms = []
    for hl in range(hps):
        kh = hl // 2
        q = q_ref[:, kh * HEAD:(kh + 1) * HEAD]
        k = k_ref[:, kh * HEAD:(kh + 1) * HEAD]
        v = v_ref[:, hl * HEAD:(hl + 1) * HEAD]
        beta = beta_ref[0][:, hl:hl + 1]
        gc = gc_ref[0][:, hl:hl + 1]
        gtot = gtot_ref[0][:, hl:hl + 1]
        gc_row = gct_ref[0, pl.ds(h0 + hl, 1), :]
        u, w, a, qd, kd = _chunk_terms(chunk, q, k, v, beta, gc, gtot, gc_row)
        wq_scr[hl, :, 0:8, :] = w.reshape(r // 8, 8, HEAD)
        wq_scr[hl, :, 8:16, :] = qd.reshape(r // 8, 8, HEAD)
        kd_scr[hl] = kd
        eg_scr[hl] = jnp.broadcast_to(jnp.exp(gtot), (r, HEAD))
        terms.append((u, a))

    row16 = lax.broadcasted_iota(jnp.int32, (16, HEAD), 0)
    row8 = lax.broadcasted_iota(jnp.int32, (8, HEAD), 0)

    def pair_pred(p, carry):
        for hl in range(hps):
            lhs = wq_scr[hl, p]
            acc = jnp.zeros((16, HEAD), F32)
            for e in range(per_row):
                pq = _dot(lhs, s0_ref[p * per_row + e, hl])
                acc = jnp.where(_blk(row16 & 7, chunk) == e, pq, acc)
            pq_scr[hl, p] = acc
        return carry
    lax.fori_loop(0, r // 8, pair_pred, 0)

    v_news = []
    for hl in range(hps):
        u, a = terms[hl]
        pq = pq_scr[hl]
        v_new = u - pq[:, 0:8, :].reshape(r, HEAD)
        o = pq[:, 8:16, :].reshape(r, HEAD) + _dot(a, v_new)
        o_ref[:, hl * HEAD:(hl + 1) * HEAD] = _gated_norm(o, z_ref[:, hl * HEAD:(hl + 1) * HEAD], gn_ref[...])
        pq_scr[hl, :, 0:8, :] = v_new.reshape(r // 8, 8, HEAD)

    def pair_update(p, carry):
        for hl in range(hps):
            v_new = pq_scr[hl, p, 0:8, :]
            kd = kd_scr[hl, pl.ds(pl.multiple_of(p * 8, 8), 8), :]
            for e in range(per_row):
                b = p * per_row + e
                decay = eg_scr[hl, pl.ds(p * 8 + e * chunk, 1), :]
                kd_e = jnp.where(_blk(row8, chunk) == e, kd, 0.0)
                s_ref[b, hl] = s0_ref[b, hl] * decay + _dot_tn(kd_e, v_new)
        return carry
    lax.fori_loop(0, r // 8, pair_update, 0)


def _gdn_sample(qkv, z, beta, gc, gtot, gct, g_onorm, state, d, chunk, r=128, hps=4):
    n = qkv.shape[0]
    nvh = 2 * d // HEAD
    kc = hps // 2 * HEAD
    vc = hps * HEAD
    nb = r // chunk
    return pl.pallas_call(
        functools.partial(_gdn_sample_kernel, chunk, hps),
        grid=(nvh // hps, n // r),
        in_specs=[pl.BlockSpec((r, kc), lambda g, s: (s, g)),
                  pl.BlockSpec((r, kc), lambda g, s: (s, d // kc + g)),
                  pl.BlockSpec((r, vc), lambda g, s: (s, 2 * d // vc + g)),
                  pl.BlockSpec((r, vc), lambda g, s: (s, g)),
                  pl.BlockSpec((1, r, hps), lambda g, s: (g, s, 0)),
                  pl.BlockSpec((1, r, hps), lambda g, s: (g, s, 0)),
                  pl.BlockSpec((1, r, hps), lambda g, s: (g, s, 0)),
                  pl.BlockSpec((1, nvh, r), lambda g, s: (s, 0, 0)),
                  pl.BlockSpec((1, HEAD), lambda g, s: (0, 0)),
                  pl.BlockSpec((nb, hps, HEAD, HEAD), lambda g, s: (s, g, 0, 0))],
        out_specs=[pl.BlockSpec((r, vc), lambda g, s: (s, g)),
                   pl.BlockSpec((nb, hps, HEAD, HEAD), lambda g, s: (s, g, 0, 0))],
        out_shape=[jax.ShapeDtypeStruct((n, 2 * d), BF16),
                   jax.ShapeDtypeStruct(state.shape, F32)],
        scratch_shapes=[pltpu.VMEM((hps, r // 8, 16, HEAD), F32), pltpu.VMEM((hps, r, HEAD), F32),
                        pltpu.VMEM((hps, r, HEAD), F32), pltpu.VMEM((hps, r // 8, 16, HEAD), F32)],
        compiler_params=_cparams("parallel", "parallel"),
        name="gdn_sample",
    )(qkv, qkv, qkv, z, beta, gc, gtot, gct, g_onorm.reshape(1, HEAD), state)


def kernel(x_prompt, x_sample, c_prompt, c_sample, state_gdn, state_gdn_conv, state_sconv, w_ada, b_ada,
           g_norm_mix, g_norm_ffn, g_norm_out, gdn_w_in, gdn_conv_w, gdn_a_log, gdn_dt_bias, gdn_g_onorm,
           gdn_w_out, sc_w_in, sc_conv_w, sc_w_out, ffn_w_up, ffn_w_down, moe_w_router, moe_b_router,
           moe_w_up, moe_w_down):
    nbp, t_len, d = x_prompt.shape
    nbs, steps, _ = x_sample.shape
    n_p, n_s = nbp * t_len, nbs * steps
    nvh = 2 * d // HEAD
    chunk_p = min(64, t_len)
    tm = nbs
    assert t_len % tm == 0 and n_p % n_s == 0 and 8 % steps == 0

    n = n_p + n_s
    tm_mm = _tile(n, 1088)
    x_in = (x_prompt.reshape(n_p, d), x_sample.reshape(nbs, steps * d))
    pad = (-(nbp + nbs)) % 8
    c_all = jnp.concatenate([c_prompt, c_sample, jnp.zeros((pad, d), F32)], axis=0)
    m = _ada(c_all, w_ada, b_ada)
    mods = [_Mod(m[i, :nbp].reshape(nbp, 1, 6 * d), m[i, nbp:nbp + nbs], d, tm, n_p // tm, t_len // tm)
            for i in range(2)]
    SH1, SC1, GA1, SH2, SC2, GA2 = range(6)

    h = _prep(x_in, g_norm_mix[0], mods[0], SC1, SH1, n, tm)
    w_in_t = jnp.swapaxes(gdn_w_in[0], 0, 1)
    proj = _matmul(h, w_in_t, 6 * d, _tile(6 * d, 1024), tm_mm, name="gdn_in_proj", w_is_transposed=True)
    ba = _matmul(h, w_in_t, 2 * nvh, 2 * nvh, tm_mm, name="gdn_in_proj_ba", w_is_transposed=True,
                 col_block0=6 * d // (2 * nvh))
    conv_w = gdn_conv_w[0]
    qkv_p = _gconv_prompt(proj, conv_w, nbp, t_len, d)
    st_gconv = jnp.transpose(state_gdn_conv[0], (1, 0, 2))
    qkv_s = _gconv_sample(proj, st_gconv, conv_w, n_p // n_s, nbs, n_s, d)
    to_seq_major = lambda a: a.reshape(steps, nbs, a.shape[-1]).transpose(1, 0, 2).reshape(n_s, a.shape[-1])
    qkv_s = to_seq_major(qkv_s)
    z_s = to_seq_major(proj[n_p:, 4 * d:6 * d])
    ba_s = to_seq_major(ba[n_p:])
    hps_p, hps_s = min(8, nvh), min(4, nvh)
    beta_p, gc_p, gtot_p, gct_p = _gates(ba[:n_p], gdn_a_log[0], gdn_dt_bias[0], chunk_p, 256, hps_p)
    beta_s, gc_s, gtot_s, gct_s = _gates(ba_s, gdn_a_log[0], gdn_dt_bias[0], steps, 128, hps_s)
    o_p, gdn_p = _gdn_prompt(qkv_p, proj, beta_p, gc_p, gtot_p, gct_p, gdn_g_onorm[0], nbp, t_len, d, chunk_p,
                             hps=hps_p)
    o_s, gdn_s = _gdn_sample(qkv_s, z_s, beta_s, gc_s, gtot_s, gct_s, gdn_g_onorm[0], state_gdn[0], d, steps,
                             hps=hps_s)
    tm_p = _tile(n_p, 1024)
    y = (_matmul(o_p, gdn_w_out[0], d, _tile(d, 512), tm_p, name="gdn_out_proj"),
         _matmul(o_s, gdn_w_out[0], d, _tile(d, 512), n_s, name="gdn_out_proj_sample").reshape(nbs, steps * d))
    gconv_p = jnp.stack([proj[(b + 1) * t_len - 3:(b + 1) * t_len, :4 * d] for b in range(nbp)])
    gconv_s = proj[n_p + nbs:, :4 * d].reshape(3, nbs, 4 * d).transpose(1, 0, 2)

    x, h = _resid_prep(x_in, y, g_norm_ffn[0], mods[0], GA1, mods[0], SC2, SH2, n, tm)
    one_expert = jnp.zeros((n // tm_mm,), jnp.int32)
    all_tiles = jnp.full((1,), n // tm_mm, jnp.int32)
    act = _swiglu_up(h, ffn_w_up, one_expert, all_tiles, tm_mm, _tile(ffn_w_down.shape[1], 512), "ffn_swiglu_up")
    f = _matmul(act, ffn_w_down[0], d, _tile(d, 512), name="ffn_down")

    x, h = _resid_prep(x, f, g_norm_mix[1], mods[0], GA2, mods[1], SC1, SH1, n, tm)
    s_in = _matmul(h, sc_w_in[0], 3 * d, _tile(3 * d, 1024), tm_mm, name="sconv_in_proj")
    mix_p, sconv_tail = _sconv_prompt(s_in, sc_conv_w[0], nbp, t_len, d)
    mix_s, u_s = _sconv_sample(s_in, state_sconv[0].reshape(nbs, 2 * d), sc_conv_w[0], n_p // n_s, nbs, n_s, d)
    y_s = _matmul(mix_s, sc_w_out[0], d, _tile(d, 1024), n_s, name="sconv_out_proj_sample")
    y = (_matmul(mix_p, sc_w_out[0], d, _tile(d, 1024), tm_p, name="sconv_out_proj"),
         y_s.reshape(steps, nbs, d).transpose(1, 0, 2).reshape(nbs, steps * d))
    sconv_p = sconv_tail[:, 6:8]
    sconv_s = u_s[n_s - 2 * nbs:].reshape(2, nbs, d).transpose(1, 0, 2)

    x, h, info = _resid_route(x, y, g_norm_ffn[1], mods[1], GA1, mods[1], SC2, SH2,
                              moe_w_router[0], moe_b_router[0], tm)
    _, n_exp, f_moe, _ = moe_w_down.shape
    tmg = 512
    pos, src, te, nv = _route_tables(info, n_exp, tmg)
    h_sorted = _gather_rows(h, src, nv, tmg)
    act = _swiglu_up(h_sorted, moe_w_up[0], te, nv, tmg, _tile(f_moe, 1024), "moe_swiglu_up")
    y_sorted = _grouped_down(act, moe_w_down[0], te, nv, tmg, _tile(d, 512), "moe_down")
    out_p, out_s = _final(x, y_sorted, pos, info, g_norm_out, mods[1], GA2, n_p, nbs, tm)

    y_prompt = out_p.reshape(nbp, t_len, d)
    y_sample = out_s.reshape(nbs, steps, d)
    return (y_prompt, y_sample, gdn_p[None], gconv_p[None], sconv_p[None], gdn_s[None], gconv_s[None], sconv_s[None])
```

```python
import functools

import jax
import jax.numpy as jnp
from jax import lax
from jax.experimental import pallas as pl
from jax.experimental.pallas import tpu as pltpu

F32 = jnp.float32
BF16 = jnp.bfloat16
NORM_EPS = 1e-6
HEAD = 128
V7X_VMEM_LIMIT = 56 * 1024 * 1024
HIGHEST = lax.Precision.HIGHEST


def _cparams(*sem):
    return pltpu.CompilerParams(dimension_semantics=sem, vmem_limit_bytes=V7X_VMEM_LIMIT)


def _tile(n, pref):
    return pref if n % pref == 0 else n


def _silu(x):
    return x * jax.nn.sigmoid(x)


def _softplus(x):
    return jnp.maximum(x, 0.0) + jnp.log1p(jnp.exp(-jnp.abs(x)))


def _blk(idx, size):
    assert size & (size - 1) == 0
    return lax.shift_right_logical(idx, size.bit_length() - 1)


def _dot(a, b):
    return jnp.dot(a.astype(BF16), b.astype(BF16), preferred_element_type=F32)


def _dot_nt(a, b):
    return lax.dot_general(a.astype(BF16), b.astype(BF16), (((1,), (1,)), ((), ())), preferred_element_type=F32)


def _dot_tn(a, b):
    return lax.dot_general(a.astype(BF16), b.astype(BF16), (((0,), (0,)), ((), ())), preferred_element_type=F32)


def _ada_kernel(c_ref, w_ref, b_ref, o_ref):
    c = c_ref[...]
    o_ref[0] = _dot(_silu(c), w_ref[0]) + b_ref[0]


def _ada(c_all, w_ada, b_ada):
    nl, d, n6 = w_ada.shape
    r = c_all.shape[0]
    tn = _tile(n6, 1024)
    return pl.pallas_call(
        _ada_kernel,
        grid=(nl, n6 // tn),
        in_specs=[pl.BlockSpec((r, d), lambda l, j: (0, 0)),
                  pl.BlockSpec((1, d, tn), lambda l, j: (l, 0, j)),
                  pl.BlockSpec((1, 1, tn), lambda l, j: (l, 0, j))],
        out_specs=pl.BlockSpec((1, r, tn), lambda l, j: (l, 0, j)),
        out_shape=jax.ShapeDtypeStruct((nl, r, n6), F32),
        compiler_params=_cparams("parallel", "parallel"),
        name="ada_mod",
    )(c_all, w_ada, b_ada.reshape(nl, 1, n6))


class _Mod:
    def __init__(self, mod_p, mod_s, d, tm, n_ptiles, tiles_per_seq):
        self.arrays = (mod_p, mod_s)
        self.n_ptiles = n_ptiles
        nb = mod_p.shape[0]

        def specs(chunk):
            return [pl.BlockSpec((1, 1, d), lambda i, *_: (jnp.minimum(i // tiles_per_seq, nb - 1), 0, chunk)),
                    pl.BlockSpec((tm, d), lambda i, *_: (0, chunk))]
        self.specs = specs
        self.row_specs = [pl.BlockSpec((tm, d), lambda i, *_: (jnp.minimum(i, n_ptiles - 1), 0)),
                          pl.BlockSpec((tm, d), lambda i, *_: (0, jnp.maximum(i - n_ptiles, 0)))]

    def is_prompt(self):
        return pl.program_id(0) < self.n_ptiles

    def pick(self, p_ref, s_ref):
        return jnp.where(self.is_prompt(), p_ref[0], s_ref[...])

    def pick_rows(self, p_ref, s_ref):
        return jnp.where(self.is_prompt(), p_ref[...], s_ref[...])


def _rms(x, gain):
    ms = jnp.mean(x * x, axis=-1, keepdims=True)
    return x * lax.rsqrt(ms + NORM_EPS) * gain


def _prep_kernel(mod, xp_ref, xs_ref, g_ref, scp, scs, shp, shs, h_ref):
    y = _rms(mod.pick_rows(xp_ref, xs_ref), g_ref[...])
    h_ref[...] = (y * (1.0 + mod.pick(scp, scs)) + mod.pick(shp, shs)).astype(h_ref.dtype)


def _prep(x_pair, gain, mod, sc_chunk, sh_chunk, n, tm):
    d = gain.shape[0]
    return pl.pallas_call(
        functools.partial(_prep_kernel, mod),
        grid=(n // tm,),
        in_specs=mod.row_specs + [pl.BlockSpec((1, d), lambda i: (0, 0))] + mod.specs(sc_chunk) + mod.specs(sh_chunk),
        out_specs=pl.BlockSpec((tm, d), lambda i: (i, 0)),
        out_shape=jax.ShapeDtypeStruct((n, d), BF16),
        compiler_params=_cparams("parallel"),
        name="norm_mod",
    )(*x_pair, gain.reshape(1, d), *mod.arrays, *mod.arrays)


def _row_inputs(mod, a, tm, d):
    if isinstance(a, tuple):
        return mod.row_specs, list(a)
    return [pl.BlockSpec((tm, d), lambda i, *_: (i, 0))], [a]


def _read_rows(mod, refs):
    return mod.pick_rows(*refs) if len(refs) == 2 else refs[0][...]


def _resid_prep_kernel(mod, nx, ny, *refs):
    x_refs, y_refs = refs[:nx], refs[nx:nx + ny]
    g_ref, gap, gas, scp, scs, shp, shs, xo_ref, h_ref = refs[nx + ny:]
    x = _read_rows(mod, x_refs) + mod.pick(gap, gas) * _read_rows(mod, y_refs)
    xo_ref[...] = x
    y = _rms(x, g_ref[...])
    h_ref[...] = (y * (1.0 + mod.pick(scp, scs)) + mod.pick(shp, shs)).astype(h_ref.dtype)


def _resid_prep(x, y, gain, mod_gate, ga_chunk, mod, sc_chunk, sh_chunk, n, tm):
    d = gain.shape[0]
    row = pl.BlockSpec((tm, d), lambda i: (i, 0))
    x_specs, x_ops = _row_inputs(mod, x, tm, d)
    y_specs, y_ops = _row_inputs(mod, y, tm, d)
    return pl.pallas_call(
        functools.partial(_resid_prep_kernel, mod, len(x_ops), len(y_ops)),
        grid=(n // tm,),
        in_specs=x_specs + y_specs + [pl.BlockSpec((1, d), lambda i: (0, 0))]
        + mod_gate.specs(ga_chunk) + mod.specs(sc_chunk) + mod.specs(sh_chunk),
        out_specs=[row, row],
        out_shape=[jax.ShapeDtypeStruct((n, d), F32), jax.ShapeDtypeStruct((n, d), BF16)],
        compiler_params=_cparams("parallel"),
        name="resid_norm_mod",
    )(*x_ops, *y_ops, gain.reshape(1, d), *mod_gate.arrays, *mod.arrays, *mod.arrays)


def _resid_route_kernel(mod, n_exp, ny, x_ref, *refs):
    y_refs = refs[:ny]
    g_ref, gap, gas, scp, scs, shp, shs, wr_ref, br_ref, xo_ref, h_ref, info_ref = refs[ny:]
    x = x_ref[...] + mod.pick(gap, gas) * _read_rows(mod, y_refs)
    xo_ref[...] = x
    h = _rms(x, g_ref[...]) * (1.0 + mod.pick(scp, scs)) + mod.pick(shp, shs)
    h_ref[...] = h
    logits = jnp.dot(h, wr_ref[...], precision=HIGHEST, preferred_element_type=F32) + br_ref[...]
    lane = lax.broadcasted_iota(jnp.int32, logits.shape, 1).astype(F32)
    m1 = jnp.max(logits, axis=-1, keepdims=True)
    i1 = jnp.min(jnp.where(logits == m1, lane, float(n_exp)), axis=-1, keepdims=True)
    rest = jnp.where(lane == i1, -jnp.inf, logits)
    m2 = jnp.max(rest, axis=-1, keepdims=True)
    i2 = jnp.min(jnp.where(rest == m2, lane, float(n_exp)), axis=-1, keepdims=True)
    e2 = jnp.exp(m2 - m1)
    g1 = 1.0 / (1.0 + e2)
    g2 = e2 / (1.0 + e2)
    info_ref[...] = (jnp.where(lane == 0.0, i1, 0.0) + jnp.where(lane == 1.0, i2, 0.0)
                     + jnp.where(lane == 2.0, g1, 0.0) + jnp.where(lane == 3.0, g2, 0.0))


def _resid_route(x, y, gain, mod_gate, ga_chunk, mod, sc_chunk, sh_chunk, w_router, b_router, tm):
    n, d = x.shape
    n_exp = w_router.shape[1]
    row = pl.BlockSpec((tm, d), lambda i: (i, 0))
    y_specs, y_ops = _row_inputs(mod, y, tm, d)
    return pl.pallas_call(
        functools.partial(_resid_route_kernel, mod, n_exp, len(y_ops)),
        grid=(n // tm,),
        in_specs=[row] + y_specs + [pl.BlockSpec((1, d), lambda i: (0, 0))]
        + mod_gate.specs(ga_chunk) + mod.specs(sc_chunk) + mod.specs(sh_chunk)
        + [pl.BlockSpec((d, n_exp), lambda i: (0, 0)), pl.BlockSpec((1, n_exp), lambda i: (0, 0))],
        out_specs=[row, row, pl.BlockSpec((tm, n_exp), lambda i: (i, 0))],
        out_shape=[jax.ShapeDtypeStruct((n, d), F32), jax.ShapeDtypeStruct((n, d), F32),
                   jax.ShapeDtypeStruct((n, n_exp), F32)],
        compiler_params=_cparams("parallel"),
        name="resid_norm_route",
    )(x, *y_ops, gain.reshape(1, d), *mod_gate.arrays, *mod.arrays, *mod.arrays, w_router,
      b_router.reshape(1, n_exp))


def _row_copy(src_hbm, row, dst, r, sem):
    return pltpu.make_async_copy(src_hbm.at[pl.ds(row, 1)], dst.at[pl.ds(r, 1)], sem)


def _final_kernel(mod, pos_ref, x_ref, y_hbm, info_ref, g_ref, gap, gas, op_ref, os_ref, buf, sem):
    tm = x_ref.shape[0]
    i = pl.program_id(0)
    n_tiles = pl.num_programs(0)

    def issue(tile, slot):
        def body(r, carry):
            for k in range(2):
                _row_copy(y_hbm, pos_ref[tile * (2 * tm) + 2 * r + k], buf.at[slot, k], r, sem.at[slot, k]).start()
            return carry
        lax.fori_loop(0, tm, body, 0, unroll=4)

    @pl.when(i == 0)
    def _():
        issue(0, 0)

    for slot in range(2):
        @pl.when((i + 1 < n_tiles) & ((i + 1) % 2 == slot))
        def _():
            issue(i + 1, slot)

    for slot in range(2):
        @pl.when(i % 2 == slot)
        def _():
            for k in range(2):
                pltpu.make_async_copy(y_hbm.at[pl.ds(0, tm)], buf.at[slot, k], sem.at[slot, k]).wait()
            info = info_ref[...]
            f = info[:, 2:3] * buf[slot, 0] + info[:, 3:4] * buf[slot, 1]
            x = x_ref[...] + mod.pick(gap, gas) * f
            out = _rms(x, g_ref[...])

            @pl.when(mod.is_prompt())
            def _():
                op_ref[...] = out

            @pl.when(jnp.logical_not(mod.is_prompt()))
            def _():
                os_ref[...] = out


def _final(x, y_sorted, pos, info, gain, mod_gate, ga_chunk, n_p, nbs, tm):
    n, d = x.shape
    row = pl.BlockSpec((tm, d), lambda i, *_: (i, 0))
    return pl.pallas_call(
        functools.partial(_final_kernel, mod_gate),
        grid_spec=pltpu.PrefetchScalarGridSpec(
            num_scalar_prefetch=1,
            grid=(n // tm,),
            in_specs=[row, pl.BlockSpec(memory_space=pl.ANY),
                      pl.BlockSpec((tm, info.shape[1]), lambda i, *_: (i, 0)),
                      pl.BlockSpec((1, d), lambda i, *_: (0, 0))] + mod_gate.specs(ga_chunk),
            out_specs=mod_gate.row_specs,
            scratch_shapes=[pltpu.VMEM((2, 2, tm, d), F32), pltpu.SemaphoreType.DMA((2, 2))]),
        out_shape=[jax.ShapeDtypeStruct((n_p, d), F32), jax.ShapeDtypeStruct((nbs, (n - n_p) // nbs * d), F32)],
        compiler_params=_cparams("arbitrary"),
        name="moe_combine_final_norm",
    )(pos, x, y_sorted, info, gain.reshape(1, d), *mod_gate.arrays)


def _route_tables(info, n_exp, tmg):
    n = info.shape[0]
    e = info[:, :2].astype(jnp.int32)
    onehot = (e[:, :, None] == jnp.arange(n_exp, dtype=jnp.int32)[None, None, :]).astype(jnp.int32).sum(axis=1)
    counts = onehot.sum(axis=0)
    padded = (counts + tmg - 1) // tmg * tmg
    ends = jnp.cumsum(padded)
    starts = ends - padded
    rank = jnp.cumsum(onehot, axis=0) - onehot
    pos = (starts[e] + jnp.take_along_axis(rank, e, axis=1)).reshape(-1).astype(jnp.int32)
    n_tiles = -(-(2 * n) // tmg) + n_exp
    src = jnp.zeros((n_tiles * tmg,), jnp.int32).at[pos].set(jnp.repeat(jnp.arange(n, dtype=jnp.int32), 2))
    n_valid = (ends[-1] // tmg).astype(jnp.int32)
    tiles = jnp.arange(n_tiles, dtype=jnp.int32)
    te = jnp.minimum((tiles[:, None] * tmg >= ends[None, :]).astype(jnp.int32).sum(axis=1), n_exp - 1)
    te = jnp.where(tiles < n_valid, te, te[jnp.maximum(n_valid - 1, 0)])
    return pos, src, te, n_valid.reshape(1)


def _gather_kernel(src_ref, nv_ref, h_hbm, o_ref, buf, sem):
    i = pl.program_id(0)
    tmg = buf.shape[1]
    nv = nv_ref[0]

    def issue(tile, slot):
        def body(r, carry):
            _row_copy(h_hbm, src_ref[tile * tmg + r], buf.at[slot], r, sem.at[slot]).start()
            return carry
        lax.fori_loop(0, tmg, body, 0, unroll=8)

    @pl.when((i == 0) & (nv > 0))
    def _():
        issue(0, 0)

    for slot in range(2):
        @pl.when((i + 1 < nv) & ((i + 1) % 2 == slot))
        def _():
            issue(i + 1, slot)

    for slot in range(2):
        @pl.when((i < nv) & (i % 2 == slot))
        def _():
            pltpu.make_async_copy(h_hbm.at[pl.ds(0, tmg)], buf.at[slot], sem.at[slot]).wait()
            o_ref[...] = buf[slot].astype(o_ref.dtype)

    @pl.when(i >= nv)
    def _():
        o_ref[...] = jnp.zeros_like(o_ref)


def _gather_rows(h, src, n_valid, tmg):
    d = h.shape[1]
    n_tiles = src.shape[0] // tmg
    return pl.pallas_call(
        _gather_kernel,
        grid_spec=pltpu.PrefetchScalarGridSpec(
            num_scalar_prefetch=2,
            grid=(n_tiles,),
            in_specs=[pl.BlockSpec(memory_space=pl.ANY)],
            out_specs=pl.BlockSpec((tmg, d), lambda i, src, nv: (i, 0)),
            scratch_shapes=[pltpu.VMEM((2, tmg, d), F32), pltpu.SemaphoreType.DMA((2,))]),
        out_shape=jax.ShapeDtypeStruct((n_tiles * tmg, d), BF16),
        compiler_params=_cparams("arbitrary"),
        name="moe_gather",
    )(src, n_valid, h)


def _mm_kernel(w_is_transposed, a_ref, w_ref, o_ref, wb_ref):
    @pl.when(pl.program_id(1) == 0)
    def _():
        wb_ref[...] = w_ref[...].astype(BF16)
    if w_is_transposed:
        acc = lax.dot_general(a_ref[...], wb_ref[...], (((1,), (1,)), ((), ())), preferred_element_type=F32)
    else:
        acc = jnp.dot(a_ref[...], wb_ref[...], preferred_element_type=F32)
    o_ref[...] = acc.astype(o_ref.dtype)


def _matmul(a, w, n_out, tn, tm=512, out_dtype=F32, name="matmul", w_is_transposed=False, col_block0=0):
    m, k = a.shape
    tm = _tile(m, tm)
    if w_is_transposed:
        w_spec = pl.BlockSpec((tn, k), lambda j, i: (col_block0 + j, 0))
        wb_shape = (tn, k)
    else:
        w_spec = pl.BlockSpec((k, tn), lambda j, i: (0, col_block0 + j))
        wb_shape = (k, tn)
    return pl.pallas_call(
        functools.partial(_mm_kernel, w_is_transposed),
        grid=(n_out // tn, m // tm),
        in_specs=[pl.BlockSpec((tm, k), lambda j, i: (i, 0)), w_spec],
        out_specs=pl.BlockSpec((tm, tn), lambda j, i: (i, j)),
        out_shape=jax.ShapeDtypeStruct((m, n_out), out_dtype),
        scratch_shapes=[pltpu.VMEM(wb_shape, BF16)],
        compiler_params=_cparams("parallel", "arbitrary"),
        name=name,
    )(a, w)


def _new_expert(te_ref):
    i = pl.program_id(1)
    return (i == 0) | (te_ref[i] != te_ref[jnp.maximum(i - 1, 0)])


def _swiglu_up_kernel(te_ref, nv_ref, a_ref, wg_ref, wu_ref, o_ref, wgb_ref, wub_ref):
    @pl.when(_new_expert(te_ref))
    def _():
        wgb_ref[...] = wg_ref[0].astype(BF16)
        wub_ref[...] = wu_ref[0].astype(BF16)

    @pl.when(pl.program_id(1) < nv_ref[0])
    def _():
        a = a_ref[...]
        gate = jnp.dot(a, wgb_ref[...], preferred_element_type=F32)
        up = jnp.dot(a, wub_ref[...], preferred_element_type=F32)
        o_ref[...] = (_silu(gate) * up).astype(o_ref.dtype)

    @pl.when(pl.program_id(1) >= nv_ref[0])
    def _():
        o_ref[...] = jnp.zeros_like(o_ref)


def _swiglu_up(a, w_up, te, nv, tm, tn, name):
    m, k = a.shape
    f = w_up.shape[2] // 2
    nj = f // tn
    row = lambda j, i, te, nv: jnp.minimum(i, nv[0] - 1)
    return pl.pallas_call(
        _swiglu_up_kernel,
        grid_spec=pltpu.PrefetchScalarGridSpec(
            num_scalar_prefetch=2,
            grid=(nj, m // tm),
            in_specs=[pl.BlockSpec((tm, k), lambda j, i, te, nv: (row(j, i, te, nv), 0)),
                      pl.BlockSpec((1, k, tn), lambda j, i, te, nv: (te[i], 0, j)),
                      pl.BlockSpec((1, k, tn), lambda j, i, te, nv: (te[i], 0, nj + j))],
            out_specs=pl.BlockSpec((tm, tn), lambda j, i, te, nv: (i, j)),
            scratch_shapes=[pltpu.VMEM((k, tn), BF16), pltpu.VMEM((k, tn), BF16)]),
        out_shape=jax.ShapeDtypeStruct((m, f), BF16),
        compiler_params=_cparams("parallel", "arbitrary"),
        name=name,
    )(te, nv, a, w_up, w_up)


def _down_kernel(te_ref, nv_ref, a_ref, w_ref, o_ref, wb_ref):
    @pl.when(_new_expert(te_ref))
    def _():
        wb_ref[...] = w_ref[0].astype(BF16)

    @pl.when(pl.program_id(1) < nv_ref[0])
    def _():
        o_ref[...] = jnp.dot(a_ref[...], wb_ref[...], preferred_element_type=F32)

    @pl.when(pl.program_id(1) >= nv_ref[0])
    def _():
        o_ref[...] = jnp.zeros_like(o_ref)


def _grouped_down(a, w_down, te, nv, tm, tn, name):
    m, f = a.shape
    d = w_down.shape[2]
    row = lambda j, i, te, nv: jnp.minimum(i, nv[0] - 1)
    return pl.pallas_call(
        _down_kernel,
        grid_spec=pltpu.PrefetchScalarGridSpec(
            num_scalar_prefetch=2,
            grid=(d // tn, m // tm),
            in_specs=[pl.BlockSpec((tm, f), lambda j, i, te, nv: (row(j, i, te, nv), 0)),
                      pl.BlockSpec((1, f, tn), lambda j, i, te, nv: (te[i], 0, j))],
            out_specs=pl.BlockSpec((tm, tn), lambda j, i, te, nv: (i, j)),
            scratch_shapes=[pltpu.VMEM((f, tn), BF16)]),
        out_shape=jax.ShapeDtypeStruct((m, d), F32),
        compiler_params=_cparams("parallel", "arbitrary"),
        name=name,
    )(te, nv, a, w_down)


def _shift_rows(carry, u, j):
    ext = jnp.concatenate([carry, u], axis=0)
    return pltpu.roll(ext, j, 0)[8:]


def _sconv_prompt_kernel(bg_ref, cg_ref, xi_ref, w_ref, o_ref, st_ref, carry_ref):
    @pl.when(pl.program_id(1) == 0)
    def _():
        carry_ref[...] = jnp.zeros_like(carry_ref)
    u = cg_ref[...] * xi_ref[...]
    carry = carry_ref[...]
    w = w_ref[...]
    y = w[2:3] * u + w[1:2] * _shift_rows(carry, u, 1) + w[0:1] * _shift_rows(carry, u, 2)
    o_ref[...] = (bg_ref[...] * y).astype(o_ref.dtype)
    tail = u[u.shape[0] - 8:]
    carry_ref[...] = tail
    st_ref[0] = tail


def _sconv_prompt(s_in, conv_w, n_seq, t_len, d, tt=256):
    tt = _tile(t_len, tt)
    nt = t_len // tt
    blk = lambda c: pl.BlockSpec((tt, d), lambda b, t: (b * nt + t, c))
    return pl.pallas_call(
        _sconv_prompt_kernel,
        grid=(n_seq, nt),
        in_specs=[blk(0), blk(1), blk(2), pl.BlockSpec((3, d), lambda b, t: (0, 0))],
        out_specs=[pl.BlockSpec((tt, d), lambda b, t: (b * nt + t, 0)),
                   pl.BlockSpec((1, 8, d), lambda b, t: (b, 0, 0))],
        out_shape=[jax.ShapeDtypeStruct((n_seq * t_len, d), BF16), jax.ShapeDtypeStruct((n_seq, 8, d), F32)],
        scratch_shapes=[pltpu.VMEM((8, d), F32)],
        compiler_params=_cparams("parallel", "arbitrary"),
        name="sconv_prompt",
    )(s_in, s_in, s_in, conv_w)


def _sconv_sample_kernel(nb, bg_ref, cg_ref, xi_ref, s0_ref, s1_ref, w_ref, o_ref, u_ref):
    u = cg_ref[...] * xi_ref[...]
    full = jnp.concatenate([s0_ref[...], s1_ref[...], u], axis=0)
    n = u.shape[0]
    w = w_ref[...]
    y = w[0:1] * full[0:n] + w[1:2] * full[nb:nb + n] + w[2:3] * full[2 * nb:2 * nb + n]
    o_ref[...] = (bg_ref[...] * y).astype(o_ref.dtype)
    u_ref[...] = u


def _sconv_sample(s_in, state2d, conv_w, row_block, nb, ns, d, ct=512):
    ct = _tile(d, ct)
    nj = d // ct
    blk = lambda c: pl.BlockSpec((ns, ct), lambda j: (row_block, c * nj + j))
    st = lambda s: pl.BlockSpec((nb, ct), lambda j: (0, s * nj + j))
    return pl.pallas_call(
        functools.partial(_sconv_sample_kernel, nb),
        grid=(nj,),
        in_specs=[blk(0), blk(1), blk(2), st(0), st(1), pl.BlockSpec((3, ct), lambda j: (0, j))],
        out_specs=[pl.BlockSpec((ns, ct), lambda j: (0, j)), pl.BlockSpec((ns, ct), lambda j: (0, j))],
        out_shape=[jax.ShapeDtypeStruct((ns, d), BF16), jax.ShapeDtypeStruct((ns, d), F32)],
        compiler_params=_cparams("parallel"),
        name="sconv_sample",
    )(s_in, s_in, s_in, state2d, state2d, conv_w)


def _l2norm_heads(x):
    parts = []
    for h in range(x.shape[1] // HEAD):
        seg = x[:, h * HEAD:(h + 1) * HEAD]
        parts.append(seg * lax.rsqrt(jnp.sum(seg * seg, axis=-1, keepdims=True) + NORM_EPS))
    return jnp.concatenate(parts, axis=1) if len(parts) > 1 else parts[0]


def _store_qkv(o_ref, act, is_qk):
    @pl.when(is_qk)
    def _():
        o_ref[...] = _l2norm_heads(act)

    @pl.when(jnp.logical_not(is_qk))
    def _():
        o_ref[...] = act


def _gconv_sample_kernel(n_qk_tiles, nb, s0_ref, s1_ref, s2_ref, x_ref, w_ref, o_ref):
    u = x_ref[...]
    full = jnp.concatenate([s0_ref[0], s1_ref[0], s2_ref[0], u], axis=0)
    n = u.shape[0]
    w = w_ref[...]
    y = (w[0:1] * full[0:n] + w[1:2] * full[nb:nb + n] + w[2:3] * full[2 * nb:2 * nb + n]
         + w[3:4] * full[3 * nb:3 * nb + n])
    _store_qkv(o_ref, _silu(y), pl.program_id(0) < n_qk_tiles)


def _gconv_sample(proj, state_tm, conv_w, row_block, nb, ns, d, ct=512):
    ch = 4 * d
    ct = _tile(ch, ct)
    nj = ch // ct
    st = lambda s: pl.BlockSpec((1, nb, ct), lambda j: (s, 0, j))
    return pl.pallas_call(
        functools.partial(_gconv_sample_kernel, 2 * d // ct, nb),
        grid=(nj,),
        in_specs=[st(0), st(1), st(2), pl.BlockSpec((ns, ct), lambda j: (row_block, j)),
                  pl.BlockSpec((4, ct), lambda j: (0, j))],
        out_specs=pl.BlockSpec((ns, ct), lambda j: (0, j)),
        out_shape=jax.ShapeDtypeStruct((ns, ch), F32),
        compiler_params=_cparams("parallel"),
        name="gdn_conv_sample",
    )(state_tm, state_tm, state_tm, proj, conv_w)


def _gates_kernel(chunk, ba_ref, alog_ref, dtb_ref, beta_ref, gc_ref, gtot_ref, gct_ref):
    ba = ba_ref[...]
    nh = ba.shape[1] // 2
    r = ba.shape[0]
    hps = beta_ref.shape[2]

    def store_by_group(ref, x):
        for grp in range(nh // hps):
            ref[grp] = x[:, grp * hps:(grp + 1) * hps]
    store_by_group(beta_ref, jax.nn.sigmoid(ba[:, :nh]))
    g = -jnp.exp(alog_ref[...]) * _softplus(ba[:, nh:] + dtb_ref[...])
    ri = lax.broadcasted_iota(jnp.int32, (r, r), 0)
    ci = lax.broadcasted_iota(jnp.int32, (r, r), 1)
    same = _blk(ri, chunk) == _blk(ci, chunk)
    tri = jnp.where(same & (ri >= ci), 1.0, 0.0)
    blk = jnp.where(same, 1.0, 0.0)
    gc = jnp.dot(tri, g, precision=HIGHEST, preferred_element_type=F32)
    store_by_group(gc_ref, gc)
    store_by_group(gtot_ref, jnp.dot(blk, g, precision=HIGHEST, preferred_element_type=F32))
    eye = jnp.where(lax.broadcasted_iota(jnp.int32, (nh, nh), 0) == lax.broadcasted_iota(jnp.int32, (nh, nh), 1),
                    1.0, 0.0)
    gct_ref[0] = lax.dot_general(eye, gc, (((1,), (1,)), ((), ())), precision=HIGHEST,
                                 preferred_element_type=F32)


def _gates(ba, a_log, dt_bias, chunk, r, hps):
    n, h2 = ba.shape
    nh = h2 // 2
    row = pl.BlockSpec((nh // hps, r, hps), lambda i: (0, i, 0))
    vec = pl.BlockSpec((1, nh), lambda i: (0, 0))
    return pl.pallas_call(
        functools.partial(_gates_kernel, chunk),
        grid=(n // r,),
        in_specs=[pl.BlockSpec((r, h2), lambda i: (i, 0)), vec, vec],
        out_specs=[row, row, row, pl.BlockSpec((1, nh, r), lambda i: (i, 0, 0))],
        out_shape=[jax.ShapeDtypeStruct((nh // hps, n, hps), F32)] * 3 + [jax.ShapeDtypeStruct((n // r, nh, r), F32)],
        compiler_params=_cparams("parallel"),
        name="gdn_gates",
    )(ba, a_log.reshape(1, nh), dt_bias.reshape(1, nh))


def _chunk_terms(chunk, q, k, v, beta, gc, gtot, gc_row):
    r = q.shape[0]
    ri = lax.broadcasted_iota(jnp.int32, (r, r), 0)
    ci = lax.broadcasted_iota(jnp.int32, (r, r), 1)
    causal = (_blk(ri, chunk) == _blk(ci, chunk)) & (ri >= ci)
    decay = jnp.exp(jnp.where(causal, gc - gc_row, -jnp.inf))
    kb = k * beta
    low = jnp.where(ri > ci, _dot_nt(kb, k) * decay, 0.0)
    e = -jnp.where(_blk(ri, 2) == _blk(ci, 2), low, 0.0)
    s = 2
    while s < chunk:
        lm = jnp.where((_blk(ri, 2 * s) == _blk(ci, 2 * s)) & (_blk(ri, s) != _blk(ci, s)), low, 0.0)
        x = lm + _dot(e, lm)
        e = e - (x + _dot(x, e))
        s *= 2
    eg = jnp.exp(gc)
    rhs = jnp.concatenate([v * beta, kb * eg], axis=1)
    uw = rhs + _dot(e, rhs)
    dv = v.shape[1]
    qs = q * (HEAD ** -0.5)
    a_intra = _dot_nt(qs, k) * decay
    return uw[:, :dv], uw[:, dv:], a_intra, qs * eg, k * jnp.exp(gtot - gc)


def _paired_chunk_terms(chunk, heads):
    r, pair = heads[0][0].shape[0], 2 * chunk
    ii = lax.broadcasted_iota(jnp.int32, (chunk, pair), 0)
    lane = lax.broadcasted_iota(jnp.int32, (chunk, pair), 1)
    right = lane >= chunk
    jj = lane & (chunk - 1)
    diag = lambda full: jnp.where(right, full[chunk:], full[:chunk])
    spread = lambda y: jnp.concatenate([jnp.where(right, 0.0, y), jnp.where(right, y, 0.0)], axis=0)
    slices = [slice(p * pair, (p + 1) * pair) for p in range(r // pair)]
    kbs = [k * beta for (q, k, v, beta, gc, gtot, gc_row) in heads]
    egs = [jnp.exp(gc) for (q, k, v, beta, gc, gtot, gc_row) in heads]
    qss = [q * (HEAD ** -0.5) for (q, k, v, beta, gc, gtot, gc_row) in heads]
    rhss = [jnp.concatenate([h[2] * h[3], kb * eg], axis=1) for h, kb, eg in zip(heads, kbs, egs)]
    probs = [(i, sl) for i in range(len(heads)) for sl in slices]
    decays = []
    for i, sl in probs:
        gcp, gc_row = heads[i][4][sl], heads[i][6]
        decays.append(jnp.exp(jnp.where(ii >= jj, jnp.where(right, gcp[chunk:], gcp[:chunk]) - gc_row[:, sl],
                                        -jnp.inf)))
    kks = [_dot_nt(kbs[i][sl], heads[i][1][sl]) for i, sl in probs]
    lows = [jnp.where(ii > jj, diag(kk) * decay, 0.0) for kk, decay in zip(kks, decays)]
    es = [-jnp.where(_blk(ii, 2) == _blk(jj, 2), low, 0.0) for low in lows]
    s = 2
    while s < chunk:
        level = (_blk(ii, 2 * s) == _blk(jj, 2 * s)) & (_blk(ii, s) != _blk(jj, s))
        lms = [jnp.where(level, low, 0.0) for low in lows]
        lm_sp = [spread(lm).astype(BF16) for lm in lms]
        xs = [lm + _dot(e, sp) for lm, e, sp in zip(lms, es, lm_sp)]
        e_sp = [spread(e).astype(BF16) for e in es]
        es = [e - (x + _dot(x, sp)) for e, x, sp in zip(es, xs, e_sp)]
        s *= 2
    e_sp = [spread(e).astype(BF16) for e in es]
    uws = [rhss[i][sl] + _dot(sp, rhss[i][sl]) for (i, sl), sp in zip(probs, e_sp)]
    qks = [_dot_nt(qss[i][sl], heads[i][1][sl]) for i, sl in probs]
    atts = [diag(qk) * decay for qk, decay in zip(qks, decays)]
    out = []
    n_pairs = len(slices)
    for i, (q, k, v, beta, gc, gtot, gc_row) in enumerate(heads):
        uw = jnp.concatenate(uws[i * n_pairs:(i + 1) * n_pairs], axis=0)
        a_chunks = []
        for a in atts[i * n_pairs:(i + 1) * n_pairs]:
            a_chunks += [jnp.where(right, 0.0, a), jnp.where(right, a, 0.0)]
        dv = v.shape[1]
        out.append((uw[:, :dv], uw[:, dv:], a_chunks, qss[i] * egs[i], k * jnp.exp(gtot - gc)))
    return out


def _gated_norm(o, z, gain):
    return (_rms(o, gain) * _silu(z)).astype(BF16)


def _conv_act_head(x_ref, w_ref, carry_ref, h, l2norm):
    sl = slice(h * HEAD, (h + 1) * HEAD)
    u, w, carry = x_ref[:, sl], w_ref[:, sl], carry_ref[:, sl]
    y = (w[3:4] * u + w[2:3] * _shift_rows(carry, u, 1) + w[1:2] * _shift_rows(carry, u, 2)
         + w[0:1] * _shift_rows(carry, u, 3))
    a = _silu(y)
    if l2norm:
        a = a * lax.rsqrt(jnp.sum(a * a, axis=-1, keepdims=True) + NORM_EPS)
    return a


def _gdn_prompt_kernel(chunk, hps, q_ref, k_ref, v_ref, z_ref, wq_ref, wk_ref, wv_ref, beta_ref, gc_ref, gtot_ref,
                       gct_ref, gn_ref, o_ref, s_ref, s_scr, cq_scr, ck_scr, cv_scr):
    @pl.when(pl.program_id(2) == 0)
    def _():
        s_scr[...] = jnp.zeros_like(s_scr)
        cq_scr[...] = jnp.zeros_like(cq_scr)
        ck_scr[...] = jnp.zeros_like(ck_scr)
        cv_scr[...] = jnp.zeros_like(cv_scr)
    r = q_ref.shape[0]
    h0 = pl.program_id(1) * hps
    paired = 2 * chunk == HEAD and r % HEAD == 0
    qs = [_conv_act_head(q_ref, wq_ref, cq_scr, kh, True) for kh in range(hps // 2)]
    ks = [_conv_act_head(k_ref, wk_ref, ck_scr, kh, True) for kh in range(hps // 2)]
    vs = [_conv_act_head(v_ref, wv_ref, cv_scr, hl, False) for hl in range(hps)]
    cq_scr[...] = q_ref[r - 8:r, :]
    ck_scr[...] = k_ref[r - 8:r, :]
    cv_scr[...] = v_ref[r - 8:r, :]
    heads = []
    for hl in range(hps):
        heads.append((qs[hl // 2], ks[hl // 2], vs[hl], beta_ref[0][:, hl:hl + 1],
                      gc_ref[0][:, hl:hl + 1], gtot_ref[0][:, hl:hl + 1],
                      gct_ref[0, pl.ds(h0 + hl, 1), :]))
    if paired:
        terms = _paired_chunk_terms(chunk, heads)
    else:
        terms = [_chunk_terms(chunk, *head) for head in heads]
    terms = [t + (jnp.exp(head[5]),) for t, head in zip(terms, heads)]
    states = [s_scr[hl] for hl in range(hps)]
    outs = [[] for _ in range(hps)]
    for c in range(r // chunk):
        lo, hi = c * chunk, (c + 1) * chunk
        for hl in range(hps):
            u, w, a, qd, kd, egt = terms[hl]
            s = states[hl]
            pq = _dot(jnp.concatenate([w[lo:hi], qd[lo:hi]], axis=0), s)
            v_new = u[lo:hi] - pq[:chunk]
            if paired:
                intra = _dot(a[c], jnp.concatenate([v_new, v_new], axis=0))
            else:
                intra = _dot(a[lo:hi, lo:hi], v_new)
            outs[hl].append(pq[chunk:] + intra)
            states[hl] = s * egt[lo:lo + 1] + _dot_tn(kd[lo:hi], v_new)
    for hl in range(hps):
        s_scr[hl] = states[hl]
        o = jnp.concatenate(outs[hl], axis=0)
        o_ref[:, hl * HEAD:(hl + 1) * HEAD] = _gated_norm(o, z_ref[:, hl * HEAD:(hl + 1) * HEAD], gn_ref[...])
    s_ref[0] = s_scr[...]


def _gdn_prompt(proj, conv_w, beta, gc, gtot, gct, g_onorm, n_seq, t_len, d, chunk, r=256, hps=8):
    nvh = 2 * d // HEAD
    ns = t_len // r
    kc = hps // 2 * HEAD
    vc = hps * HEAD
    rows = lambda b, g, s: b * ns + s
    return pl.pallas_call(
        functools.partial(_gdn_prompt_kernel, chunk, hps),
        grid=(n_seq, nvh // hps, ns),
        in_specs=[pl.BlockSpec((r, kc), lambda b, g, s: (rows(b, g, s), g)),
                  pl.BlockSpec((r, kc), lambda b, g, s: (rows(b, g, s), d // kc + g)),
                  pl.BlockSpec((r, vc), lambda b, g, s: (rows(b, g, s), 2 * d // vc + g)),
                  pl.BlockSpec((r, vc), lambda b, g, s: (rows(b, g, s), 4 * d // vc + g)),
                  pl.BlockSpec((4, kc), lambda b, g, s: (0, g)),
                  pl.BlockSpec((4, kc), lambda b, g, s: (0, d // kc + g)),
                  pl.BlockSpec((4, vc), lambda b, g, s: (0, 2 * d // vc + g)),
                  pl.BlockSpec((1, r, hps), lambda b, g, s: (g, rows(b, g, s), 0)),
                  pl.BlockSpec((1, r, hps), lambda b, g, s: (g, rows(b, g, s), 0)),
                  pl.BlockSpec((1, r, hps), lambda b, g, s: (g, rows(b, g, s), 0)),
                  pl.BlockSpec((1, nvh, r), lambda b, g, s: (rows(b, g, s), 0, 0)),
                  pl.BlockSpec((1, HEAD), lambda b, g, s: (0, 0))],
        out_specs=[pl.BlockSpec((r, vc), lambda b, g, s: (rows(b, g, s), g)),
                   pl.BlockSpec((1, hps, HEAD, HEAD), lambda b, g, s: (b, g, 0, 0))],
        out_shape=[jax.ShapeDtypeStruct((n_seq * t_len, 2 * d), BF16),
                   jax.ShapeDtypeStruct((n_seq, nvh, HEAD, HEAD), F32)],
        scratch_shapes=[pltpu.VMEM((hps, HEAD, HEAD), F32), pltpu.VMEM((8, kc), F32), pltpu.VMEM((8, kc), F32),
                        pltpu.VMEM((8, vc), F32)],
        compiler_params=_cparams("parallel", "parallel", "arbitrary"),
        name="gdn_prompt",
    )(proj, proj, proj, proj, conv_w, conv_w, conv_w, beta, gc, gtot, gct, g_onorm.reshape(1, HEAD))


def _gdn_sample_kernel(chunk, hps, q_ref, k_ref, v_ref, z_ref, beta_ref, gc_ref, gtot_ref, gct_ref, gn_ref, s0_ref,
                       o_ref, s_ref, wq_scr, kd_scr, eg_scr, pq_scr):
    r = q_ref.shape[0]
    h0 = pl.program_id(0) * hps
    per_row = 8 // chunk
    terms = []
    for hl in range(hps):
        kh = hl // 2
        q = q_ref[:, kh * HEAD:(kh + 1) * HEAD]
        k = k_ref[:, kh * HEAD:(kh + 1) * HEAD]
        v = v_ref[:, hl * HEAD:(hl + 1) * HEAD]
        beta = beta_ref[0][:, hl:hl + 1]
        gc = gc_ref[0][:, hl:hl + 1]
        gtot = gtot_ref[0][:, hl:hl + 1]
        gc_row = gct_ref[0, pl.ds(h0 + hl, 1), :]
        u, w, a, qd, kd = _chunk_terms(chunk, q, k, v, beta, gc, gtot, gc_row)
        wq_scr[hl, :, 0:8, :] = w.reshape(r // 8, 8, HEAD)
        wq_scr[hl, :, 8:16, :] = qd.reshape(r // 8, 8, HEAD)
        kd_scr[hl] = kd
        eg_scr[hl] = jnp.broadcast_to(jnp.exp(gtot), (r, HEAD))
        terms.append((u, a))

    row16 = lax.broadcasted_iota(jnp.int32, (16, HEAD), 0)
    row8 = lax.broadcasted_iota(jnp.int32, (8, HEAD), 0)

    def pair_pred(p, carry):
        for hl in range(hps):
            lhs = wq_scr[hl, p]
            acc = jnp.zeros((16, HEAD), F32)
            for e in range(per_row):
                pq = _dot(lhs, s0_ref[p * per_row + e, hl])
                acc = jnp.where(_blk(row16 & 7, chunk) == e, pq, acc)
            pq_scr[hl, p] = acc
        return carry
    lax.fori_loop(0, r // 8, pair_pred, 0)

    v_news = []
    for hl in range(hps):
        u, a = terms[hl]
        pq = pq_scr[hl]
        v_new = u - pq[:, 0:8, :].reshape(r, HEAD)
        o = pq[:, 8:16, :].reshape(r, HEAD) + _dot(a, v_new)
        o_ref[:, hl * HEAD:(hl + 1) * HEAD] = _gated_norm(o, z_ref[:, hl * HEAD:(hl + 1) * HEAD], gn_ref[...])
        pq_scr[hl, :, 0:8, :] = v_new.reshape(r // 8, 8, HEAD)

    def pair_update(p, carry):
        for hl in range(hps):
            v_new = pq_scr[hl, p, 0:8, :]
            kd = kd_scr[hl, pl.ds(pl.multiple_of(p * 8, 8), 8), :]
            for e in range(per_row):
                b = p * per_row + e
                decay = eg_scr[hl, pl.ds(p * 8 + e * chunk, 1), :]
                kd_e = jnp.where(_blk(row8, chunk) == e, kd, 0.0)
                s_ref[b, hl] = s0_ref[b, hl] * decay + _dot_tn(kd_e, v_new)
        return carry
    lax.fori_loop(0, r // 8, pair_update, 0)


def _gdn_sample(qkv, z, beta, gc, gtot, gct, g_onorm, state, d, chunk, r=128, hps=4):
    n = qkv.shape[0]
    nvh = 2 * d // HEAD
    kc = hps // 2 * HEAD
    vc = hps * HEAD
    nb = r // chunk
    return pl.pallas_call(
        functools.partial(_gdn_sample_kernel, chunk, hps),
        grid=(nvh // hps, n // r),
        in_specs=[pl.BlockSpec((r, kc), lambda g, s: (s, g)),
                  pl.BlockSpec((r, kc), lambda g, s: (s, d // kc + g)),
                  pl.BlockSpec((r, vc), lambda g, s: (s, 2 * d // vc + g)),
                  pl.BlockSpec((r, vc), lambda g, s: (s, g)),
                  pl.BlockSpec((1, r, hps), lambda g, s: (g, s, 0)),
                  pl.BlockSpec((1, r, hps), lambda g, s: (g, s, 0)),
                  pl.BlockSpec((1, r, hps), lambda g, s: (g, s, 0)),
                  pl.BlockSpec((1, nvh, r), lambda g, s: (s, 0, 0)),
                  pl.BlockSpec((1, HEAD), lambda g, s: (0, 0)),
                  pl.BlockSpec((nb, hps, HEAD, HEAD), lambda g, s: (s, g, 0, 0))],
        out_specs=[pl.BlockSpec((r, vc), lambda g, s: (s, g)),
                   pl.BlockSpec((nb, hps, HEAD, HEAD), lambda g, s: (s, g, 0, 0))],
        out_shape=[jax.ShapeDtypeStruct((n, 2 * d), BF16),
                   jax.ShapeDtypeStruct(state.shape, F32)],
        scratch_shapes=[pltpu.VMEM((hps, r // 8, 16, HEAD), F32), pltpu.VMEM((hps, r, HEAD), F32),
                        pltpu.VMEM((hps, r, HEAD), F32), pltpu.VMEM((hps, r // 8, 16, HEAD), F32)],
        compiler_params=_cparams("parallel", "parallel"),
        name="gdn_sample",
    )(qkv, qkv, qkv, z, beta, gc, gtot, gct, g_onorm.reshape(1, HEAD), state)


def kernel(x_prompt, x_sample, c_prompt, c_sample, state_gdn, state_gdn_conv, state_sconv, w_ada, b_ada,
           g_norm_mix, g_norm_ffn, g_norm_out, gdn_w_in, gdn_conv_w, gdn_a_log, gdn_dt_bias, gdn_g_onorm,
           gdn_w_out, sc_w_in, sc_conv_w, sc_w_out, ffn_w_up, ffn_w_down, moe_w_router, moe_b_router,
           moe_w_up, moe_w_down):
    nbp, t_len, d = x_prompt.shape
    nbs, steps, _ = x_sample.shape
    n_p, n_s = nbp * t_len, nbs * steps
    nvh = 2 * d // HEAD
    chunk_p = min(64, t_len)
    tm = nbs
    assert t_len % tm == 0 and n_p % n_s == 0 and 8 % steps == 0

    n = n_p + n_s
    tm_mm = _tile(n, 1088)
    x_in = (x_prompt.reshape(n_p, d), x_sample.reshape(nbs, steps * d))
    pad = (-(nbp + nbs)) % 8
    c_all = jnp.concatenate([c_prompt, c_sample, jnp.zeros((pad, d), F32)], axis=0)
    m = _ada(c_all, w_ada, b_ada)
    mods = [_Mod(m[i, :nbp].reshape(nbp, 1, 6 * d), m[i, nbp:nbp + nbs], d, tm, n_p // tm, t_len // tm)
            for i in range(2)]
    SH1, SC1, GA1, SH2, SC2, GA2 = range(6)

    h = _prep(x_in, g_norm_mix[0], mods[0], SC1, SH1, n, tm)
    w_in_t = jnp.swapaxes(gdn_w_in[0], 0, 1)
    proj = _matmul(h, w_in_t, 6 * d, _tile(6 * d, 1024), tm_mm, name="gdn_in_proj", w_is_transposed=True)
    ba = _matmul(h, w_in_t, 2 * nvh, 2 * nvh, tm_mm, name="gdn_in_proj_ba", w_is_transposed=True,
                 col_block0=6 * d // (2 * nvh))
    conv_w = gdn_conv_w[0]
    st_gconv = jnp.transpose(state_gdn_conv[0], (1, 0, 2))
    qkv_s = _gconv_sample(proj, st_gconv, conv_w, n_p // n_s, nbs, n_s, d)
    to_seq_major = lambda a: a.reshape(steps, nbs, a.shape[-1]).transpose(1, 0, 2).reshape(n_s, a.shape[-1])
    qkv_s = to_seq_major(qkv_s)
    z_s = to_seq_major(proj[n_p:, 4 * d:6 * d])
    ba_s = to_seq_major(ba[n_p:])
    hps_p, hps_s = min(8, nvh), min(4, nvh)
    beta_p, gc_p, gtot_p, gct_p = _gates(ba[:n_p], gdn_a_log[0], gdn_dt_bias[0], chunk_p, 256, hps_p)
    beta_s, gc_s, gtot_s, gct_s = _gates(ba_s, gdn_a_log[0], gdn_dt_bias[0], steps, 128, hps_s)
    o_p, gdn_p = _gdn_prompt(proj, conv_w, beta_p, gc_p, gtot_p, gct_p, gdn_g_onorm[0], nbp, t_len, d, chunk_p,
                             hps=hps_p)
    o_s, gdn_s = _gdn_sample(qkv_s, z_s, beta_s, gc_s, gtot_s, gct_s, gdn_g_onorm[0], state_gdn[0], d, steps,
                             hps=hps_s)
    tm_p = _tile(n_p, 1024)
    y = (_matmul(o_p, gdn_w_out[0], d, _tile(d, 512), tm_p, name="gdn_out_proj"),
         _matmul(o_s, gdn_w_out[0], d, _tile(d, 512), n_s, name="gdn_out_proj_sample").reshape(nbs, steps * d))
    gconv_p = jnp.stack([proj[(b + 1) * t_len - 3:(b + 1) * t_len, :4 * d] for b in range(nbp)])
    gconv_s = proj[n_p + nbs:, :4 * d].reshape(3, nbs, 4 * d).transpose(1, 0, 2)

    x, h = _resid_prep(x_in, y, g_norm_ffn[0], mods[0], GA1, mods[0], SC2, SH2, n, tm)
    one_expert = jnp.zeros((n // tm_mm,), jnp.int32)
    all_tiles = jnp.full((1,), n // tm_mm, jnp.int32)
    act = _swiglu_up(h, ffn_w_up, one_expert, all_tiles, tm_mm, _tile(ffn_w_down.shape[1], 512), "ffn_swiglu_up")
    f = _matmul(act, ffn_w_down[0], d, _tile(d, 512), name="ffn_down")

    x, h = _resid_prep(x, f, g_norm_mix[1], mods[0], GA2, mods[1], SC1, SH1, n, tm)
    s_in = _matmul(h, sc_w_in[0], 3 * d, _tile(3 * d, 1024), tm_mm, name="sconv_in_proj")
    mix_p, sconv_tail = _sconv_prompt(s_in, sc_conv_w[0], nbp, t_len, d)
    mix_s, u_s = _sconv_sample(s_in, state_sconv[0].reshape(nbs, 2 * d), sc_conv_w[0], n_p // n_s, nbs, n_s, d)
    y_s = _matmul(mix_s, sc_w_out[0], d, _tile(d, 1024), n_s, name="sconv_out_proj_sample")
    y = (_matmul(mix_p, sc_w_out[0], d, _tile(d, 1024), tm_p, name="sconv_out_proj"),
         y_s.reshape(steps, nbs, d).transpose(1, 0, 2).reshape(nbs, steps * d))
    sconv_p = sconv_tail[:, 6:8]
    sconv_s = u_s[n_s - 2 * nbs:].reshape(2, nbs, d).transpose(1, 0, 2)

    x, h, info = _resid_route(x, y, g_norm_ffn[1], mods[1], GA1, mods[1], SC2, SH2,
                              moe_w_router[0], moe_b_router[0], tm)
    _, n_exp, f_moe, _ = moe_w_down.shape
    tmg = 384
    pos, src, te, nv = _route_tables(info, n_exp, tmg)
    tg = _tile(tmg, 256)
    h_sorted = _gather_rows(h, src, nv * (tmg // tg), tg)
    act = _swiglu_up(h_sorted, moe_w_up[0], te, nv, tmg, _tile(f_moe, 1024), "moe_swiglu_up")
    y_sorted = _grouped_down(act, moe_w_down[0], te, nv, tmg, _tile(d, 512), "moe_down")
    out_p, out_s = _final(x, y_sorted, pos, info, g_norm_out, mods[1], GA2, n_p, nbs, tm)

    y_prompt = out_p.reshape(nbp, t_len, d)
    y_sample = out_s.reshape(nbs, steps, d)
    return (y_prompt, y_sample, gdn_p[None], gconv_p[None], sconv_p[None], gdn_s[None], gconv_s[None], sconv_s[None])
```

```python
import functools

import jax
import jax.numpy as jnp
from jax import lax
from jax.experimental import pallas as pl
from jax.experimental.pallas import tpu as pltpu

F32 = jnp.float32
BF16 = jnp.bfloat16
NORM_EPS = 1e-6
HEAD = 128
V7X_VMEM_LIMIT = 56 * 1024 * 1024
HIGHEST = lax.Precision.HIGHEST


def _cparams(*sem):
    return pltpu.CompilerParams(dimension_semantics=sem, vmem_limit_bytes=V7X_VMEM_LIMIT)


def _tile(n, pref):
    return pref if n % pref == 0 else n


def _silu(x):
    return 0.5 * x * (1.0 + jnp.tanh(0.5 * x))


def _softplus(x):
    return jnp.maximum(x, 0.0) + jnp.log1p(jnp.exp(-jnp.abs(x)))


def _blk(idx, size):
    assert size & (size - 1) == 0
    return lax.shift_right_logical(idx, size.bit_length() - 1)


def _dot(a, b):
    return jnp.dot(a.astype(BF16), b.astype(BF16), preferred_element_type=F32)


def _dot_nt(a, b):
    return lax.dot_general(a.astype(BF16), b.astype(BF16), (((1,), (1,)), ((), ())), preferred_element_type=F32)


def _dot_tn(a, b):
    return lax.dot_general(a.astype(BF16), b.astype(BF16), (((0,), (0,)), ((), ())), preferred_element_type=F32)


def _ada_kernel(c_ref, w_ref, b_ref, o_ref):
    c = c_ref[...]
    o_ref[0] = _dot(_silu(c), w_ref[0]) + b_ref[0]


def _ada(c_all, w_ada, b_ada):
    nl, d, n6 = w_ada.shape
    r = c_all.shape[0]
    tn = _tile(n6, 1024)
    return pl.pallas_call(
        _ada_kernel,
        grid=(nl, n6 // tn),
        in_specs=[pl.BlockSpec((r, d), lambda l, j: (0, 0)),
                  pl.BlockSpec((1, d, tn), lambda l, j: (l, 0, j)),
                  pl.BlockSpec((1, 1, tn), lambda l, j: (l, 0, j))],
        out_specs=pl.BlockSpec((1, r, tn), lambda l, j: (l, 0, j)),
        out_shape=jax.ShapeDtypeStruct((nl, r, n6), F32),
        compiler_params=_cparams("parallel", "parallel"),
        name="ada_mod",
    )(c_all, w_ada, b_ada.reshape(nl, 1, n6))


class _Mod:
    def __init__(self, mod_p, mod_s, d, tm, n_ptiles, tiles_per_seq):
        self.arrays = (mod_p, mod_s)
        self.n_ptiles = n_ptiles
        nb = mod_p.shape[0]

        def specs(chunk):
            return [pl.BlockSpec((1, 1, d), lambda i, *_: (jnp.minimum(i // tiles_per_seq, nb - 1), 0, chunk)),
                    pl.BlockSpec((tm, d), lambda i, *_: (0, chunk))]
        self.specs = specs
        self.row_specs = [pl.BlockSpec((tm, d), lambda i, *_: (jnp.minimum(i, n_ptiles - 1), 0)),
                          pl.BlockSpec((tm, d), lambda i, *_: (0, jnp.maximum(i - n_ptiles, 0)))]

    def is_prompt(self):
        return pl.program_id(0) < self.n_ptiles

    def pick(self, p_ref, s_ref):
        return jnp.where(self.is_prompt(), p_ref[0], s_ref[...])

    def pick_rows(self, p_ref, s_ref):
        return jnp.where(self.is_prompt(), p_ref[...], s_ref[...])


def _rms(x, gain):
    ms = jnp.mean(x * x, axis=-1, keepdims=True)
    return x * lax.rsqrt(ms + NORM_EPS) * gain


def _prep_kernel(mod, xp_ref, xs_ref, g_ref, scp, scs, shp, shs, h_ref):
    y = _rms(mod.pick_rows(xp_ref, xs_ref), g_ref[...])
    h_ref[...] = (y * (1.0 + mod.pick(scp, scs)) + mod.pick(shp, shs)).astype(h_ref.dtype)


def _prep(x_pair, gain, mod, sc_chunk, sh_chunk, n, tm):
    d = gain.shape[0]
    return pl.pallas_call(
        functools.partial(_prep_kernel, mod),
        grid=(n // tm,),
        in_specs=mod.row_specs + [pl.BlockSpec((1, d), lambda i: (0, 0))] + mod.specs(sc_chunk) + mod.specs(sh_chunk),
        out_specs=pl.BlockSpec((tm, d), lambda i: (i, 0)),
        out_shape=jax.ShapeDtypeStruct((n, d), BF16),
        compiler_params=_cparams("parallel"),
        name="norm_mod",
    )(*x_pair, gain.reshape(1, d), *mod.arrays, *mod.arrays)


def _row_inputs(mod, a, tm, d):
    if isinstance(a, tuple):
        return mod.row_specs, list(a)
    return [pl.BlockSpec((tm, d), lambda i, *_: (i, 0))], [a]


def _read_rows(mod, refs):
    return mod.pick_rows(*refs) if len(refs) == 2 else refs[0][...]


def _resid_prep_kernel(mod, nx, ny, *refs):
    x_refs, y_refs = refs[:nx], refs[nx:nx + ny]
    g_ref, gap, gas, scp, scs, shp, shs, xo_ref, h_ref = refs[nx + ny:]
    x = _read_rows(mod, x_refs) + mod.pick(gap, gas) * _read_rows(mod, y_refs)
    xo_ref[...] = x
    y = _rms(x, g_ref[...])
    h_ref[...] = (y * (1.0 + mod.pick(scp, scs)) + mod.pick(shp, shs)).astype(h_ref.dtype)


def _resid_prep(x, y, gain, mod_gate, ga_chunk, mod, sc_chunk, sh_chunk, n, tm):
    d = gain.shape[0]
    row = pl.BlockSpec((tm, d), lambda i: (i, 0))
    x_specs, x_ops = _row_inputs(mod, x, tm, d)
    y_specs, y_ops = _row_inputs(mod, y, tm, d)
    return pl.pallas_call(
        functools.partial(_resid_prep_kernel, mod, len(x_ops), len(y_ops)),
        grid=(n // tm,),
        in_specs=x_specs + y_specs + [pl.BlockSpec((1, d), lambda i: (0, 0))]
        + mod_gate.specs(ga_chunk) + mod.specs(sc_chunk) + mod.specs(sh_chunk),
        out_specs=[row, row],
        out_shape=[jax.ShapeDtypeStruct((n, d), F32), jax.ShapeDtypeStruct((n, d), BF16)],
        compiler_params=_cparams("parallel"),
        name="resid_norm_mod",
    )(*x_ops, *y_ops, gain.reshape(1, d), *mod_gate.arrays, *mod.arrays, *mod.arrays)


def _resid_route_kernel(mod, n_exp, ny, x_ref, *refs):
    y_refs = refs[:ny]
    g_ref, gap, gas, scp, scs, shp, shs, wr_ref, br_ref, xo_ref, h_ref, info_ref = refs[ny:]
    x = x_ref[...] + mod.pick(gap, gas) * _read_rows(mod, y_refs)
    xo_ref[...] = x
    h = _rms(x, g_ref[...]) * (1.0 + mod.pick(scp, scs)) + mod.pick(shp, shs)
    h_ref[...] = h
    logits = jnp.dot(h, wr_ref[...], precision=HIGHEST, preferred_element_type=F32) + br_ref[...]
    lane = lax.broadcasted_iota(jnp.int32, logits.shape, 1).astype(F32)
    m1 = jnp.max(logits, axis=-1, keepdims=True)
    i1 = jnp.min(jnp.where(logits == m1, lane, float(n_exp)), axis=-1, keepdims=True)
    rest = jnp.where(lane == i1, -jnp.inf, logits)
    m2 = jnp.max(rest, axis=-1, keepdims=True)
    i2 = jnp.min(jnp.where(rest == m2, lane, float(n_exp)), axis=-1, keepdims=True)
    e2 = jnp.exp(m2 - m1)
    g1 = 1.0 / (1.0 + e2)
    g2 = e2 / (1.0 + e2)
    info_ref[...] = (jnp.where(lane == 0.0, i1, 0.0) + jnp.where(lane == 1.0, i2, 0.0)
                     + jnp.where(lane == 2.0, g1, 0.0) + jnp.where(lane == 3.0, g2, 0.0))


def _resid_route(x, y, gain, mod_gate, ga_chunk, mod, sc_chunk, sh_chunk, w_router, b_router, tm):
    n, d = x.shape
    n_exp = w_router.shape[1]
    row = pl.BlockSpec((tm, d), lambda i: (i, 0))
    y_specs, y_ops = _row_inputs(mod, y, tm, d)
    return pl.pallas_call(
        functools.partial(_resid_route_kernel, mod, n_exp, len(y_ops)),
        grid=(n // tm,),
        in_specs=[row] + y_specs + [pl.BlockSpec((1, d), lambda i: (0, 0))]
        + mod_gate.specs(ga_chunk) + mod.specs(sc_chunk) + mod.specs(sh_chunk)
        + [pl.BlockSpec((d, n_exp), lambda i: (0, 0)), pl.BlockSpec((1, n_exp), lambda i: (0, 0))],
        out_specs=[row, row, pl.BlockSpec((tm, n_exp), lambda i: (i, 0))],
        out_shape=[jax.ShapeDtypeStruct((n, d), F32), jax.ShapeDtypeStruct((n, d), F32),
                   jax.ShapeDtypeStruct((n, n_exp), F32)],
        compiler_params=_cparams("parallel"),
        name="resid_norm_route",
    )(x, *y_ops, gain.reshape(1, d), *mod_gate.arrays, *mod.arrays, *mod.arrays, w_router,
      b_router.reshape(1, n_exp))


def _row_copy(src_hbm, row, dst, r, sem):
    return pltpu.make_async_copy(src_hbm.at[pl.ds(row, 1)], dst.at[pl.ds(r, 1)], sem)


def _final_kernel(mod, pos_ref, x_ref, y_hbm, info_ref, g_ref, gap, gas, op_ref, os_ref, buf, sem):
    tm = x_ref.shape[0]
    i = pl.program_id(0)
    n_tiles = pl.num_programs(0)

    def issue(tile, slot):
        def body(r, carry):
            for k in range(2):
                _row_copy(y_hbm, pos_ref[tile * (2 * tm) + 2 * r + k], buf.at[slot, k], r, sem.at[slot, k]).start()
            return carry
        lax.fori_loop(0, tm, body, 0, unroll=4)

    @pl.when(i == 0)
    def _():
        issue(0, 0)

    for slot in range(2):
        @pl.when((i + 1 < n_tiles) & ((i + 1) % 2 == slot))
        def _():
            issue(i + 1, slot)

    for slot in range(2):
        @pl.when(i % 2 == slot)
        def _():
            for k in range(2):
                pltpu.make_async_copy(y_hbm.at[pl.ds(0, tm)], buf.at[slot, k], sem.at[slot, k]).wait()
            info = info_ref[...]
            f = info[:, 2:3] * buf[slot, 0] + info[:, 3:4] * buf[slot, 1]
            x = x_ref[...] + mod.pick(gap, gas) * f
            out = _rms(x, g_ref[...])

            @pl.when(mod.is_prompt())
            def _():
                op_ref[...] = out

            @pl.when(jnp.logical_not(mod.is_prompt()))
            def _():
                os_ref[...] = out


def _final(x, y_sorted, pos, info, gain, mod_gate, ga_chunk, n_p, nbs, tm):
    n, d = x.shape
    row = pl.BlockSpec((tm, d), lambda i, *_: (i, 0))
    return pl.pallas_call(
        functools.partial(_final_kernel, mod_gate),
        grid_spec=pltpu.PrefetchScalarGridSpec(
            num_scalar_prefetch=1,
            grid=(n // tm,),
            in_specs=[row, pl.BlockSpec(memory_space=pl.ANY),
                      pl.BlockSpec((tm, info.shape[1]), lambda i, *_: (i, 0)),
                      pl.BlockSpec((1, d), lambda i, *_: (0, 0))] + mod_gate.specs(ga_chunk),
            out_specs=mod_gate.row_specs,
            scratch_shapes=[pltpu.VMEM((2, 2, tm, d), F32), pltpu.SemaphoreType.DMA((2, 2))]),
        out_shape=[jax.ShapeDtypeStruct((n_p, d), F32), jax.ShapeDtypeStruct((nbs, (n - n_p) // nbs * d), F32)],
        compiler_params=_cparams("arbitrary"),
        name="moe_combine_final_norm",
    )(pos, x, y_sorted, info, gain.reshape(1, d), *mod_gate.arrays)


def _route_tables(info, n_exp, tmg):
    n = info.shape[0]
    e = info[:, :2].astype(jnp.int32)
    onehot = (e[:, :, None] == jnp.arange(n_exp, dtype=jnp.int32)[None, None, :]).astype(jnp.int32).sum(axis=1)
    counts = onehot.sum(axis=0)
    padded = (counts + tmg - 1) // tmg * tmg
    ends = jnp.cumsum(padded)
    starts = ends - padded
    rank = jnp.cumsum(onehot, axis=0) - onehot
    pos = (starts[e] + jnp.take_along_axis(rank, e, axis=1)).reshape(-1).astype(jnp.int32)
    n_tiles = -(-(2 * n) // tmg) + n_exp
    src = jnp.zeros((n_tiles * tmg,), jnp.int32).at[pos].set(jnp.repeat(jnp.arange(n, dtype=jnp.int32), 2))
    n_valid = (ends[-1] // tmg).astype(jnp.int32)
    tiles = jnp.arange(n_tiles, dtype=jnp.int32)
    te = jnp.minimum((tiles[:, None] * tmg >= ends[None, :]).astype(jnp.int32).sum(axis=1), n_exp - 1)
    te = jnp.where(tiles < n_valid, te, te[jnp.maximum(n_valid - 1, 0)])
    return pos, src, te, n_valid.reshape(1), (starts // tmg).astype(jnp.int32), (padded // tmg).astype(jnp.int32)


def _expert_rows_loop(cs_ref, cc_ref, a_hbm, o_hbm, abuf, obuf, asem, osem, col0, compute):
    e = pl.program_id(0)
    c0, nc = cs_ref[e], cc_ref[e]
    ck, tn = abuf.shape[1], obuf.shape[2]

    def rows(c):
        return pl.ds(pl.multiple_of((c0 + c) * ck, ck), ck)

    def a_copy(c, slot):
        return pltpu.make_async_copy(a_hbm.at[rows(c)], abuf.at[slot], asem.at[slot])

    def o_copy(c, slot):
        return pltpu.make_async_copy(obuf.at[slot], o_hbm.at[rows(c), pl.ds(col0, tn)], osem.at[slot])

    @pl.when(nc > 0)
    def _():
        a_copy(0, 0).start()

    def body(c, carry):
        slot = c % 2

        @pl.when(c + 1 < nc)
        def _():
            a_copy(c + 1, 1 - slot).start()
        a_copy(c, slot).wait()

        @pl.when(c >= 2)
        def _():
            o_copy(c - 2, slot).wait()
        obuf[slot] = compute(abuf[slot]).astype(obuf.dtype)
        o_copy(c, slot).start()
        return carry
    lax.fori_loop(0, nc, body, 0)

    @pl.when(nc >= 2)
    def _():
        o_copy(nc - 2, nc % 2).wait()

    @pl.when(nc >= 1)
    def _():
        o_copy(nc - 1, (nc - 1) % 2).wait()

    @pl.when(e == pl.num_programs(0) - 1)
    def _():
        total = o_hbm.shape[0] // ck
        obuf[0] = jnp.zeros(obuf.shape[1:], obuf.dtype)

        def fill(c, carry):
            cp = pltpu.make_async_copy(obuf.at[0], o_hbm.at[pl.ds(pl.multiple_of(c * ck, ck), ck), pl.ds(col0, tn)],
                                       osem.at[0])
            cp.start()
            cp.wait()
            return carry
        lax.fori_loop(c0 + nc, total, fill, 0)


def _moe_up_kernel(cs_ref, cc_ref, a_hbm, wg_ref, wu_ref, o_hbm, wgb_ref, wub_ref, abuf, obuf, asem, osem):
    wgb_ref[...] = wg_ref[0].astype(BF16)
    wub_ref[...] = wu_ref[0].astype(BF16)
    tn = obuf.shape[2]

    def compute(a):
        gate = jnp.dot(a, wgb_ref[...], preferred_element_type=F32)
        up = jnp.dot(a, wub_ref[...], preferred_element_type=F32)
        return _silu(gate) * up
    _expert_rows_loop(cs_ref, cc_ref, a_hbm, o_hbm, abuf, obuf, asem, osem,
                      pl.multiple_of(pl.program_id(1) * tn, tn), compute)


def _moe_down_kernel(cs_ref, cc_ref, a_hbm, w_ref, o_hbm, wb_ref, abuf, obuf, asem, osem):
    wb_ref[...] = w_ref[0].astype(BF16)
    tn = obuf.shape[2]
    _expert_rows_loop(cs_ref, cc_ref, a_hbm, o_hbm, abuf, obuf, asem, osem,
                      pl.multiple_of(pl.program_id(1) * tn, tn),
                      lambda a: jnp.dot(a, wb_ref[...], preferred_element_type=F32))


def _moe_matmul(kernel_fn, a, weights, w_specs, wb_shapes, chunk_start, chunk_count, ck, tn, n_out, out_dtype, name):
    m, k = a.shape
    n_exp = chunk_start.shape[0]
    return pl.pallas_call(
        kernel_fn,
        grid_spec=pltpu.PrefetchScalarGridSpec(
            num_scalar_prefetch=2,
            grid=(n_exp, n_out // tn),
            in_specs=[pl.BlockSpec(memory_space=pl.ANY)] + w_specs,
            out_specs=pl.BlockSpec(memory_space=pl.ANY),
            scratch_shapes=[pltpu.VMEM(s, BF16) for s in wb_shapes]
            + [pltpu.VMEM((2, ck, k), a.dtype), pltpu.VMEM((2, ck, tn), out_dtype),
               pltpu.SemaphoreType.DMA((2,)), pltpu.SemaphoreType.DMA((2,))]),
        out_shape=jax.ShapeDtypeStruct((m, n_out), out_dtype),
        compiler_params=_cparams("arbitrary", "arbitrary"),
        name=name,
    )(chunk_start, chunk_count, a, *weights)


def _moe_up(a, w_up, chunk_start, chunk_count, ck, tn):
    k = a.shape[1]
    f = w_up.shape[2] // 2
    nj = f // tn
    w_specs = [pl.BlockSpec((1, k, tn), lambda e, j, cs, cc: (e, 0, j)),
               pl.BlockSpec((1, k, tn), lambda e, j, cs, cc: (e, 0, nj + j))]
    return _moe_matmul(_moe_up_kernel, a, (w_up, w_up), w_specs, [(k, tn), (k, tn)], chunk_start, chunk_count,
                       ck, tn, f, BF16, "moe_swiglu_up")


def _moe_down(a, w_down, chunk_start, chunk_count, ck, tn):
    f = a.shape[1]
    d = w_down.shape[2]
    w_specs = [pl.BlockSpec((1, f, tn), lambda e, j, cs, cc: (e, 0, j))]
    return _moe_matmul(_moe_down_kernel, a, (w_down,), w_specs, [(f, tn)], chunk_start, chunk_count,
                       ck, tn, d, F32, "moe_down")


def _gather_kernel(src_ref, nv_ref, h_hbm, o_ref, buf, sem):
    i = pl.program_id(0)
    tmg = buf.shape[1]
    nv = nv_ref[0]

    def issue(tile, slot):
        def body(r, carry):
            _row_copy(h_hbm, src_ref[tile * tmg + r], buf.at[slot], r, sem.at[slot]).start()
            return carry
        lax.fori_loop(0, tmg, body, 0, unroll=8)

    @pl.when((i == 0) & (nv > 0))
    def _():
        issue(0, 0)

    for slot in range(2):
        @pl.when((i + 1 < nv) & ((i + 1) % 2 == slot))
        def _():
            issue(i + 1, slot)

    for slot in range(2):
        @pl.when((i < nv) & (i % 2 == slot))
        def _():
            pltpu.make_async_copy(h_hbm.at[pl.ds(0, tmg)], buf.at[slot], sem.at[slot]).wait()
            o_ref[...] = buf[slot].astype(o_ref.dtype)

    @pl.when(i >= nv)
    def _():
        o_ref[...] = jnp.zeros_like(o_ref)


def _gather_rows(h, src, n_valid, tmg):
    d = h.shape[1]
    n_tiles = src.shape[0] // tmg
    return pl.pallas_call(
        _gather_kernel,
        grid_spec=pltpu.PrefetchScalarGridSpec(
            num_scalar_prefetch=2,
            grid=(n_tiles,),
            in_specs=[pl.BlockSpec(memory_space=pl.ANY)],
            out_specs=pl.BlockSpec((tmg, d), lambda i, src, nv: (i, 0)),
            scratch_shapes=[pltpu.VMEM((2, tmg, d), F32), pltpu.SemaphoreType.DMA((2,))]),
        out_shape=jax.ShapeDtypeStruct((n_tiles * tmg, d), BF16),
        compiler_params=_cparams("arbitrary"),
        name="moe_gather",
    )(src, n_valid, h)


def _mm_kernel(w_is_transposed, a_ref, w_ref, o_ref, wb_ref):
    @pl.when(pl.program_id(1) == 0)
    def _():
        wb_ref[...] = w_ref[...].astype(BF16)
    if w_is_transposed:
        acc = lax.dot_general(a_ref[...], wb_ref[...], (((1,), (1,)), ((), ())), preferred_element_type=F32)
    else:
        acc = jnp.dot(a_ref[...], wb_ref[...], preferred_element_type=F32)
    o_ref[...] = acc.astype(o_ref.dtype)


def _matmul(a, w, n_out, tn, tm=512, out_dtype=F32, name="matmul", w_is_transposed=False, col_block0=0):
    m, k = a.shape
    tm = _tile(m, tm)
    if w_is_transposed:
        w_spec = pl.BlockSpec((tn, k), lambda j, i: (col_block0 + j, 0))
        wb_shape = (tn, k)
    else:
        w_spec = pl.BlockSpec((k, tn), lambda j, i: (0, col_block0 + j))
        wb_shape = (k, tn)
    return pl.pallas_call(
        functools.partial(_mm_kernel, w_is_transposed),
        grid=(n_out // tn, m // tm),
        in_specs=[pl.BlockSpec((tm, k), lambda j, i: (i, 0)), w_spec],
        out_specs=pl.BlockSpec((tm, tn), lambda j, i: (i, j)),
        out_shape=jax.ShapeDtypeStruct((m, n_out), out_dtype),
        scratch_shapes=[pltpu.VMEM(wb_shape, BF16)],
        compiler_params=_cparams("parallel", "arbitrary"),
        name=name,
    )(a, w)


def _new_expert(te_ref):
    i = pl.program_id(1)
    return (i == 0) | (te_ref[i] != te_ref[jnp.maximum(i - 1, 0)])


def _swiglu_up_kernel(te_ref, nv_ref, a_ref, wg_ref, wu_ref, o_ref, wgb_ref, wub_ref):
    @pl.when(_new_expert(te_ref))
    def _():
        wgb_ref[...] = wg_ref[0].astype(BF16)
        wub_ref[...] = wu_ref[0].astype(BF16)

    @pl.when(pl.program_id(1) < nv_ref[0])
    def _():
        a = a_ref[...]
        gate = jnp.dot(a, wgb_ref[...], preferred_element_type=F32)
        up = jnp.dot(a, wub_ref[...], preferred_element_type=F32)
        o_ref[...] = (_silu(gate) * up).astype(o_ref.dtype)

    @pl.when(pl.program_id(1) >= nv_ref[0])
    def _():
        o_ref[...] = jnp.zeros_like(o_ref)


def _swiglu_up(a, w_up, te, nv, tm, tn, name):
    m, k = a.shape
    f = w_up.shape[2] // 2
    nj = f // tn
    row = lambda j, i, te, nv: jnp.minimum(i, nv[0] - 1)
    return pl.pallas_call(
        _swiglu_up_kernel,
        grid_spec=pltpu.PrefetchScalarGridSpec(
            num_scalar_prefetch=2,
            grid=(nj, m // tm),
            in_specs=[pl.BlockSpec((tm, k), lambda j, i, te, nv: (row(j, i, te, nv), 0)),
                      pl.BlockSpec((1, k, tn), lambda j, i, te, nv: (te[i], 0, j)),
                      pl.BlockSpec((1, k, tn), lambda j, i, te, nv: (te[i], 0, nj + j))],
            out_specs=pl.BlockSpec((tm, tn), lambda j, i, te, nv: (i, j)),
            scratch_shapes=[pltpu.VMEM((k, tn), BF16), pltpu.VMEM((k, tn), BF16)]),
        out_shape=jax.ShapeDtypeStruct((m, f), BF16),
        compiler_params=_cparams("parallel", "arbitrary"),
        name=name,
    )(te, nv, a, w_up, w_up)


def _down_kernel(te_ref, nv_ref, a_ref, w_ref, o_ref, wb_ref):
    @pl.when(_new_expert(te_ref))
    def _():
        wb_ref[...] = w_ref[0].astype(BF16)

    @pl.when(pl.program_id(1) < nv_ref[0])
    def _():
        o_ref[...] = jnp.dot(a_ref[...], wb_ref[...], preferred_element_type=F32)

    @pl.when(pl.program_id(1) >= nv_ref[0])
    def _():
        o_ref[...] = jnp.zeros_like(o_ref)


def _grouped_down(a, w_down, te, nv, tm, tn, name):
    m, f = a.shape
    d = w_down.shape[2]
    row = lambda j, i, te, nv: jnp.minimum(i, nv[0] - 1)
    return pl.pallas_call(
        _down_kernel,
        grid_spec=pltpu.PrefetchScalarGridSpec(
            num_scalar_prefetch=2,
            grid=(d // tn, m // tm),
            in_specs=[pl.BlockSpec((tm, f), lambda j, i, te, nv: (row(j, i, te, nv), 0)),
                      pl.BlockSpec((1, f, tn), lambda j, i, te, nv: (te[i], 0, j))],
            out_specs=pl.BlockSpec((tm, tn), lambda j, i, te, nv: (i, j)),
            scratch_shapes=[pltpu.VMEM((f, tn), BF16)]),
        out_shape=jax.ShapeDtypeStruct((m, d), F32),
        compiler_params=_cparams("parallel", "arbitrary"),
        name=name,
    )(te, nv, a, w_down)


def _shift_rows(carry, u, j):
    ext = jnp.concatenate([carry, u], axis=0)
    return pltpu.roll(ext, j, 0)[8:]


def _sconv_prompt_kernel(bg_ref, cg_ref, xi_ref, w_ref, o_ref, st_ref, carry_ref):
    @pl.when(pl.program_id(1) == 0)
    def _():
        carry_ref[...] = jnp.zeros_like(carry_ref)
    u = cg_ref[...] * xi_ref[...]
    carry = carry_ref[...]
    w = w_ref[...]
    y = w[2:3] * u + w[1:2] * _shift_rows(carry, u, 1) + w[0:1] * _shift_rows(carry, u, 2)
    o_ref[...] = (bg_ref[...] * y).astype(o_ref.dtype)
    tail = u[u.shape[0] - 8:]
    carry_ref[...] = tail
    st_ref[0] = tail


def _sconv_prompt(s_in, conv_w, n_seq, t_len, d, tt=256):
    tt = _tile(t_len, tt)
    nt = t_len // tt
    blk = lambda c: pl.BlockSpec((tt, d), lambda b, t: (b * nt + t, c))
    return pl.pallas_call(
        _sconv_prompt_kernel,
        grid=(n_seq, nt),
        in_specs=[blk(0), blk(1), blk(2), pl.BlockSpec((3, d), lambda b, t: (0, 0))],
        out_specs=[pl.BlockSpec((tt, d), lambda b, t: (b * nt + t, 0)),
                   pl.BlockSpec((1, 8, d), lambda b, t: (b, 0, 0))],
        out_shape=[jax.ShapeDtypeStruct((n_seq * t_len, d), BF16), jax.ShapeDtypeStruct((n_seq, 8, d), F32)],
        scratch_shapes=[pltpu.VMEM((8, d), F32)],
        compiler_params=_cparams("parallel", "arbitrary"),
        name="sconv_prompt",
    )(s_in, s_in, s_in, conv_w)


def _sconv_sample_kernel(nb, bg_ref, cg_ref, xi_ref, s0_ref, s1_ref, w_ref, o_ref, u_ref):
    u = cg_ref[...] * xi_ref[...]
    full = jnp.concatenate([s0_ref[...], s1_ref[...], u], axis=0)
    n = u.shape[0]
    w = w_ref[...]
    y = w[0:1] * full[0:n] + w[1:2] * full[nb:nb + n] + w[2:3] * full[2 * nb:2 * nb + n]
    o_ref[...] = (bg_ref[...] * y).astype(o_ref.dtype)
    u_ref[...] = u


def _sconv_sample(s_in, state2d, conv_w, row_block, nb, ns, d, ct=512):
    ct = _tile(d, ct)
    nj = d // ct
    blk = lambda c: pl.BlockSpec((ns, ct), lambda j: (row_block, c * nj + j))
    st = lambda s: pl.BlockSpec((nb, ct), lambda j: (0, s * nj + j))
    return pl.pallas_call(
        functools.partial(_sconv_sample_kernel, nb),
        grid=(nj,),
        in_specs=[blk(0), blk(1), blk(2), st(0), st(1), pl.BlockSpec((3, ct), lambda j: (0, j))],
        out_specs=[pl.BlockSpec((ns, ct), lambda j: (0, j)), pl.BlockSpec((ns, ct), lambda j: (0, j))],
        out_shape=[jax.ShapeDtypeStruct((ns, d), BF16), jax.ShapeDtypeStruct((ns, d), F32)],
        compiler_params=_cparams("parallel"),
        name="sconv_sample",
    )(s_in, s_in, s_in, state2d, state2d, conv_w)


def _l2norm_heads(x):
    parts = []
    for h in range(x.shape[1] // HEAD):
        seg = x[:, h * HEAD:(h + 1) * HEAD]
        parts.append(seg * lax.rsqrt(jnp.sum(seg * seg, axis=-1, keepdims=True) + NORM_EPS))
    return jnp.concatenate(parts, axis=1) if len(parts) > 1 else parts[0]


def _store_qkv(o_ref, act, is_qk):
    @pl.when(is_qk)
    def _():
        o_ref[...] = _l2norm_heads(act)

    @pl.when(jnp.logical_not(is_qk))
    def _():
        o_ref[...] = act


def _gconv_sample_kernel(n_qk_tiles, nb, s0_ref, s1_ref, s2_ref, x_ref, w_ref, o_ref):
    u = x_ref[...]
    full = jnp.concatenate([s0_ref[0], s1_ref[0], s2_ref[0], u], axis=0)
    n = u.shape[0]
    w = w_ref[...]
    y = (w[0:1] * full[0:n] + w[1:2] * full[nb:nb + n] + w[2:3] * full[2 * nb:2 * nb + n]
         + w[3:4] * full[3 * nb:3 * nb + n])
    _store_qkv(o_ref, _silu(y), pl.program_id(0) < n_qk_tiles)


def _gconv_sample(proj, state_tm, conv_w, row_block, nb, ns, d, ct=512):
    ch = 4 * d
    ct = _tile(ch, ct)
    nj = ch // ct
    st = lambda s: pl.BlockSpec((1, nb, ct), lambda j: (s, 0, j))
    return pl.pallas_call(
        functools.partial(_gconv_sample_kernel, 2 * d // ct, nb),
        grid=(nj,),
        in_specs=[st(0), st(1), st(2), pl.BlockSpec((ns, ct), lambda j: (row_block, j)),
                  pl.BlockSpec((4, ct), lambda j: (0, j))],
        out_specs=pl.BlockSpec((ns, ct), lambda j: (0, j)),
        out_shape=jax.ShapeDtypeStruct((ns, ch), F32),
        compiler_params=_cparams("parallel"),
        name="gdn_conv_sample",
    )(state_tm, state_tm, state_tm, proj, conv_w)


def _gates_kernel(chunk, ba_ref, alog_ref, dtb_ref, beta_ref, gc_ref, gtot_ref, gct_ref):
    ba = ba_ref[...]
    nh = ba.shape[1] // 2
    r = ba.shape[0]
    hps = beta_ref.shape[2]

    def store_by_group(ref, x):
        for grp in range(nh // hps):
            ref[grp] = x[:, grp * hps:(grp + 1) * hps]
    store_by_group(beta_ref, jax.nn.sigmoid(ba[:, :nh]))
    g = -jnp.exp(alog_ref[...]) * _softplus(ba[:, nh:] + dtb_ref[...])
    ri = lax.broadcasted_iota(jnp.int32, (r, r), 0)
    ci = lax.broadcasted_iota(jnp.int32, (r, r), 1)
    same = _blk(ri, chunk) == _blk(ci, chunk)
    tri = jnp.where(same & (ri >= ci), 1.0, 0.0)
    blk = jnp.where(same, 1.0, 0.0)
    gc = jnp.dot(tri, g, precision=HIGHEST, preferred_element_type=F32)
    store_by_group(gc_ref, gc)
    store_by_group(gtot_ref, jnp.dot(blk, g, precision=HIGHEST, preferred_element_type=F32))
    eye = jnp.where(lax.broadcasted_iota(jnp.int32, (nh, nh), 0) == lax.broadcasted_iota(jnp.int32, (nh, nh), 1),
                    1.0, 0.0)
    gct_ref[0] = lax.dot_general(eye, gc, (((1,), (1,)), ((), ())), precision=HIGHEST,
                                 preferred_element_type=F32)


def _gates(ba, a_log, dt_bias, chunk, r, hps):
    n, h2 = ba.shape
    nh = h2 // 2
    row = pl.BlockSpec((nh // hps, r, hps), lambda i: (0, i, 0))
    vec = pl.BlockSpec((1, nh), lambda i: (0, 0))
    return pl.pallas_call(
        functools.partial(_gates_kernel, chunk),
        grid=(n // r,),
        in_specs=[pl.BlockSpec((r, h2), lambda i: (i, 0)), vec, vec],
        out_specs=[row, row, row, pl.BlockSpec((1, nh, r), lambda i: (i, 0, 0))],
        out_shape=[jax.ShapeDtypeStruct((nh // hps, n, hps), F32)] * 3 + [jax.ShapeDtypeStruct((n // r, nh, r), F32)],
        compiler_params=_cparams("parallel"),
        name="gdn_gates",
    )(ba, a_log.reshape(1, nh), dt_bias.reshape(1, nh))


def _chunk_terms(chunk, q, k, v, beta, gc, gtot, gc_row):
    r = q.shape[0]
    ri = lax.broadcasted_iota(jnp.int32, (r, r), 0)
    ci = lax.broadcasted_iota(jnp.int32, (r, r), 1)
    causal = (_blk(ri, chunk) == _blk(ci, chunk)) & (ri >= ci)
    decay = jnp.exp(jnp.where(causal, gc - gc_row, -jnp.inf))
    kb = k * beta
    low = jnp.where(ri > ci, _dot_nt(kb, k) * decay, 0.0)
    e = -jnp.where(_blk(ri, 2) == _blk(ci, 2), low, 0.0)
    s = 2
    while s < chunk:
        lm = jnp.where((_blk(ri, 2 * s) == _blk(ci, 2 * s)) & (_blk(ri, s) != _blk(ci, s)), low, 0.0)
        x = lm + _dot(e, lm)
        e = e - (x + _dot(x, e))
        s *= 2
    eg = jnp.exp(gc)
    rhs = jnp.concatenate([v * beta, kb * eg], axis=1)
    uw = rhs + _dot(e, rhs)
    dv = v.shape[1]
    qs = q * (HEAD ** -0.5)
    a_intra = _dot_nt(qs, k) * decay
    return uw[:, :dv], uw[:, dv:], a_intra, qs * eg, k * jnp.exp(gtot - gc)


def _paired_chunk_terms(chunk, heads):
    r, pair = heads[0][0].shape[0], 2 * chunk
    ii = lax.broadcasted_iota(jnp.int32, (chunk, pair), 0)
    lane = lax.broadcasted_iota(jnp.int32, (chunk, pair), 1)
    right = lane >= chunk
    jj = lane & (chunk - 1)
    diag = lambda full: jnp.where(right, full[chunk:], full[:chunk])
    spread = lambda y: jnp.concatenate([jnp.where(right, 0.0, y), jnp.where(right, y, 0.0)], axis=0)
    slices = [slice(p * pair, (p + 1) * pair) for p in range(r // pair)]
    kbs = [k * beta for (q, k, v, beta, gc, gtot, gc_row) in heads]
    egs = [jnp.exp(gc) for (q, k, v, beta, gc, gtot, gc_row) in heads]
    qss = [q * (HEAD ** -0.5) for (q, k, v, beta, gc, gtot, gc_row) in heads]
    rhss = [jnp.concatenate([h[2] * h[3], kb * eg], axis=1) for h, kb, eg in zip(heads, kbs, egs)]
    probs = [(i, sl) for i in range(len(heads)) for sl in slices]
    decays = []
    for i, sl in probs:
        gcp, gc_row = heads[i][4][sl], heads[i][6]
        decays.append(jnp.exp(jnp.where(ii >= jj, jnp.where(right, gcp[chunk:], gcp[:chunk]) - gc_row[:, sl],
                                        -jnp.inf)))
    kks = [_dot_nt(kbs[i][sl], heads[i][1][sl]) for i, sl in probs]
    lows = [jnp.where(ii > jj, diag(kk) * decay, 0.0) for kk, decay in zip(kks, decays)]
    es = [-jnp.where(_blk(ii, 2) == _blk(jj, 2), low, 0.0) for low in lows]
    s = 2
    while s < chunk:
        level = (_blk(ii, 2 * s) == _blk(jj, 2 * s)) & (_blk(ii, s) != _blk(jj, s))
        lms = [jnp.where(level, low, 0.0) for low in lows]
        lm_sp = [spread(lm).astype(BF16) for lm in lms]
        xs = [lm + _dot(e, sp) for lm, e, sp in zip(lms, es, lm_sp)]
        e_sp = [spread(e).astype(BF16) for e in es]
        es = [e - (x + _dot(x, sp)) for e, x, sp in zip(es, xs, e_sp)]
        s *= 2
    e_sp = [spread(e).astype(BF16) for e in es]
    uws = [rhss[i][sl] + _dot(sp, rhss[i][sl]) for (i, sl), sp in zip(probs, e_sp)]
    qks = [_dot_nt(qss[i][sl], heads[i][1][sl]) for i, sl in probs]
    atts = [diag(qk) * decay for qk, decay in zip(qks, decays)]
    out = []
    n_pairs = len(slices)
    for i, (q, k, v, beta, gc, gtot, gc_row) in enumerate(heads):
        uw = jnp.concatenate(uws[i * n_pairs:(i + 1) * n_pairs], axis=0)
        a_chunks = []
        for a in atts[i * n_pairs:(i + 1) * n_pairs]:
            a_chunks += [jnp.where(right, 0.0, a), jnp.where(right, a, 0.0)]
        dv = v.shape[1]
        out.append((uw[:, :dv], uw[:, dv:], a_chunks, qss[i] * egs[i], k * jnp.exp(gtot - gc)))
    return out


def _gated_norm(o, z, gain):
    return (_rms(o, gain) * _silu(z)).astype(BF16)


def _conv_act_head(x_ref, w_ref, carry_ref, h, l2norm):
    sl = slice(h * HEAD, (h + 1) * HEAD)
    u, w, carry = x_ref[:, sl], w_ref[:, sl], carry_ref[:, sl]
    y = (w[3:4] * u + w[2:3] * _shift_rows(carry, u, 1) + w[1:2] * _shift_rows(carry, u, 2)
         + w[0:1] * _shift_rows(carry, u, 3))
    a = _silu(y)
    if l2norm:
        a = a * lax.rsqrt(jnp.sum(a * a, axis=-1, keepdims=True) + NORM_EPS)
    return a


def _gdn_prompt_kernel(chunk, hps, q_ref, k_ref, v_ref, z_ref, wq_ref, wk_ref, wv_ref, beta_ref, gc_ref, gtot_ref,
                       gct_ref, gn_ref, o_ref, s_ref, s_scr, cq_scr, ck_scr, cv_scr):
    @pl.when(pl.program_id(2) == 0)
    def _():
        s_scr[...] = jnp.zeros_like(s_scr)
        cq_scr[...] = jnp.zeros_like(cq_scr)
        ck_scr[...] = jnp.zeros_like(ck_scr)
        cv_scr[...] = jnp.zeros_like(cv_scr)
    r = q_ref.shape[0]
    h0 = pl.program_id(1) * hps
    paired = 2 * chunk == HEAD and r % HEAD == 0
    qs = [_conv_act_head(q_ref, wq_ref, cq_scr, kh, True) for kh in range(hps // 2)]
    ks = [_conv_act_head(k_ref, wk_ref, ck_scr, kh, True) for kh in range(hps // 2)]
    vs = [_conv_act_head(v_ref, wv_ref, cv_scr, hl, False) for hl in range(hps)]
    cq_scr[...] = q_ref[r - 8:r, :]
    ck_scr[...] = k_ref[r - 8:r, :]
    cv_scr[...] = v_ref[r - 8:r, :]
    heads = []
    for hl in range(hps):
        heads.append((qs[hl // 2], ks[hl // 2], vs[hl], beta_ref[0][:, hl:hl + 1],
                      gc_ref[0][:, hl:hl + 1], gtot_ref[0][:, hl:hl + 1],
                      gct_ref[0, pl.ds(h0 + hl, 1), :]))
    if paired:
        terms = _paired_chunk_terms(chunk, heads)
    else:
        terms = [_chunk_terms(chunk, *head) for head in heads]
    terms = [t + (jnp.exp(head[5]),) for t, head in zip(terms, heads)]
    states = [s_scr[hl] for hl in range(hps)]
    outs = [[] for _ in range(hps)]
    for c in range(r // chunk):
        lo, hi = c * chunk, (c + 1) * chunk
        for hl in range(hps):
            u, w, a, qd, kd, egt = terms[hl]
            s = states[hl]
            pq = _dot(jnp.concatenate([w[lo:hi], qd[lo:hi]], axis=0), s)
            v_new = u[lo:hi] - pq[:chunk]
            if paired:
                intra = _dot(a[c], jnp.concatenate([v_new, v_new], axis=0))
            else:
                intra = _dot(a[lo:hi, lo:hi], v_new)
            outs[hl].append(pq[chunk:] + intra)
            states[hl] = s * egt[lo:lo + 1] + _dot_tn(kd[lo:hi], v_new)
    for hl in range(hps):
        s_scr[hl] = states[hl]
        o = jnp.concatenate(outs[hl], axis=0)
        o_ref[:, hl * HEAD:(hl + 1) * HEAD] = _gated_norm(o, z_ref[:, hl * HEAD:(hl + 1) * HEAD], gn_ref[...])
    s_ref[0] = s_scr[...]


def _gdn_prompt(proj, conv_w, beta, gc, gtot, gct, g_onorm, n_seq, t_len, d, chunk, r=256, hps=8):
    nvh = 2 * d // HEAD
    ns = t_len // r
    kc = hps // 2 * HEAD
    vc = hps * HEAD
    rows = lambda b, g, s: b * ns + s
    return pl.pallas_call(
        functools.partial(_gdn_prompt_kernel, chunk, hps),
        grid=(n_seq, nvh // hps, ns),
        in_specs=[pl.BlockSpec((r, kc), lambda b, g, s: (rows(b, g, s), g)),
                  pl.BlockSpec((r, kc), lambda b, g, s: (rows(b, g, s), d // kc + g)),
                  pl.BlockSpec((r, vc), lambda b, g, s: (rows(b, g, s), 2 * d // vc + g)),
                  pl.BlockSpec((r, vc), lambda b, g, s: (rows(b, g, s), 4 * d // vc + g)),
                  pl.BlockSpec((4, kc), lambda b, g, s: (0, g)),
                  pl.BlockSpec((4, kc), lambda b, g, s: (0, d // kc + g)),
                  pl.BlockSpec((4, vc), lambda b, g, s: (0, 2 * d // vc + g)),
                  pl.BlockSpec((1, r, hps), lambda b, g, s: (g, rows(b, g, s), 0)),
                  pl.BlockSpec((1, r, hps), lambda b, g, s: (g, rows(b, g, s), 0)),
                  pl.BlockSpec((1, r, hps), lambda b, g, s: (g, rows(b, g, s), 0)),
                  pl.BlockSpec((1, nvh, r), lambda b, g, s: (rows(b, g, s), 0, 0)),
                  pl.BlockSpec((1, HEAD), lambda b, g, s: (0, 0))],
        out_specs=[pl.BlockSpec((r, vc), lambda b, g, s: (rows(b, g, s), g)),
                   pl.BlockSpec((1, hps, HEAD, HEAD), lambda b, g, s: (b, g, 0, 0))],
        out_shape=[jax.ShapeDtypeStruct((n_seq * t_len, 2 * d), BF16),
                   jax.ShapeDtypeStruct((n_seq, nvh, HEAD, HEAD), F32)],
        scratch_shapes=[pltpu.VMEM((hps, HEAD, HEAD), F32), pltpu.VMEM((8, kc), F32), pltpu.VMEM((8, kc), F32),
                        pltpu.VMEM((8, vc), F32)],
        compiler_params=_cparams("parallel", "parallel", "arbitrary"),
        name="gdn_prompt",
    )(proj, proj, proj, proj, conv_w, conv_w, conv_w, beta, gc, gtot, gct, g_onorm.reshape(1, HEAD))


def _gdn_sample_kernel(chunk, hps, q_ref, k_ref, v_ref, z_ref, beta_ref, gc_ref, gtot_ref, gct_ref, gn_ref, s0_ref,
                       o_ref, s_ref, wq_scr, kd_scr, eg_scr, pq_scr):
    r = q_ref.shape[0]
    h0 = pl.program_id(0) * hps
    per_row = 8 // chunk
    terms = []
    for hl in range(hps):
        kh = hl // 2
        q = q_ref[:, kh * HEAD:(kh + 1) * HEAD]
        k = k_ref[:, kh * HEAD:(kh + 1) * HEAD]
        v = v_ref[:, hl * HEAD:(hl + 1) * HEAD]
        beta = beta_ref[0][:, hl:hl + 1]
        gc = gc_ref[0][:, hl:hl + 1]
        gtot = gtot_ref[0][:, hl:hl + 1]
        gc_row = gct_ref[0, pl.ds(h0 + hl, 1), :]
        u, w, a, qd, kd = _chunk_terms(chunk, q, k, v, beta, gc, gtot, gc_row)
        wq_scr[hl, :, 0:8, :] = w.reshape(r // 8, 8, HEAD)
        wq_scr[hl, :, 8:16, :] = qd.reshape(r // 8, 8, HEAD)
        kd_scr[hl] = kd
        eg_scr[hl] = jnp.broadcast_to(jnp.exp(gtot), (r, HEAD))
        terms.append((u, a))

    row16 = lax.broadcasted_iota(jnp.int32, (16, HEAD), 0)
    row8 = lax.broadcasted_iota(jnp.int32, (8, HEAD), 0)

    def pair_pred(p, carry):
        for hl in range(hps):
            lhs = wq_scr[hl, p]
            acc = jnp.zeros((16, HEAD), F32)
            for e in range(per_row):
                pq = _dot(lhs, s0_ref[p * per_row + e, hl])
                acc = jnp.where(_blk(row16 & 7, chunk) == e, pq, acc)
            pq_scr[hl, p] = acc
        return carry
    lax.fori_loop(0, r // 8, pair_pred, 0)

    v_news = []
    for hl in range(hps):
        u, a = terms[hl]
        pq = pq_scr[hl]
        v_new = u - pq[:, 0:8, :].reshape(r, HEAD)
        o = pq[:, 8:16, :].reshape(r, HEAD) + _dot(a, v_new)
        o_ref[:, hl * HEAD:(hl + 1) * HEAD] = _gated_norm(o, z_ref[:, hl * HEAD:(hl + 1) * HEAD], gn_ref[...])
        pq_scr[hl, :, 0:8, :] = v_new.reshape(r // 8, 8, HEAD)

    def pair_update(p, carry):
        for hl in range(hps):
            v_new = pq_scr[hl, p, 0:8, :]
            kd = kd_scr[hl, pl.ds(pl.multiple_of(p * 8, 8), 8), :]
            for e in range(per_row):
                b = p * per_row + e
                decay = eg_scr[hl, pl.ds(p * 8 + e * chunk, 1), :]
                kd_e = jnp.where(_blk(row8, chunk) == e, kd, 0.0)
                s_ref[b, hl] = s0_ref[b, hl] * decay + _dot_tn(kd_e, v_new)
        return carry
    lax.fori_loop(0, r // 8, pair_update, 0)


def _gdn_sample(qkv, z, beta, gc, gtot, gct, g_onorm, state, d, chunk, r=128, hps=4):
    n = qkv.shape[0]
    nvh = 2 * d // HEAD
    kc = hps // 2 * HEAD
    vc = hps * HEAD
    nb = r // chunk
    return pl.pallas_call(
        functools.partial(_gdn_sample_kernel, chunk, hps),
        grid=(nvh // hps, n // r),
        in_specs=[pl.BlockSpec((r, kc), lambda g, s: (s, g)),
                  pl.BlockSpec((r, kc), lambda g, s: (s, d // kc + g)),
                  pl.BlockSpec((r, vc), lambda g, s: (s, 2 * d // vc + g)),
                  pl.BlockSpec((r, vc), lambda g, s: (s, g)),
                  pl.BlockSpec((1, r, hps), lambda g, s: (g, s, 0)),
                  pl.BlockSpec((1, r, hps), lambda g, s: (g, s, 0)),
                  pl.BlockSpec((1, r, hps), lambda g, s: (g, s, 0)),
                  pl.BlockSpec((1, nvh, r), lambda g, s: (s, 0, 0)),
                  pl.BlockSpec((1, HEAD), lambda g, s: (0, 0)),
                  pl.BlockSpec((nb, hps, HEAD, HEAD), lambda g, s: (s, g, 0, 0))],
        out_specs=[pl.BlockSpec((r, vc), lambda g, s: (s, g)),
                   pl.BlockSpec((nb, hps, HEAD, HEAD), lambda g, s: (s, g, 0, 0))],
        out_shape=[jax.ShapeDtypeStruct((n, 2 * d), BF16),
                   jax.ShapeDtypeStruct(state.shape, F32)],
        scratch_shapes=[pltpu.VMEM((hps, r // 8, 16, HEAD), F32), pltpu.VMEM((hps, r, HEAD), F32),
                        pltpu.VMEM((hps, r, HEAD), F32), pltpu.VMEM((hps, r // 8, 16, HEAD), F32)],
        compiler_params=_cparams("parallel", "parallel"),
        name="gdn_sample",
    )(qkv, qkv, qkv, z, beta, gc, gtot, gct, g_onorm.reshape(1, HEAD), state)


def kernel(x_prompt, x_sample, c_prompt, c_sample, state_gdn, state_gdn_conv, state_sconv, w_ada, b_ada,
           g_norm_mix, g_norm_ffn, g_norm_out, gdn_w_in, gdn_conv_w, gdn_a_log, gdn_dt_bias, gdn_g_onorm,
           gdn_w_out, sc_w_in, sc_conv_w, sc_w_out, ffn_w_up, ffn_w_down, moe_w_router, moe_b_router,
           moe_w_up, moe_w_down):
    nbp, t_len, d = x_prompt.shape
    nbs, steps, _ = x_sample.shape
    n_p, n_s = nbp * t_len, nbs * steps
    nvh = 2 * d // HEAD
    chunk_p = min(64, t_len)
    tm = nbs
    assert t_len % tm == 0 and n_p % n_s == 0 and 8 % steps == 0

    n = n_p + n_s
    tm_mm = _tile(n, 1088)
    x_in = (x_prompt.reshape(n_p, d), x_sample.reshape(nbs, steps * d))
    pad = (-(nbp + nbs)) % 8
    c_all = jnp.concatenate([c_prompt, c_sample, jnp.zeros((pad, d), F32)], axis=0)
    m = _ada(c_all, w_ada, b_ada)
    mods = [_Mod(m[i, :nbp].reshape(nbp, 1, 6 * d), m[i, nbp:nbp + nbs], d, tm, n_p // tm, t_len // tm)
            for i in range(2)]
    SH1, SC1, GA1, SH2, SC2, GA2 = range(6)

    h = _prep(x_in, g_norm_mix[0], mods[0], SC1, SH1, n, tm)
    w_in_t = jnp.swapaxes(gdn_w_in[0], 0, 1)
    proj = _matmul(h, w_in_t, 6 * d, _tile(6 * d, 1024), tm_mm, name="gdn_in_proj", w_is_transposed=True)
    ba = _matmul(h, w_in_t, 2 * nvh, 2 * nvh, tm_mm, name="gdn_in_proj_ba", w_is_transposed=True,
                 col_block0=6 * d // (2 * nvh))
    conv_w = gdn_conv_w[0]
    st_gconv = jnp.transpose(state_gdn_conv[0], (1, 0, 2))
    qkv_s = _gconv_sample(proj, st_gconv, conv_w, n_p // n_s, nbs, n_s, d)
    to_seq_major = lambda a: a.reshape(steps, nbs, a.shape[-1]).transpose(1, 0, 2).reshape(n_s, a.shape[-1])
    qkv_s = to_seq_major(qkv_s)
    z_s = to_seq_major(proj[n_p:, 4 * d:6 * d])
    ba_s = to_seq_major(ba[n_p:])
    hps_p, hps_s = min(8, nvh), min(4, nvh)
    beta_p, gc_p, gtot_p, gct_p = _gates(ba[:n_p], gdn_a_log[0], gdn_dt_bias[0], chunk_p, 256, hps_p)
    beta_s, gc_s, gtot_s, gct_s = _gates(ba_s, gdn_a_log[0], gdn_dt_bias[0], steps, 128, hps_s)
    o_p, gdn_p = _gdn_prompt(proj, conv_w, beta_p, gc_p, gtot_p, gct_p, gdn_g_onorm[0], nbp, t_len, d, chunk_p,
                             hps=hps_p)
    o_s, gdn_s = _gdn_sample(qkv_s, z_s, beta_s, gc_s, gtot_s, gct_s, gdn_g_onorm[0], state_gdn[0], d, steps,
                             hps=hps_s)
    tm_p = _tile(n_p, 1024)
    y = (_matmul(o_p, gdn_w_out[0], d, _tile(d, 512), tm_p, name="gdn_out_proj"),
         _matmul(o_s, gdn_w_out[0], d, _tile(d, 512), n_s, name="gdn_out_proj_sample").reshape(nbs, steps * d))
    gconv_p = jnp.stack([proj[(b + 1) * t_len - 3:(b + 1) * t_len, :4 * d] for b in range(nbp)])
    gconv_s = proj[n_p + nbs:, :4 * d].reshape(3, nbs, 4 * d).transpose(1, 0, 2)

    x, h = _resid_prep(x_in, y, g_norm_ffn[0], mods[0], GA1, mods[0], SC2, SH2, n, tm)
    one_expert = jnp.zeros((n // tm_mm,), jnp.int32)
    all_tiles = jnp.full((1,), n // tm_mm, jnp.int32)
    act = _swiglu_up(h, ffn_w_up, one_expert, all_tiles, tm_mm, _tile(ffn_w_down.shape[1], 512), "ffn_swiglu_up")
    f = _matmul(act, ffn_w_down[0], d, _tile(d, 512), name="ffn_down")

    x, h = _resid_prep(x, f, g_norm_mix[1], mods[0], GA2, mods[1], SC1, SH1, n, tm)
    s_in = _matmul(h, sc_w_in[0], 3 * d, _tile(3 * d, 1024), tm_mm, name="sconv_in_proj")
    mix_p, sconv_tail = _sconv_prompt(s_in, sc_conv_w[0], nbp, t_len, d)
    mix_s, u_s = _sconv_sample(s_in, state_sconv[0].reshape(nbs, 2 * d), sc_conv_w[0], n_p // n_s, nbs, n_s, d)
    y_s = _matmul(mix_s, sc_w_out[0], d, _tile(d, 1024), n_s, name="sconv_out_proj_sample")
    y = (_matmul(mix_p, sc_w_out[0], d, _tile(d, 1024), tm_p, name="sconv_out_proj"),
         y_s.reshape(steps, nbs, d).transpose(1, 0, 2).reshape(nbs, steps * d))
    sconv_p = sconv_tail[:, 6:8]
    sconv_s = u_s[n_s - 2 * nbs:].reshape(2, nbs, d).transpose(1, 0, 2)

    x, h, info = _resid_route(x, y, g_norm_ffn[1], mods[1], GA1, mods[1], SC2, SH2,
                              moe_w_router[0], moe_b_router[0], tm)
    _, n_exp, f_moe, _ = moe_w_down.shape
    ck = 256
    pos, src, _, nv, chunk_start, chunk_count = _route_tables(info, n_exp, ck)
    h_sorted = _gather_rows(h, src, nv, ck)
    act = _moe_up(h_sorted, moe_w_up[0], chunk_start, chunk_count, ck, _tile(f_moe, 1024))
    y_sorted = _moe_down(act, moe_w_down[0], chunk_start, chunk_count, ck, _tile(d, 512))
    out_p, out_s = _final(x, y_sorted, pos, info, g_norm_out, mods[1], GA2, n_p, nbs, tm)

    y_prompt = out_p.reshape(nbp, t_len, d)
    y_sample = out_s.reshape(nbs, steps, d)
    return (y_prompt, y_sample, gdn_p[None], gconv_p[None], sconv_p[None], gdn_s[None], gconv_s[None], sconv_s[None])
```

```python
import functools

import jax
import jax.numpy as jnp
from jax import lax
from jax.experimental import pallas as pl
from jax.experimental.pallas import tpu as pltpu

F32 = jnp.float32
BF16 = jnp.bfloat16
NORM_EPS = 1e-6
HEAD = 128
V7X_VMEM_LIMIT = 56 * 1024 * 1024
HIGHEST = lax.Precision.HIGHEST


def _cparams(*sem):
    return pltpu.CompilerParams(dimension_semantics=sem, vmem_limit_bytes=V7X_VMEM_LIMIT)


def _tile(n, pref):
    return pref if n % pref == 0 else n


def _silu(x):
    return 0.5 * x * (1.0 + jnp.tanh(0.5 * x))


def _softplus(x):
    return jnp.maximum(x, 0.0) + jnp.log1p(jnp.exp(-jnp.abs(x)))


def _blk(idx, size):
    assert size & (size - 1) == 0
    return lax.shift_right_logical(idx, size.bit_length() - 1)


def _dot(a, b):
    return jnp.dot(a.astype(BF16), b.astype(BF16), preferred_element_type=F32)


def _dot_nt(a, b):
    return lax.dot_general(a.astype(BF16), b.astype(BF16), (((1,), (1,)), ((), ())), preferred_element_type=F32)


def _dot_tn(a, b):
    return lax.dot_general(a.astype(BF16), b.astype(BF16), (((0,), (0,)), ((), ())), preferred_element_type=F32)


def _ada_kernel(c_ref, w_ref, b_ref, o_ref):
    c = c_ref[...]
    o_ref[0] = _dot(_silu(c), w_ref[0]) + b_ref[0]


def _ada(c_all, w_ada, b_ada):
    nl, d, n6 = w_ada.shape
    r = c_all.shape[0]
    tn = _tile(n6, 1024)
    return pl.pallas_call(
        _ada_kernel,
        grid=(nl, n6 // tn),
        in_specs=[pl.BlockSpec((r, d), lambda l, j: (0, 0)),
                  pl.BlockSpec((1, d, tn), lambda l, j: (l, 0, j)),
                  pl.BlockSpec((1, 1, tn), lambda l, j: (l, 0, j))],
        out_specs=pl.BlockSpec((1, r, tn), lambda l, j: (l, 0, j)),
        out_shape=jax.ShapeDtypeStruct((nl, r, n6), F32),
        compiler_params=_cparams("parallel", "parallel"),
        name="ada_mod",
    )(c_all, w_ada, b_ada.reshape(nl, 1, n6))


class _Mod:
    def __init__(self, mod_p, mod_s, d, tm, n_ptiles, tiles_per_seq):
        self.arrays = (mod_p, mod_s)
        self.n_ptiles = n_ptiles
        nb = mod_p.shape[0]

        def specs(chunk):
            return [pl.BlockSpec((1, 1, d), lambda i, *_: (jnp.minimum(i // tiles_per_seq, nb - 1), 0, chunk)),
                    pl.BlockSpec((tm, d), lambda i, *_: (0, chunk))]
        self.specs = specs
        self.row_specs = [pl.BlockSpec((tm, d), lambda i, *_: (jnp.minimum(i, n_ptiles - 1), 0)),
                          pl.BlockSpec((tm, d), lambda i, *_: (0, jnp.maximum(i - n_ptiles, 0)))]

    def is_prompt(self):
        return pl.program_id(0) < self.n_ptiles

    def pick(self, p_ref, s_ref):
        return jnp.where(self.is_prompt(), p_ref[0], s_ref[...])

    def pick_rows(self, p_ref, s_ref):
        return jnp.where(self.is_prompt(), p_ref[...], s_ref[...])


def _rms(x, gain):
    ms = jnp.mean(x * x, axis=-1, keepdims=True)
    return x * lax.rsqrt(ms + NORM_EPS) * gain


def _prep_kernel(mod, xp_ref, xs_ref, g_ref, scp, scs, shp, shs, h_ref):
    y = _rms(mod.pick_rows(xp_ref, xs_ref), g_ref[...])
    h_ref[...] = (y * (1.0 + mod.pick(scp, scs)) + mod.pick(shp, shs)).astype(h_ref.dtype)


def _prep(x_pair, gain, mod, sc_chunk, sh_chunk, n, tm):
    d = gain.shape[0]
    return pl.pallas_call(
        functools.partial(_prep_kernel, mod),
        grid=(n // tm,),
        in_specs=mod.row_specs + [pl.BlockSpec((1, d), lambda i: (0, 0))] + mod.specs(sc_chunk) + mod.specs(sh_chunk),
        out_specs=pl.BlockSpec((tm, d), lambda i: (i, 0)),
        out_shape=jax.ShapeDtypeStruct((n, d), BF16),
        compiler_params=_cparams("parallel"),
        name="norm_mod",
    )(*x_pair, gain.reshape(1, d), *mod.arrays, *mod.arrays)


def _row_inputs(mod, a, tm, d):
    if isinstance(a, tuple):
        return mod.row_specs, list(a)
    return [pl.BlockSpec((tm, d), lambda i, *_: (i, 0))], [a]


def _read_rows(mod, refs):
    return mod.pick_rows(*refs) if len(refs) == 2 else refs[0][...]


def _resid_prep_kernel(mod, nx, ny, *refs):
    x_refs, y_refs = refs[:nx], refs[nx:nx + ny]
    g_ref, gap, gas, scp, scs, shp, shs, xo_ref, h_ref = refs[nx + ny:]
    x = _read_rows(mod, x_refs) + mod.pick(gap, gas) * _read_rows(mod, y_refs)
    xo_ref[...] = x
    y = _rms(x, g_ref[...])
    h_ref[...] = (y * (1.0 + mod.pick(scp, scs)) + mod.pick(shp, shs)).astype(h_ref.dtype)


def _resid_prep(x, y, gain, mod_gate, ga_chunk, mod, sc_chunk, sh_chunk, n, tm):
    d = gain.shape[0]
    row = pl.BlockSpec((tm, d), lambda i: (i, 0))
    x_specs, x_ops = _row_inputs(mod, x, tm, d)
    y_specs, y_ops = _row_inputs(mod, y, tm, d)
    return pl.pallas_call(
        functools.partial(_resid_prep_kernel, mod, len(x_ops), len(y_ops)),
        grid=(n // tm,),
        in_specs=x_specs + y_specs + [pl.BlockSpec((1, d), lambda i: (0, 0))]
        + mod_gate.specs(ga_chunk) + mod.specs(sc_chunk) + mod.specs(sh_chunk),
        out_specs=[row, row],
        out_shape=[jax.ShapeDtypeStruct((n, d), F32), jax.ShapeDtypeStruct((n, d), BF16)],
        compiler_params=_cparams("parallel"),
        name="resid_norm_mod",
    )(*x_ops, *y_ops, gain.reshape(1, d), *mod_gate.arrays, *mod.arrays, *mod.arrays)


def _resid_route_kernel(mod, n_exp, ny, x_ref, *refs):
    y_refs = refs[:ny]
    g_ref, gap, gas, scp, scs, shp, shs, wr_ref, br_ref, xo_ref, h_ref, info_ref = refs[ny:]
    x = x_ref[...] + mod.pick(gap, gas) * _read_rows(mod, y_refs)
    xo_ref[...] = x
    h = _rms(x, g_ref[...]) * (1.0 + mod.pick(scp, scs)) + mod.pick(shp, shs)
    h_ref[...] = h
    logits = jnp.dot(h, wr_ref[...], precision=HIGHEST, preferred_element_type=F32) + br_ref[...]
    lane = lax.broadcasted_iota(jnp.int32, logits.shape, 1).astype(F32)
    m1 = jnp.max(logits, axis=-1, keepdims=True)
    i1 = jnp.min(jnp.where(logits == m1, lane, float(n_exp)), axis=-1, keepdims=True)
    rest = jnp.where(lane == i1, -jnp.inf, logits)
    m2 = jnp.max(rest, axis=-1, keepdims=True)
    i2 = jnp.min(jnp.where(rest == m2, lane, float(n_exp)), axis=-1, keepdims=True)
    e2 = jnp.exp(m2 - m1)
    g1 = 1.0 / (1.0 + e2)
    g2 = e2 / (1.0 + e2)
    info_ref[...] = (jnp.where(lane == 0.0, i1, 0.0) + jnp.where(lane == 1.0, i2, 0.0)
                     + jnp.where(lane == 2.0, g1, 0.0) + jnp.where(lane == 3.0, g2, 0.0))


def _resid_route(x, y, gain, mod_gate, ga_chunk, mod, sc_chunk, sh_chunk, w_router, b_router, tm):
    n, d = x.shape
    n_exp = w_router.shape[1]
    row = pl.BlockSpec((tm, d), lambda i: (i, 0))
    y_specs, y_ops = _row_inputs(mod, y, tm, d)
    return pl.pallas_call(
        functools.partial(_resid_route_kernel, mod, n_exp, len(y_ops)),
        grid=(n // tm,),
        in_specs=[row] + y_specs + [pl.BlockSpec((1, d), lambda i: (0, 0))]
        + mod_gate.specs(ga_chunk) + mod.specs(sc_chunk) + mod.specs(sh_chunk)
        + [pl.BlockSpec((d, n_exp), lambda i: (0, 0)), pl.BlockSpec((1, n_exp), lambda i: (0, 0))],
        out_specs=[row, row, pl.BlockSpec((tm, n_exp), lambda i: (i, 0))],
        out_shape=[jax.ShapeDtypeStruct((n, d), F32), jax.ShapeDtypeStruct((n, d), F32),
                   jax.ShapeDtypeStruct((n, n_exp), F32)],
        compiler_params=_cparams("parallel"),
        name="resid_norm_route",
    )(x, *y_ops, gain.reshape(1, d), *mod_gate.arrays, *mod.arrays, *mod.arrays, w_router,
      b_router.reshape(1, n_exp))


def _row_copy(src_hbm, row, dst, r, sem):
    return pltpu.make_async_copy(src_hbm.at[pl.ds(row, 1)], dst.at[pl.ds(r, 1)], sem)


def _final_kernel(mod, pos_ref, x_ref, y_hbm, info_ref, g_ref, gap, gas, op_ref, os_ref, buf, sem):
    tm = x_ref.shape[0]
    i = pl.program_id(0)
    n_tiles = pl.num_programs(0)

    def issue(tile, slot):
        def body(r, carry):
            for k in range(2):
                _row_copy(y_hbm, pos_ref[tile * (2 * tm) + 2 * r + k], buf.at[slot, k], r,
                          sem.at[slot, k]).start(priority=k)
            return carry
        lax.fori_loop(0, tm, body, 0, unroll=4)

    @pl.when(i == 0)
    def _():
        issue(0, 0)

    for slot in range(2):
        @pl.when((i + 1 < n_tiles) & ((i + 1) % 2 == slot))
        def _():
            issue(i + 1, slot)

    for slot in range(2):
        @pl.when(i % 2 == slot)
        def _():
            for k in range(2):
                pltpu.make_async_copy(y_hbm.at[pl.ds(0, tm)], buf.at[slot, k], sem.at[slot, k]).wait()
            info = info_ref[...]
            f = info[:, 2:3] * buf[slot, 0] + info[:, 3:4] * buf[slot, 1]
            x = x_ref[...] + mod.pick(gap, gas) * f
            out = _rms(x, g_ref[...])

            @pl.when(mod.is_prompt())
            def _():
                op_ref[...] = out

            @pl.when(jnp.logical_not(mod.is_prompt()))
            def _():
                os_ref[...] = out


def _final(x, y_sorted, pos, info, gain, mod_gate, ga_chunk, n_p, nbs, tm):
    n, d = x.shape
    row = pl.BlockSpec((tm, d), lambda i, *_: (i, 0))
    return pl.pallas_call(
        functools.partial(_final_kernel, mod_gate),
        grid_spec=pltpu.PrefetchScalarGridSpec(
            num_scalar_prefetch=1,
            grid=(n // tm,),
            in_specs=[row, pl.BlockSpec(memory_space=pl.ANY),
                      pl.BlockSpec((tm, info.shape[1]), lambda i, *_: (i, 0)),
                      pl.BlockSpec((1, d), lambda i, *_: (0, 0))] + mod_gate.specs(ga_chunk),
            out_specs=mod_gate.row_specs,
            scratch_shapes=[pltpu.VMEM((2, 2, tm, d), F32), pltpu.SemaphoreType.DMA((2, 2))]),
        out_shape=[jax.ShapeDtypeStruct((n_p, d), F32), jax.ShapeDtypeStruct((nbs, (n - n_p) // nbs * d), F32)],
        compiler_params=_cparams("arbitrary"),
        name="moe_combine_final_norm",
    )(pos, x, y_sorted, info, gain.reshape(1, d), *mod_gate.arrays)


def _route_tables(info, n_exp, tmg):
    n = info.shape[0]
    e = info[:, :2].astype(jnp.int32)
    onehot = (e[:, :, None] == jnp.arange(n_exp, dtype=jnp.int32)[None, None, :]).astype(jnp.int32).sum(axis=1)
    counts = onehot.sum(axis=0)
    padded = (counts + tmg - 1) // tmg * tmg
    ends = jnp.cumsum(padded)
    starts = ends - padded
    rank = jnp.cumsum(onehot, axis=0) - onehot
    pos = (starts[e] + jnp.take_along_axis(rank, e, axis=1)).reshape(-1).astype(jnp.int32)
    n_tiles = -(-(2 * n) // tmg) + n_exp
    src = jnp.zeros((n_tiles * tmg,), jnp.int32).at[pos].set(jnp.repeat(jnp.arange(n, dtype=jnp.int32), 2))
    n_valid = (ends[-1] // tmg).astype(jnp.int32)
    tiles = jnp.arange(n_tiles, dtype=jnp.int32)
    te = jnp.minimum((tiles[:, None] * tmg >= ends[None, :]).astype(jnp.int32).sum(axis=1), n_exp - 1)
    te = jnp.where(tiles < n_valid, te, te[jnp.maximum(n_valid - 1, 0)])
    return pos, src, te, n_valid.reshape(1), (starts // tmg).astype(jnp.int32), (padded // tmg).astype(jnp.int32)


def _expert_rows_loop(cs_ref, cc_ref, a_hbm, o_hbm, abuf, obuf, asem, osem, col0, compute):
    e = pl.program_id(0)
    c0, nc = cs_ref[e], cc_ref[e]
    ck, tn = abuf.shape[1], obuf.shape[2]

    def rows(c):
        return pl.ds(pl.multiple_of((c0 + c) * ck, ck), ck)

    def a_copy(c, slot):
        return pltpu.make_async_copy(a_hbm.at[rows(c)], abuf.at[slot], asem.at[slot])

    def o_copy(c, slot):
        return pltpu.make_async_copy(obuf.at[slot], o_hbm.at[rows(c), pl.ds(col0, tn)], osem.at[slot])

    @pl.when(nc > 0)
    def _():
        a_copy(0, 0).start(priority=1)

    def body(c, carry):
        slot = c % 2

        @pl.when(c + 1 < nc)
        def _():
            a_copy(c + 1, 1 - slot).start(priority=1)
        a_copy(c, slot).wait()

        @pl.when(c >= 2)
        def _():
            o_copy(c - 2, slot).wait()
        obuf[slot] = compute(abuf[slot]).astype(obuf.dtype)
        o_copy(c, slot).start(priority=1)
        return carry
    lax.fori_loop(0, nc, body, 0)

    @pl.when(nc >= 2)
    def _():
        o_copy(nc - 2, nc % 2).wait()

    @pl.when(nc >= 1)
    def _():
        o_copy(nc - 1, (nc - 1) % 2).wait()

    @pl.when(e == pl.num_programs(0) - 1)
    def _():
        total = o_hbm.shape[0] // ck
        obuf[0] = jnp.zeros(obuf.shape[1:], obuf.dtype)

        def fill(c, carry):
            cp = pltpu.make_async_copy(obuf.at[0], o_hbm.at[pl.ds(pl.multiple_of(c * ck, ck), ck), pl.ds(col0, tn)],
                                       osem.at[0])
            cp.start()
            cp.wait()
            return carry
        lax.fori_loop(c0 + nc, total, fill, 0)


def _moe_up_kernel(cs_ref, cc_ref, a_hbm, wg_ref, wu_ref, o_hbm, wgb_ref, wub_ref, abuf, obuf, asem, osem):
    wgb_ref[...] = wg_ref[0].astype(BF16)
    wub_ref[...] = wu_ref[0].astype(BF16)
    tn = obuf.shape[2]

    def compute(a):
        gate = jnp.dot(a, wgb_ref[...], preferred_element_type=F32)
        up = jnp.dot(a, wub_ref[...], preferred_element_type=F32)
        return _silu(gate) * up
    _expert_rows_loop(cs_ref, cc_ref, a_hbm, o_hbm, abuf, obuf, asem, osem,
                      pl.multiple_of(pl.program_id(1) * tn, tn), compute)


def _moe_down_kernel(cs_ref, cc_ref, a_hbm, w_ref, o_hbm, wb_ref, abuf, obuf, asem, osem):
    wb_ref[...] = w_ref[0].astype(BF16)
    tn = obuf.shape[2]
    _expert_rows_loop(cs_ref, cc_ref, a_hbm, o_hbm, abuf, obuf, asem, osem,
                      pl.multiple_of(pl.program_id(1) * tn, tn),
                      lambda a: jnp.dot(a, wb_ref[...], preferred_element_type=F32))


def _moe_matmul(kernel_fn, a, weights, w_specs, wb_shapes, chunk_start, chunk_count, ck, tn, n_out, out_dtype, name):
    m, k = a.shape
    n_exp = chunk_start.shape[0]
    return pl.pallas_call(
        kernel_fn,
        grid_spec=pltpu.PrefetchScalarGridSpec(
            num_scalar_prefetch=2,
            grid=(n_exp, n_out // tn),
            in_specs=[pl.BlockSpec(memory_space=pl.ANY)] + w_specs,
            out_specs=pl.BlockSpec(memory_space=pl.ANY),
            scratch_shapes=[pltpu.VMEM(s, BF16) for s in wb_shapes]
            + [pltpu.VMEM((2, ck, k), a.dtype), pltpu.VMEM((2, ck, tn), out_dtype),
               pltpu.SemaphoreType.DMA((2,)), pltpu.SemaphoreType.DMA((2,))]),
        out_shape=jax.ShapeDtypeStruct((m, n_out), out_dtype),
        compiler_params=_cparams("arbitrary", "arbitrary"),
        name=name,
    )(chunk_start, chunk_count, a, *weights)


def _moe_up(a, w_up, chunk_start, chunk_count, ck, tn):
    k = a.shape[1]
    f = w_up.shape[2] // 2
    nj = f // tn
    w_specs = [pl.BlockSpec((1, k, tn), lambda e, j, cs, cc: (e, 0, j)),
               pl.BlockSpec((1, k, tn), lambda e, j, cs, cc: (e, 0, nj + j))]
    return _moe_matmul(_moe_up_kernel, a, (w_up, w_up), w_specs, [(k, tn), (k, tn)], chunk_start, chunk_count,
                       ck, tn, f, BF16, "moe_swiglu_up")


def _moe_down(a, w_down, chunk_start, chunk_count, ck, tn):
    f = a.shape[1]
    d = w_down.shape[2]
    w_specs = [pl.BlockSpec((1, f, tn), lambda e, j, cs, cc: (e, 0, j))]
    return _moe_matmul(_moe_down_kernel, a, (w_down,), w_specs, [(f, tn)], chunk_start, chunk_count,
                       ck, tn, d, F32, "moe_down")


def _gather_kernel(src_ref, nv_ref, h_hbm, o_ref, buf, sem):
    i = pl.program_id(0)
    tmg = buf.shape[1]
    nv = nv_ref[0]

    def issue(tile, slot):
        def body(r2, carry):
            for u in range(2):
                r = 2 * r2 + u
                _row_copy(h_hbm, src_ref[tile * tmg + r], buf.at[slot], r, sem.at[slot]).start(priority=u)
            return carry
        lax.fori_loop(0, tmg // 2, body, 0, unroll=4)

    @pl.when((i == 0) & (nv > 0))
    def _():
        issue(0, 0)

    for slot in range(2):
        @pl.when((i + 1 < nv) & ((i + 1) % 2 == slot))
        def _():
            issue(i + 1, slot)

    for slot in range(2):
        @pl.when((i < nv) & (i % 2 == slot))
        def _():
            pltpu.make_async_copy(h_hbm.at[pl.ds(0, tmg)], buf.at[slot], sem.at[slot]).wait()
            o_ref[...] = buf[slot].astype(o_ref.dtype)

    @pl.when(i >= nv)
    def _():
        o_ref[...] = jnp.zeros_like(o_ref)


def _gather_rows(h, src, n_valid, tmg):
    d = h.shape[1]
    n_tiles = src.shape[0] // tmg
    return pl.pallas_call(
        _gather_kernel,
        grid_spec=pltpu.PrefetchScalarGridSpec(
            num_scalar_prefetch=2,
            grid=(n_tiles,),
            in_specs=[pl.BlockSpec(memory_space=pl.ANY)],
            out_specs=pl.BlockSpec((tmg, d), lambda i, src, nv: (i, 0)),
            scratch_shapes=[pltpu.VMEM((2, tmg, d), F32), pltpu.SemaphoreType.DMA((2,))]),
        out_shape=jax.ShapeDtypeStruct((n_tiles * tmg, d), BF16),
        compiler_params=_cparams("arbitrary"),
        name="moe_gather",
    )(src, n_valid, h)


def _mm_kernel(w_is_transposed, a_ref, w_ref, o_ref, wb_ref):
    @pl.when(pl.program_id(1) == 0)
    def _():
        wb_ref[...] = w_ref[...].astype(BF16)
    if w_is_transposed:
        acc = lax.dot_general(a_ref[...], wb_ref[...], (((1,), (1,)), ((), ())), preferred_element_type=F32)
    else:
        acc = jnp.dot(a_ref[...], wb_ref[...], preferred_element_type=F32)
    o_ref[...] = acc.astype(o_ref.dtype)


def _matmul(a, w, n_out, tn, tm=512, out_dtype=F32, name="matmul", w_is_transposed=False, col_block0=0):
    m, k = a.shape
    tm = _tile(m, tm)
    if w_is_transposed:
        w_spec = pl.BlockSpec((tn, k), lambda j, i: (col_block0 + j, 0))
        wb_shape = (tn, k)
    else:
        w_spec = pl.BlockSpec((k, tn), lambda j, i: (0, col_block0 + j))
        wb_shape = (k, tn)
    return pl.pallas_call(
        functools.partial(_mm_kernel, w_is_transposed),
        grid=(n_out // tn, m // tm),
        in_specs=[pl.BlockSpec((tm, k), lambda j, i: (i, 0)), w_spec],
        out_specs=pl.BlockSpec((tm, tn), lambda j, i: (i, j)),
        out_shape=jax.ShapeDtypeStruct((m, n_out), out_dtype),
        scratch_shapes=[pltpu.VMEM(wb_shape, BF16)],
        compiler_params=_cparams("parallel", "arbitrary"),
        name=name,
    )(a, w)


def _new_expert(te_ref):
    i = pl.program_id(1)
    return (i == 0) | (te_ref[i] != te_ref[jnp.maximum(i - 1, 0)])


def _swiglu_up_kernel(te_ref, nv_ref, a_ref, wg_ref, wu_ref, o_ref, wgb_ref, wub_ref):
    @pl.when(_new_expert(te_ref))
    def _():
        wgb_ref[...] = wg_ref[0].astype(BF16)
        wub_ref[...] = wu_ref[0].astype(BF16)

    @pl.when(pl.program_id(1) < nv_ref[0])
    def _():
        a = a_ref[...]
        gate = jnp.dot(a, wgb_ref[...], preferred_element_type=F32)
        up = jnp.dot(a, wub_ref[...], preferred_element_type=F32)
        o_ref[...] = (_silu(gate) * up).astype(o_ref.dtype)

    @pl.when(pl.program_id(1) >= nv_ref[0])
    def _():
        o_ref[...] = jnp.zeros_like(o_ref)


def _swiglu_up(a, w_up, te, nv, tm, tn, name):
    m, k = a.shape
    f = w_up.shape[2] // 2
    nj = f // tn
    row = lambda j, i, te, nv: jnp.minimum(i, nv[0] - 1)
    return pl.pallas_call(
        _swiglu_up_kernel,
        grid_spec=pltpu.PrefetchScalarGridSpec(
            num_scalar_prefetch=2,
            grid=(nj, m // tm),
            in_specs=[pl.BlockSpec((tm, k), lambda j, i, te, nv: (row(j, i, te, nv), 0)),
                      pl.BlockSpec((1, k, tn), lambda j, i, te, nv: (te[i], 0, j)),
                      pl.BlockSpec((1, k, tn), lambda j, i, te, nv: (te[i], 0, nj + j))],
            out_specs=pl.BlockSpec((tm, tn), lambda j, i, te, nv: (i, j)),
            scratch_shapes=[pltpu.VMEM((k, tn), BF16), pltpu.VMEM((k, tn), BF16)]),
        out_shape=jax.ShapeDtypeStruct((m, f), BF16),
        compiler_params=_cparams("parallel", "arbitrary"),
        name=name,
    )(te, nv, a, w_up, w_up)


def _down_kernel(te_ref, nv_ref, a_ref, w_ref, o_ref, wb_ref):
    @pl.when(_new_expert(te_ref))
    def _():
        wb_ref[...] = w_ref[0].astype(BF16)

    @pl.when(pl.program_id(1) < nv_ref[0])
    def _():
        o_ref[...] = jnp.dot(a_ref[...], wb_ref[...], preferred_element_type=F32)

    @pl.when(pl.program_id(1) >= nv_ref[0])
    def _():
        o_ref[...] = jnp.zeros_like(o_ref)


def _grouped_down(a, w_down, te, nv, tm, tn, name):
    m, f = a.shape
    d = w_down.shape[2]
    row = lambda j, i, te, nv: jnp.minimum(i, nv[0] - 1)
    return pl.pallas_call(
        _down_kernel,
        grid_spec=pltpu.PrefetchScalarGridSpec(
            num_scalar_prefetch=2,
            grid=(d // tn, m // tm),
            in_specs=[pl.BlockSpec((tm, f), lambda j, i, te, nv: (row(j, i, te, nv), 0)),
                      pl.BlockSpec((1, f, tn), lambda j, i, te, nv: (te[i], 0, j))],
            out_specs=pl.BlockSpec((tm, tn), lambda j, i, te, nv: (i, j)),
            scratch_shapes=[pltpu.VMEM((f, tn), BF16)]),
        out_shape=jax.ShapeDtypeStruct((m, d), F32),
        compiler_params=_cparams("parallel", "arbitrary"),
        name=name,
    )(te, nv, a, w_down)


def _shift_rows(carry, u, j):
    ext = jnp.concatenate([carry, u], axis=0)
    return pltpu.roll(ext, j, 0)[8:]


def _sconv_prompt_kernel(bg_ref, cg_ref, xi_ref, w_ref, o_ref, st_ref, carry_ref):
    @pl.when(pl.program_id(1) == 0)
    def _():
        carry_ref[...] = jnp.zeros_like(carry_ref)
    u = cg_ref[...] * xi_ref[...]
    carry = carry_ref[...]
    w = w_ref[...]
    y = w[2:3] * u + w[1:2] * _shift_rows(carry, u, 1) + w[0:1] * _shift_rows(carry, u, 2)
    o_ref[...] = (bg_ref[...] * y).astype(o_ref.dtype)
    tail = u[u.shape[0] - 8:]
    carry_ref[...] = tail
    st_ref[0] = tail


def _sconv_prompt(s_in, conv_w, n_seq, t_len, d, tt=256):
    tt = _tile(t_len, tt)
    nt = t_len // tt
    blk = lambda c: pl.BlockSpec((tt, d), lambda b, t: (b * nt + t, c))
    return pl.pallas_call(
        _sconv_prompt_kernel,
        grid=(n_seq, nt),
        in_specs=[blk(0), blk(1), blk(2), pl.BlockSpec((3, d), lambda b, t: (0, 0))],
        out_specs=[pl.BlockSpec((tt, d), lambda b, t: (b * nt + t, 0)),
                   pl.BlockSpec((1, 8, d), lambda b, t: (b, 0, 0))],
        out_shape=[jax.ShapeDtypeStruct((n_seq * t_len, d), BF16), jax.ShapeDtypeStruct((n_seq, 8, d), F32)],
        scratch_shapes=[pltpu.VMEM((8, d), F32)],
        compiler_params=_cparams("parallel", "arbitrary"),
        name="sconv_prompt",
    )(s_in, s_in, s_in, conv_w)


def _sconv_sample_kernel(nb, bg_ref, cg_ref, xi_ref, s0_ref, s1_ref, w_ref, o_ref, u_ref):
    u = cg_ref[...] * xi_ref[...]
    full = jnp.concatenate([s0_ref[...], s1_ref[...], u], axis=0)
    n = u.shape[0]
    w = w_ref[...]
    y = w[0:1] * full[0:n] + w[1:2] * full[nb:nb + n] + w[2:3] * full[2 * nb:2 * nb + n]
    o_ref[...] = (bg_ref[...] * y).astype(o_ref.dtype)
    u_ref[...] = u


def _sconv_sample(s_in, state2d, conv_w, row_block, nb, ns, d, ct=512):
    ct = _tile(d, ct)
    nj = d // ct
    blk = lambda c: pl.BlockSpec((ns, ct), lambda j: (row_block, c * nj + j))
    st = lambda s: pl.BlockSpec((nb, ct), lambda j: (0, s * nj + j))
    return pl.pallas_call(
        functools.partial(_sconv_sample_kernel, nb),
        grid=(nj,),
        in_specs=[blk(0), blk(1), blk(2), st(0), st(1), pl.BlockSpec((3, ct), lambda j: (0, j))],
        out_specs=[pl.BlockSpec((ns, ct), lambda j: (0, j)), pl.BlockSpec((ns, ct), lambda j: (0, j))],
        out_shape=[jax.ShapeDtypeStruct((ns, d), BF16), jax.ShapeDtypeStruct((ns, d), F32)],
        compiler_params=_cparams("parallel"),
        name="sconv_sample",
    )(s_in, s_in, s_in, state2d, state2d, conv_w)


def _l2norm_heads(x):
    parts = []
    for h in range(x.shape[1] // HEAD):
        seg = x[:, h * HEAD:(h + 1) * HEAD]
        parts.append(seg * lax.rsqrt(jnp.sum(seg * seg, axis=-1, keepdims=True) + NORM_EPS))
    return jnp.concatenate(parts, axis=1) if len(parts) > 1 else parts[0]


def _store_qkv(o_ref, act, is_qk):
    @pl.when(is_qk)
    def _():
        o_ref[...] = _l2norm_heads(act)

    @pl.when(jnp.logical_not(is_qk))
    def _():
        o_ref[...] = act


def _gconv_sample_kernel(n_qk_tiles, nb, s0_ref, s1_ref, s2_ref, x_ref, w_ref, o_ref):
    u = x_ref[...]
    full = jnp.concatenate([s0_ref[0], s1_ref[0], s2_ref[0], u], axis=0)
    n = u.shape[0]
    w = w_ref[...]
    y = (w[0:1] * full[0:n] + w[1:2] * full[nb:nb + n] + w[2:3] * full[2 * nb:2 * nb + n]
         + w[3:4] * full[3 * nb:3 * nb + n])
    _store_qkv(o_ref, _silu(y), pl.program_id(0) < n_qk_tiles)


def _gconv_sample(proj, state_tm, conv_w, row_block, nb, ns, d, ct=512):
    ch = 4 * d
    ct = _tile(ch, ct)
    nj = ch // ct
    st = lambda s: pl.BlockSpec((1, nb, ct), lambda j: (s, 0, j))
    return pl.pallas_call(
        functools.partial(_gconv_sample_kernel, 2 * d // ct, nb),
        grid=(nj,),
        in_specs=[st(0), st(1), st(2), pl.BlockSpec((ns, ct), lambda j: (row_block, j)),
                  pl.BlockSpec((4, ct), lambda j: (0, j))],
        out_specs=pl.BlockSpec((ns, ct), lambda j: (0, j)),
        out_shape=jax.ShapeDtypeStruct((ns, ch), F32),
        compiler_params=_cparams("parallel"),
        name="gdn_conv_sample",
    )(state_tm, state_tm, state_tm, proj, conv_w)


def _gates_kernel(chunk, ba_ref, alog_ref, dtb_ref, beta_ref, gc_ref, gtot_ref, gct_ref):
    ba = ba_ref[...]
    nh = ba.shape[1] // 2
    r = ba.shape[0]
    hps = beta_ref.shape[2]

    def store_by_group(ref, x):
        for grp in range(nh // hps):
            ref[grp] = x[:, grp * hps:(grp + 1) * hps]
    store_by_group(beta_ref, jax.nn.sigmoid(ba[:, :nh]))
    g = -jnp.exp(alog_ref[...]) * _softplus(ba[:, nh:] + dtb_ref[...])
    ri = lax.broadcasted_iota(jnp.int32, (r, r), 0)
    ci = lax.broadcasted_iota(jnp.int32, (r, r), 1)
    same = _blk(ri, chunk) == _blk(ci, chunk)
    tri = jnp.where(same & (ri >= ci), 1.0, 0.0)
    blk = jnp.where(same, 1.0, 0.0)
    gc = jnp.dot(tri, g, precision=HIGHEST, preferred_element_type=F32)
    store_by_group(gc_ref, gc)
    store_by_group(gtot_ref, jnp.dot(blk, g, precision=HIGHEST, preferred_element_type=F32))
    eye = jnp.where(lax.broadcasted_iota(jnp.int32, (nh, nh), 0) == lax.broadcasted_iota(jnp.int32, (nh, nh), 1),
                    1.0, 0.0)
    gct_ref[0] = lax.dot_general(eye, gc, (((1,), (1,)), ((), ())), precision=HIGHEST,
                                 preferred_element_type=F32)


def _gates(ba, a_log, dt_bias, chunk, r, hps):
    n, h2 = ba.shape
    nh = h2 // 2
    row = pl.BlockSpec((nh // hps, r, hps), lambda i: (0, i, 0))
    vec = pl.BlockSpec((1, nh), lambda i: (0, 0))
    return pl.pallas_call(
        functools.partial(_gates_kernel, chunk),
        grid=(n // r,),
        in_specs=[pl.BlockSpec((r, h2), lambda i: (i, 0)), vec, vec],
        out_specs=[row, row, row, pl.BlockSpec((1, nh, r), lambda i: (i, 0, 0))],
        out_shape=[jax.ShapeDtypeStruct((nh // hps, n, hps), F32)] * 3 + [jax.ShapeDtypeStruct((n // r, nh, r), F32)],
        compiler_params=_cparams("parallel"),
        name="gdn_gates",
    )(ba, a_log.reshape(1, nh), dt_bias.reshape(1, nh))


def _chunk_terms(chunk, q, k, v, beta, gc, gtot, gc_row):
    r = q.shape[0]
    ri = lax.broadcasted_iota(jnp.int32, (r, r), 0)
    ci = lax.broadcasted_iota(jnp.int32, (r, r), 1)
    causal = (_blk(ri, chunk) == _blk(ci, chunk)) & (ri >= ci)
    decay = jnp.exp(jnp.where(causal, gc - gc_row, -jnp.inf))
    kb = k * beta
    low = jnp.where(ri > ci, _dot_nt(kb, k) * decay, 0.0)
    e = -jnp.where(_blk(ri, 2) == _blk(ci, 2), low, 0.0)
    s = 2
    while s < chunk:
        lm = jnp.where((_blk(ri, 2 * s) == _blk(ci, 2 * s)) & (_blk(ri, s) != _blk(ci, s)), low, 0.0)
        x = lm + _dot(e, lm)
        e = e - (x + _dot(x, e))
        s *= 2
    eg = jnp.exp(gc)
    rhs = jnp.concatenate([v * beta, kb * eg], axis=1)
    uw = rhs + _dot(e, rhs)
    dv = v.shape[1]
    qs = q * (HEAD ** -0.5)
    a_intra = _dot_nt(qs, k) * decay
    return uw[:, :dv], uw[:, dv:], a_intra, qs * eg, k * jnp.exp(gtot - gc)


def _paired_chunk_terms(chunk, heads):
    r, pair = heads[0][0].shape[0], 2 * chunk
    ii = lax.broadcasted_iota(jnp.int32, (chunk, pair), 0)
    lane = lax.broadcasted_iota(jnp.int32, (chunk, pair), 1)
    right = lane >= chunk
    jj = lane & (chunk - 1)
    diag = lambda full: jnp.where(right, full[chunk:], full[:chunk])
    spread = lambda y: jnp.concatenate([jnp.where(right, 0.0, y), jnp.where(right, y, 0.0)], axis=0)
    slices = [slice(p * pair, (p + 1) * pair) for p in range(r // pair)]
    kbs = [k * beta for (q, k, v, beta, gc, gtot, gc_row) in heads]
    egs = [jnp.exp(gc) for (q, k, v, beta, gc, gtot, gc_row) in heads]
    qss = [q * (HEAD ** -0.5) for (q, k, v, beta, gc, gtot, gc_row) in heads]
    rhss = [jnp.concatenate([h[2] * h[3], kb * eg], axis=1) for h, kb, eg in zip(heads, kbs, egs)]
    probs = [(i, sl) for i in range(len(heads)) for sl in slices]
    decays = []
    for i, sl in probs:
        gcp, gc_row = heads[i][4][sl], heads[i][6]
        decays.append(jnp.exp(jnp.where(ii >= jj, jnp.where(right, gcp[chunk:], gcp[:chunk]) - gc_row[:, sl],
                                        -jnp.inf)))
    kks = [_dot_nt(kbs[i][sl], heads[i][1][sl]) for i, sl in probs]
    lows = [jnp.where(ii > jj, diag(kk) * decay, 0.0) for kk, decay in zip(kks, decays)]
    es = [-jnp.where(_blk(ii, 2) == _blk(jj, 2), low, 0.0) for low in lows]
    s = 2
    while s < chunk:
        level = (_blk(ii, 2 * s) == _blk(jj, 2 * s)) & (_blk(ii, s) != _blk(jj, s))
        lms = [jnp.where(level, low, 0.0) for low in lows]
        lm_sp = [spread(lm).astype(BF16) for lm in lms]
        xs = [lm + _dot(e, sp) for lm, e, sp in zip(lms, es, lm_sp)]
        e_sp = [spread(e).astype(BF16) for e in es]
        es = [e - (x + _dot(x, sp)) for e, x, sp in zip(es, xs, e_sp)]
        s *= 2
    e_sp = [spread(e).astype(BF16) for e in es]
    uws = [rhss[i][sl] + _dot(sp, rhss[i][sl]) for (i, sl), sp in zip(probs, e_sp)]
    qks = [_dot_nt(qss[i][sl], heads[i][1][sl]) for i, sl in probs]
    atts = [diag(qk) * decay for qk, decay in zip(qks, decays)]
    out = []
    n_pairs = len(slices)
    for i, (q, k, v, beta, gc, gtot, gc_row) in enumerate(heads):
        uw = jnp.concatenate(uws[i * n_pairs:(i + 1) * n_pairs], axis=0)
        a_chunks = []
        for a in atts[i * n_pairs:(i + 1) * n_pairs]:
            a_chunks += [jnp.where(right, 0.0, a), jnp.where(right, a, 0.0)]
        dv = v.shape[1]
        out.append((uw[:, :dv], uw[:, dv:], a_chunks, qss[i] * egs[i], k * jnp.exp(gtot - gc)))
    return out


def _gated_norm(o, z, gain):
    return (_rms(o, gain) * _silu(z)).astype(BF16)


def _conv_act_head(x_ref, w_ref, carry_ref, h, l2norm):
    sl = slice(h * HEAD, (h + 1) * HEAD)
    u, w, carry = x_ref[:, sl], w_ref[:, sl], carry_ref[:, sl]
    y = (w[3:4] * u + w[2:3] * _shift_rows(carry, u, 1) + w[1:2] * _shift_rows(carry, u, 2)
         + w[0:1] * _shift_rows(carry, u, 3))
    a = _silu(y)
    if l2norm:
        a = a * lax.rsqrt(jnp.sum(a * a, axis=-1, keepdims=True) + NORM_EPS)
    return a


def _gdn_prompt_kernel(chunk, hps, q_ref, k_ref, v_ref, z_ref, wq_ref, wk_ref, wv_ref, beta_ref, gc_ref, gtot_ref,
                       gct_ref, gn_ref, o_ref, s_ref, s_scr, cq_scr, ck_scr, cv_scr):
    @pl.when(pl.program_id(2) == 0)
    def _():
        s_scr[...] = jnp.zeros_like(s_scr)
        cq_scr[...] = jnp.zeros_like(cq_scr)
        ck_scr[...] = jnp.zeros_like(ck_scr)
        cv_scr[...] = jnp.zeros_like(cv_scr)
    r = q_ref.shape[0]
    h0 = pl.program_id(1) * hps
    paired = 2 * chunk == HEAD and r % HEAD == 0
    qs = [_conv_act_head(q_ref, wq_ref, cq_scr, kh, True) for kh in range(hps // 2)]
    ks = [_conv_act_head(k_ref, wk_ref, ck_scr, kh, True) for kh in range(hps // 2)]
    vs = [_conv_act_head(v_ref, wv_ref, cv_scr, hl, False) for hl in range(hps)]
    cq_scr[...] = q_ref[r - 8:r, :]
    ck_scr[...] = k_ref[r - 8:r, :]
    cv_scr[...] = v_ref[r - 8:r, :]
    heads = []
    for hl in range(hps):
        heads.append((qs[hl // 2], ks[hl // 2], vs[hl], beta_ref[0][:, hl:hl + 1],
                      gc_ref[0][:, hl:hl + 1], gtot_ref[0][:, hl:hl + 1],
                      gct_ref[0, pl.ds(h0 + hl, 1), :]))
    if paired:
        terms = _paired_chunk_terms(chunk, heads)
    else:
        terms = [_chunk_terms(chunk, *head) for head in heads]
    terms = [t + (jnp.exp(head[5]),) for t, head in zip(terms, heads)]
    states = [s_scr[hl] for hl in range(hps)]
    outs = [[] for _ in range(hps)]
    for c in range(r // chunk):
        lo, hi = c * chunk, (c + 1) * chunk
        for hl in range(hps):
            u, w, a, qd, kd, egt = terms[hl]
            s = states[hl]
            pq = _dot(jnp.concatenate([w[lo:hi], qd[lo:hi]], axis=0), s)
            v_new = u[lo:hi] - pq[:chunk]
            if paired:
                intra = _dot(a[c], jnp.concatenate([v_new, v_new], axis=0))
            else:
                intra = _dot(a[lo:hi, lo:hi], v_new)
            outs[hl].append(pq[chunk:] + intra)
            states[hl] = s * egt[lo:lo + 1] + _dot_tn(kd[lo:hi], v_new)
    for hl in range(hps):
        s_scr[hl] = states[hl]
        o = jnp.concatenate(outs[hl], axis=0)
        o_ref[:, hl * HEAD:(hl + 1) * HEAD] = _gated_norm(o, z_ref[:, hl * HEAD:(hl + 1) * HEAD], gn_ref[...])
    s_ref[0] = s_scr[...]


def _gdn_prompt(proj, conv_w, beta, gc, gtot, gct, g_onorm, n_seq, t_len, d, chunk, r=256, hps=8):
    nvh = 2 * d // HEAD
    ns = t_len // r
    kc = hps // 2 * HEAD
    vc = hps * HEAD
    rows = lambda b, g, s: b * ns + s
    return pl.pallas_call(
        functools.partial(_gdn_prompt_kernel, chunk, hps),
        grid=(n_seq, nvh // hps, ns),
        in_specs=[pl.BlockSpec((r, kc), lambda b, g, s: (rows(b, g, s), g)),
                  pl.BlockSpec((r, kc), lambda b, g, s: (rows(b, g, s), d // kc + g)),
                  pl.BlockSpec((r, vc), lambda b, g, s: (rows(b, g, s), 2 * d // vc + g)),
                  pl.BlockSpec((r, vc), lambda b, g, s: (rows(b, g, s), 4 * d // vc + g)),
                  pl.BlockSpec((4, kc), lambda b, g, s: (0, g)),
                  pl.BlockSpec((4, kc), lambda b, g, s: (0, d // kc + g)),
                  pl.BlockSpec((4, vc), lambda b, g, s: (0, 2 * d // vc + g)),
                  pl.BlockSpec((1, r, hps), lambda b, g, s: (g, rows(b, g, s), 0)),
                  pl.BlockSpec((1, r, hps), lambda b, g, s: (g, rows(b, g, s), 0)),
                  pl.BlockSpec((1, r, hps), lambda b, g, s: (g, rows(b, g, s), 0)),
                  pl.BlockSpec((1, nvh, r), lambda b, g, s: (rows(b, g, s), 0, 0)),
                  pl.BlockSpec((1, HEAD), lambda b, g, s: (0, 0))],
        out_specs=[pl.BlockSpec((r, vc), lambda b, g, s: (rows(b, g, s), g)),
                   pl.BlockSpec((1, hps, HEAD, HEAD), lambda b, g, s: (b, g, 0, 0))],
        out_shape=[jax.ShapeDtypeStruct((n_seq * t_len, 2 * d), BF16),
                   jax.ShapeDtypeStruct((n_seq, nvh, HEAD, HEAD), F32)],
        scratch_shapes=[pltpu.VMEM((hps, HEAD, HEAD), F32), pltpu.VMEM((8, kc), F32), pltpu.VMEM((8, kc), F32),
                        pltpu.VMEM((8, vc), F32)],
        compiler_params=_cparams("parallel", "parallel", "arbitrary"),
        name="gdn_prompt",
    )(proj, proj, proj, proj, conv_w, conv_w, conv_w, beta, gc, gtot, gct, g_onorm.reshape(1, HEAD))


def _gdn_sample_kernel(chunk, hps, q_ref, k_ref, v_ref, z_ref, beta_ref, gc_ref, gtot_ref, gct_ref, gn_ref, s0_ref,
                       o_ref, s_ref, wq_scr, kd_scr, eg_scr, pq_scr):
    r = q_ref.shape[0]
    h0 = pl.program_id(0) * hps
    per_row = 8 // chunk
    terms = []
    for hl in range(hps):
        kh = hl // 2
        q = q_ref[:, kh * HEAD:(kh + 1) * HEAD]
        k = k_ref[:, kh * HEAD:(kh + 1) * HEAD]
        v = v_ref[:, hl * HEAD:(hl + 1) * HEAD]
        beta = beta_ref[0][:, hl:hl + 1]
        gc = gc_ref[0][:, hl:hl + 1]
        gtot = gtot_ref[0][:, hl:hl + 1]
        gc_row = gct_ref[0, pl.ds(h0 + hl, 1), :]
        u, w, a, qd, kd = _chunk_terms(chunk, q, k, v, beta, gc, gtot, gc_row)
        wq_scr[hl, :, 0:8, :] = w.reshape(r // 8, 8, HEAD)
        wq_scr[hl, :, 8:16, :] = qd.reshape(r // 8, 8, HEAD)
        kd_scr[hl] = kd
        eg_scr[hl] = jnp.broadcast_to(jnp.exp(gtot), (r, HEAD))
        terms.append((u, a))

    row16 = lax.broadcasted_iota(jnp.int32, (16, HEAD), 0)
    row8 = lax.broadcasted_iota(jnp.int32, (8, HEAD), 0)

    def pair_pred(p, carry):
        for hl in range(hps):
            lhs = wq_scr[hl, p]
            acc = jnp.zeros((16, HEAD), F32)
            for e in range(per_row):
                pq = _dot(lhs, s0_ref[p * per_row + e, hl])
                acc = jnp.where(_blk(row16 & 7, chunk) == e, pq, acc)
            pq_scr[hl, p] = acc
        return carry
    lax.fori_loop(0, r // 8, pair_pred, 0)

    v_news = []
    for hl in range(hps):
        u, a = terms[hl]
        pq = pq_scr[hl]
        v_new = u - pq[:, 0:8, :].reshape(r, HEAD)
        o = pq[:, 8:16, :].reshape(r, HEAD) + _dot(a, v_new)
        o_ref[:, hl * HEAD:(hl + 1) * HEAD] = _gated_norm(o, z_ref[:, hl * HEAD:(hl + 1) * HEAD], gn_ref[...])
        pq_scr[hl, :, 0:8, :] = v_new.reshape(r // 8, 8, HEAD)

    def pair_update(p, carry):
        for hl in range(hps):
            v_new = pq_scr[hl, p, 0:8, :]
            kd = kd_scr[hl, pl.ds(pl.multiple_of(p * 8, 8), 8), :]
            for e in range(per_row):
                b = p * per_row + e
                decay = eg_scr[hl, pl.ds(p * 8 + e * chunk, 1), :]
                kd_e = jnp.where(_blk(row8, chunk) == e, kd, 0.0)
                s_ref[b, hl] = s0_ref[b, hl] * decay + _dot_tn(kd_e, v_new)
        return carry
    lax.fori_loop(0, r // 8, pair_update, 0)


def _gdn_sample(qkv, z, beta, gc, gtot, gct, g_onorm, state, d, chunk, r=128, hps=4):
    n = qkv.shape[0]
    nvh = 2 * d // HEAD
    kc = hps // 2 * HEAD
    vc = hps * HEAD
    nb = r // chunk
    return pl.pallas_call(
        functools.partial(_gdn_sample_kernel, chunk, hps),
        grid=(nvh // hps, n // r),
        in_specs=[pl.BlockSpec((r, kc), lambda g, s: (s, g)),
                  pl.BlockSpec((r, kc), lambda g, s: (s, d // kc + g)),
                  pl.BlockSpec((r, vc), lambda g, s: (s, 2 * d // vc + g)),
                  pl.BlockSpec((r, vc), lambda g, s: (s, g)),
                  pl.BlockSpec((1, r, hps), lambda g, s: (g, s, 0)),
                  pl.BlockSpec((1, r, hps), lambda g, s: (g, s, 0)),
                  pl.BlockSpec((1, r, hps), lambda g, s: (g, s, 0)),
                  pl.BlockSpec((1, nvh, r), lambda g, s: (s, 0, 0)),
                  pl.BlockSpec((1, HEAD), lambda g, s: (0, 0)),
                  pl.BlockSpec((nb, hps, HEAD, HEAD), lambda g, s: (s, g, 0, 0))],
        out_specs=[pl.BlockSpec((r, vc), lambda g, s: (s, g)),
                   pl.BlockSpec((nb, hps, HEAD, HEAD), lambda g, s: (s, g, 0, 0))],
        out_shape=[jax.ShapeDtypeStruct((n, 2 * d), BF16),
                   jax.ShapeDtypeStruct(state.shape, F32)],
        scratch_shapes=[pltpu.VMEM((hps, r // 8, 16, HEAD), F32), pltpu.VMEM((hps, r, HEAD), F32),
                        pltpu.VMEM((hps, r, HEAD), F32), pltpu.VMEM((hps, r // 8, 16, HEAD), F32)],
        compiler_params=_cparams("parallel", "parallel"),
        name="gdn_sample",
    )(qkv, qkv, qkv, z, beta, gc, gtot, gct, g_onorm.reshape(1, HEAD), state)


def kernel(x_prompt, x_sample, c_prompt, c_sample, state_gdn, state_gdn_conv, state_sconv, w_ada, b_ada,
           g_norm_mix, g_norm_ffn, g_norm_out, gdn_w_in, gdn_conv_w, gdn_a_log, gdn_dt_bias, gdn_g_onorm,
           gdn_w_out, sc_w_in, sc_conv_w, sc_w_out, ffn_w_up, ffn_w_down, moe_w_router, moe_b_router,
           moe_w_up, moe_w_down):
    nbp, t_len, d = x_prompt.shape
    nbs, steps, _ = x_sample.shape
    n_p, n_s = nbp * t_len, nbs * steps
    nvh = 2 * d // HEAD
    chunk_p = min(64, t_len)
    tm = nbs
    assert t_len % tm == 0 and n_p % n_s == 0 and 8 % steps == 0

    n = n_p + n_s
    tm_mm = _tile(n, 1088)
    x_in = (x_prompt.reshape(n_p, d), x_sample.reshape(nbs, steps * d))
    pad = (-(nbp + nbs)) % 8
    c_all = jnp.concatenate([c_prompt, c_sample, jnp.zeros((pad, d), F32)], axis=0)
    m = _ada(c_all, w_ada, b_ada)
    mods = [_Mod(m[i, :nbp].reshape(nbp, 1, 6 * d), m[i, nbp:nbp + nbs], d, tm, n_p // tm, t_len // tm)
            for i in range(2)]
    SH1, SC1, GA1, SH2, SC2, GA2 = range(6)

    h = _prep(x_in, g_norm_mix[0], mods[0], SC1, SH1, n, tm)
    w_in_t = jnp.swapaxes(gdn_w_in[0], 0, 1)
    proj = _matmul(h, w_in_t, 6 * d, _tile(6 * d, 1024), tm_mm, name="gdn_in_proj", w_is_transposed=True)
    ba = _matmul(h, w_in_t, 2 * nvh, 2 * nvh, tm_mm, name="gdn_in_proj_ba", w_is_transposed=True,
                 col_block0=6 * d // (2 * nvh))
    conv_w = gdn_conv_w[0]
    st_gconv = jnp.transpose(state_gdn_conv[0], (1, 0, 2))
    qkv_s = _gconv_sample(proj, st_gconv, conv_w, n_p // n_s, nbs, n_s, d)
    to_seq_major = lambda a: a.reshape(steps, nbs, a.shape[-1]).transpose(1, 0, 2).reshape(n_s, a.shape[-1])
    qkv_s = to_seq_major(qkv_s)
    z_s = to_seq_major(proj[n_p:, 4 * d:6 * d])
    ba_s = to_seq_major(ba[n_p:])
    hps_p, hps_s = min(8, nvh), min(4, nvh)
    beta_p, gc_p, gtot_p, gct_p = _gates(ba[:n_p], gdn_a_log[0], gdn_dt_bias[0], chunk_p, 256, hps_p)
    beta_s, gc_s, gtot_s, gct_s = _gates(ba_s, gdn_a_log[0], gdn_dt_bias[0], steps, 128, hps_s)
    o_p, gdn_p = _gdn_prompt(proj, conv_w, beta_p, gc_p, gtot_p, gct_p, gdn_g_onorm[0], nbp, t_len, d, chunk_p,
                             hps=hps_p)
    o_s, gdn_s = _gdn_sample(qkv_s, z_s, beta_s, gc_s, gtot_s, gct_s, gdn_g_onorm[0], state_gdn[0], d, steps,
                             hps=hps_s)
    tm_p = _tile(n_p, 1024)
    y = (_matmul(o_p, gdn_w_out[0], d, _tile(d, 512), tm_p, name="gdn_out_proj"),
         _matmul(o_s, gdn_w_out[0], d, _tile(d, 512), n_s, name="gdn_out_proj_sample").reshape(nbs, steps * d))
    gconv_p = jnp.stack([proj[(b + 1) * t_len - 3:(b + 1) * t_len, :4 * d] for b in range(nbp)])
    gconv_s = proj[n_p + nbs:, :4 * d].reshape(3, nbs, 4 * d).transpose(1, 0, 2)

    x, h = _resid_prep(x_in, y, g_norm_ffn[0], mods[0], GA1, mods[0], SC2, SH2, n, tm)
    one_expert = jnp.zeros((n // tm_mm,), jnp.int32)
    all_tiles = jnp.full((1,), n // tm_mm, jnp.int32)
    act = _swiglu_up(h, ffn_w_up, one_expert, all_tiles, tm_mm, _tile(ffn_w_down.shape[1], 512), "ffn_swiglu_up")
    f = _matmul(act, ffn_w_down[0], d, _tile(d, 512), name="ffn_down")

    x, h = _resid_prep(x, f, g_norm_mix[1], mods[0], GA2, mods[1], SC1, SH1, n, tm)
    s_in = _matmul(h, sc_w_in[0], 3 * d, _tile(3 * d, 1024), tm_mm, name="sconv_in_proj")
    mix_p, sconv_tail = _sconv_prompt(s_in, sc_conv_w[0], nbp, t_len, d)
    mix_s, u_s = _sconv_sample(s_in, state_sconv[0].reshape(nbs, 2 * d), sc_conv_w[0], n_p // n_s, nbs, n_s, d)
    y_s = _matmul(mix_s, sc_w_out[0], d, _tile(d, 1024), n_s, name="sconv_out_proj_sample")
    y = (_matmul(mix_p, sc_w_out[0], d, _tile(d, 1024), tm_p, name="sconv_out_proj"),
         y_s.reshape(steps, nbs, d).transpose(1, 0, 2).reshape(nbs, steps * d))
    sconv_p = sconv_tail[:, 6:8]
    sconv_s = u_s[n_s - 2 * nbs:].reshape(2, nbs, d).transpose(1, 0, 2)

    x, h, info = _resid_route(x, y, g_norm_ffn[1], mods[1], GA1, mods[1], SC2, SH2,
                              moe_w_router[0], moe_b_router[0], tm)
    _, n_exp, f_moe, _ = moe_w_down.shape
    ck = 256
    pos, src, _, nv, chunk_start, chunk_count = _route_tables(info, n_exp, ck)
    h_sorted = _gather_rows(h, src, nv, ck)
    act = _moe_up(h_sorted, moe_w_up[0], chunk_start, chunk_count, ck, _tile(f_moe, 1024))
    y_sorted = _moe_down(act, moe_w_down[0], chunk_start, chunk_count, ck, _tile(d, 512))
    out_p, out_s = _final(x, y_sorted, pos, info, g_norm_out, mods[1], GA2, n_p, nbs, tm)

    y_prompt = out_p.reshape(nbp, t_len, d)
    y_sample = out_s.reshape(nbs, steps, d)
    return (y_prompt, y_sample, gdn_p[None], gconv_p[None], sconv_p[None], gdn_s[None], gconv_s[None], sconv_s[None])
```

```python
import functools

import jax
import jax.numpy as jnp
from jax import lax
from jax.experimental import pallas as pl
from jax.experimental.pallas import tpu as pltpu

F32 = jnp.float32
BF16 = jnp.bfloat16
NORM_EPS = 1e-6
HEAD = 128
V7X_VMEM_LIMIT = 56 * 1024 * 1024
HIGHEST = lax.Precision.HIGHEST


def _cparams(*sem):
    return pltpu.CompilerParams(dimension_semantics=sem, vmem_limit_bytes=V7X_VMEM_LIMIT)


def _tile(n, pref):
    return pref if n % pref == 0 else n


def _silu(x):
    return 0.5 * x * (1.0 + jnp.tanh(0.5 * x))


def _softplus(x):
    return jnp.maximum(x, 0.0) + jnp.log1p(jnp.exp(-jnp.abs(x)))


def _blk(idx, size):
    assert size & (size - 1) == 0
    return lax.shift_right_logical(idx, size.bit_length() - 1)


def _dot(a, b):
    return jnp.dot(a.astype(BF16), b.astype(BF16), preferred_element_type=F32)


def _dot_nt(a, b):
    return lax.dot_general(a.astype(BF16), b.astype(BF16), (((1,), (1,)), ((), ())), preferred_element_type=F32)


def _dot_tn(a, b):
    return lax.dot_general(a.astype(BF16), b.astype(BF16), (((0,), (0,)), ((), ())), preferred_element_type=F32)


def _ada_kernel(c_ref, w_ref, b_ref, o_ref):
    c = c_ref[...]
    o_ref[0] = _dot(_silu(c), w_ref[0]) + b_ref[0]


def _ada(c_all, w_ada, b_ada):
    nl, d, n6 = w_ada.shape
    r = c_all.shape[0]
    tn = _tile(n6, 1024)
    return pl.pallas_call(
        _ada_kernel,
        grid=(nl, n6 // tn),
        in_specs=[pl.BlockSpec((r, d), lambda l, j: (0, 0)),
                  pl.BlockSpec((1, d, tn), lambda l, j: (l, 0, j)),
                  pl.BlockSpec((1, 1, tn), lambda l, j: (l, 0, j))],
        out_specs=pl.BlockSpec((1, r, tn), lambda l, j: (l, 0, j)),
        out_shape=jax.ShapeDtypeStruct((nl, r, n6), F32),
        compiler_params=_cparams("parallel", "parallel"),
        name="ada_mod",
    )(c_all, w_ada, b_ada.reshape(nl, 1, n6))


class _Mod:
    def __init__(self, mod_p, mod_s, d, tm, n_ptiles, tiles_per_seq):
        self.arrays = (mod_p, mod_s)
        self.n_ptiles = n_ptiles
        nb = mod_p.shape[0]

        def specs(chunk):
            return [pl.BlockSpec((1, 1, d), lambda i, *_: (jnp.minimum(i // tiles_per_seq, nb - 1), 0, chunk)),
                    pl.BlockSpec((tm, d), lambda i, *_: (0, chunk))]
        self.specs = specs
        self.row_specs = [pl.BlockSpec((tm, d), lambda i, *_: (jnp.minimum(i, n_ptiles - 1), 0)),
                          pl.BlockSpec((tm, d), lambda i, *_: (0, jnp.maximum(i - n_ptiles, 0)))]

    def is_prompt(self):
        return pl.program_id(0) < self.n_ptiles

    def pick(self, p_ref, s_ref):
        return jnp.where(self.is_prompt(), p_ref[0], s_ref[...])

    def pick_rows(self, p_ref, s_ref):
        return jnp.where(self.is_prompt(), p_ref[...], s_ref[...])


def _rms(x, gain):
    ms = jnp.mean(x * x, axis=-1, keepdims=True)
    return x * lax.rsqrt(ms + NORM_EPS) * gain


def _prep_kernel(mod, xp_ref, xs_ref, g_ref, scp, scs, shp, shs, h_ref):
    y = _rms(mod.pick_rows(xp_ref, xs_ref), g_ref[...])
    h_ref[...] = (y * (1.0 + mod.pick(scp, scs)) + mod.pick(shp, shs)).astype(h_ref.dtype)


def _prep(x_pair, gain, mod, sc_chunk, sh_chunk, n, tm):
    d = gain.shape[0]
    return pl.pallas_call(
        functools.partial(_prep_kernel, mod),
        grid=(n // tm,),
        in_specs=mod.row_specs + [pl.BlockSpec((1, d), lambda i: (0, 0))] + mod.specs(sc_chunk) + mod.specs(sh_chunk),
        out_specs=pl.BlockSpec((tm, d), lambda i: (i, 0)),
        out_shape=jax.ShapeDtypeStruct((n, d), BF16),
        compiler_params=_cparams("parallel"),
        name="norm_mod",
    )(*x_pair, gain.reshape(1, d), *mod.arrays, *mod.arrays)


def _row_inputs(mod, a, tm, d):
    if isinstance(a, tuple):
        return mod.row_specs, list(a)
    return [pl.BlockSpec((tm, d), lambda i, *_: (i, 0))], [a]


def _read_rows(mod, refs):
    return mod.pick_rows(*refs) if len(refs) == 2 else refs[0][...]


def _resid_prep_kernel(mod, nx, ny, *refs):
    x_refs, y_refs = refs[:nx], refs[nx:nx + ny]
    g_ref, gap, gas, scp, scs, shp, shs, xo_ref, h_ref = refs[nx + ny:]
    x = _read_rows(mod, x_refs) + mod.pick(gap, gas) * _read_rows(mod, y_refs)
    xo_ref[...] = x
    y = _rms(x, g_ref[...])
    h_ref[...] = (y * (1.0 + mod.pick(scp, scs)) + mod.pick(shp, shs)).astype(h_ref.dtype)


def _resid_prep(x, y, gain, mod_gate, ga_chunk, mod, sc_chunk, sh_chunk, n, tm):
    d = gain.shape[0]
    row = pl.BlockSpec((tm, d), lambda i: (i, 0))
    x_specs, x_ops = _row_inputs(mod, x, tm, d)
    y_specs, y_ops = _row_inputs(mod, y, tm, d)
    return pl.pallas_call(
        functools.partial(_resid_prep_kernel, mod, len(x_ops), len(y_ops)),
        grid=(n // tm,),
        in_specs=x_specs + y_specs + [pl.BlockSpec((1, d), lambda i: (0, 0))]
        + mod_gate.specs(ga_chunk) + mod.specs(sc_chunk) + mod.specs(sh_chunk),
        out_specs=[row, row],
        out_shape=[jax.ShapeDtypeStruct((n, d), F32), jax.ShapeDtypeStruct((n, d), BF16)],
        compiler_params=_cparams("parallel"),
        name="resid_norm_mod",
    )(*x_ops, *y_ops, gain.reshape(1, d), *mod_gate.arrays, *mod.arrays, *mod.arrays)


def _resid_route_kernel(mod, n_exp, ny, x_ref, *refs):
    y_refs = refs[:ny]
    g_ref, gap, gas, scp, scs, shp, shs, wr_ref, br_ref, xo_ref, h_ref, info_ref = refs[ny:]
    x = x_ref[...] + mod.pick(gap, gas) * _read_rows(mod, y_refs)
    xo_ref[...] = x
    h = _rms(x, g_ref[...]) * (1.0 + mod.pick(scp, scs)) + mod.pick(shp, shs)
    h_ref[...] = h
    logits = jnp.dot(h, wr_ref[...], precision=HIGHEST, preferred_element_type=F32) + br_ref[...]
    lane = lax.broadcasted_iota(jnp.int32, logits.shape, 1).astype(F32)
    m1 = jnp.max(logits, axis=-1, keepdims=True)
    i1 = jnp.min(jnp.where(logits == m1, lane, float(n_exp)), axis=-1, keepdims=True)
    rest = jnp.where(lane == i1, -jnp.inf, logits)
    m2 = jnp.max(rest, axis=-1, keepdims=True)
    i2 = jnp.min(jnp.where(rest == m2, lane, float(n_exp)), axis=-1, keepdims=True)
    e2 = jnp.exp(m2 - m1)
    g1 = 1.0 / (1.0 + e2)
    g2 = e2 / (1.0 + e2)
    info_ref[...] = (jnp.where(lane == 0.0, i1, 0.0) + jnp.where(lane == 1.0, i2, 0.0)
                     + jnp.where(lane == 2.0, g1, 0.0) + jnp.where(lane == 3.0, g2, 0.0))


def _resid_route(x, y, gain, mod_gate, ga_chunk, mod, sc_chunk, sh_chunk, w_router, b_router, tm):
    n, d = x.shape
    n_exp = w_router.shape[1]
    row = pl.BlockSpec((tm, d), lambda i: (i, 0))
    y_specs, y_ops = _row_inputs(mod, y, tm, d)
    return pl.pallas_call(
        functools.partial(_resid_route_kernel, mod, n_exp, len(y_ops)),
        grid=(n // tm,),
        in_specs=[row] + y_specs + [pl.BlockSpec((1, d), lambda i: (0, 0))]
        + mod_gate.specs(ga_chunk) + mod.specs(sc_chunk) + mod.specs(sh_chunk)
        + [pl.BlockSpec((d, n_exp), lambda i: (0, 0)), pl.BlockSpec((1, n_exp), lambda i: (0, 0))],
        out_specs=[row, row, pl.BlockSpec((tm, n_exp), lambda i: (i, 0))],
        out_shape=[jax.ShapeDtypeStruct((n, d), F32), jax.ShapeDtypeStruct((n, d), F32),
                   jax.ShapeDtypeStruct((n, n_exp), F32)],
        compiler_params=_cparams("parallel"),
        name="resid_norm_route",
    )(x, *y_ops, gain.reshape(1, d), *mod_gate.arrays, *mod.arrays, *mod.arrays, w_router,
      b_router.reshape(1, n_exp))


def _row_copy(src_hbm, row, dst, r, sem):
    return pltpu.make_async_copy(src_hbm.at[pl.ds(row, 1)], dst.at[pl.ds(r, 1)], sem)


def _final_kernel(mod, pos_ref, x_ref, y_hbm, info_ref, g_ref, gap, gas, op_ref, os_ref, buf, sem):
    tm = x_ref.shape[0]
    i = pl.program_id(0)
    n_tiles = pl.num_programs(0)

    def issue(tile, slot):
        def body(r, carry):
            for k in range(2):
                _row_copy(y_hbm, pos_ref[tile * (2 * tm) + 2 * r + k], buf.at[slot, k], r, sem.at[slot, k]).start()
            return carry
        lax.fori_loop(0, tm, body, 0, unroll=4)

    @pl.when(i == 0)
    def _():
        issue(0, 0)

    for slot in range(2):
        @pl.when((i + 1 < n_tiles) & ((i + 1) % 2 == slot))
        def _():
            issue(i + 1, slot)

    for slot in range(2):
        @pl.when(i % 2 == slot)
        def _():
            for k in range(2):
                pltpu.make_async_copy(y_hbm.at[pl.ds(0, tm)], buf.at[slot, k], sem.at[slot, k]).wait()
            info = info_ref[...]
            f = info[:, 2:3] * buf[slot, 0] + info[:, 3:4] * buf[slot, 1]
            x = x_ref[...] + mod.pick(gap, gas) * f
            out = _rms(x, g_ref[...])

            @pl.when(mod.is_prompt())
            def _():
                op_ref[...] = out

            @pl.when(jnp.logical_not(mod.is_prompt()))
            def _():
                os_ref[...] = out


def _final(x, y_sorted, pos, info, gain, mod_gate, ga_chunk, n_p, nbs, tm):
    n, d = x.shape
    row = pl.BlockSpec((tm, d), lambda i, *_: (i, 0))
    return pl.pallas_call(
        functools.partial(_final_kernel, mod_gate),
        grid_spec=pltpu.PrefetchScalarGridSpec(
            num_scalar_prefetch=1,
            grid=(n // tm,),
            in_specs=[row, pl.BlockSpec(memory_space=pl.ANY),
                      pl.BlockSpec((tm, info.shape[1]), lambda i, *_: (i, 0)),
                      pl.BlockSpec((1, d), lambda i, *_: (0, 0))] + mod_gate.specs(ga_chunk),
            out_specs=mod_gate.row_specs,
            scratch_shapes=[pltpu.VMEM((2, 2, tm, d), F32), pltpu.SemaphoreType.DMA((2, 2))]),
        out_shape=[jax.ShapeDtypeStruct((n_p, d), F32), jax.ShapeDtypeStruct((nbs, (n - n_p) // nbs * d), F32)],
        compiler_params=_cparams("arbitrary"),
        name="moe_combine_final_norm",
    )(pos, x, y_sorted, info, gain.reshape(1, d), *mod_gate.arrays)


def _route_tables(info, n_exp, tmg):
    n = info.shape[0]
    e = info[:, :2].astype(jnp.int32)
    onehot = (e[:, :, None] == jnp.arange(n_exp, dtype=jnp.int32)[None, None, :]).astype(jnp.int32).sum(axis=1)
    counts = onehot.sum(axis=0)
    padded = (counts + tmg - 1) // tmg * tmg
    ends = jnp.cumsum(padded)
    starts = ends - padded
    rank = jnp.cumsum(onehot, axis=0) - onehot
    pos = (starts[e] + jnp.take_along_axis(rank, e, axis=1)).reshape(-1).astype(jnp.int32)
    n_tiles = -(-(2 * n) // tmg) + n_exp
    src = jnp.zeros((n_tiles * tmg,), jnp.int32).at[pos].set(jnp.repeat(jnp.arange(n, dtype=jnp.int32), 2))
    n_valid = (ends[-1] // tmg).astype(jnp.int32)
    tiles = jnp.arange(n_tiles, dtype=jnp.int32)
    te = jnp.minimum((tiles[:, None] * tmg >= ends[None, :]).astype(jnp.int32).sum(axis=1), n_exp - 1)
    te = jnp.where(tiles < n_valid, te, te[jnp.maximum(n_valid - 1, 0)])
    return pos, src, te, n_valid.reshape(1)


def _gather_kernel(src_ref, nv_ref, h_hbm, o_ref, buf, sem):
    i = pl.program_id(0)
    tmg = buf.shape[1]
    nv = nv_ref[0]

    def issue(tile, slot):
        def body(r, carry):
            _row_copy(h_hbm, src_ref[tile * tmg + r], buf.at[slot], r, sem.at[slot]).start()
            return carry
        lax.fori_loop(0, tmg, body, 0, unroll=8)

    @pl.when((i == 0) & (nv > 0))
    def _():
        issue(0, 0)

    for slot in range(2):
        @pl.when((i + 1 < nv) & ((i + 1) % 2 == slot))
        def _():
            issue(i + 1, slot)

    for slot in range(2):
        @pl.when((i < nv) & (i % 2 == slot))
        def _():
            pltpu.make_async_copy(h_hbm.at[pl.ds(0, tmg)], buf.at[slot], sem.at[slot]).wait()
            o_ref[...] = buf[slot].astype(o_ref.dtype)

    @pl.when(i >= nv)
    def _():
        o_ref[...] = jnp.zeros_like(o_ref)


def _gather_rows(h, src, n_valid, tmg):
    d = h.shape[1]
    n_tiles = src.shape[0] // tmg
    return pl.pallas_call(
        _gather_kernel,
        grid_spec=pltpu.PrefetchScalarGridSpec(
            num_scalar_prefetch=2,
            grid=(n_tiles,),
            in_specs=[pl.BlockSpec(memory_space=pl.ANY)],
            out_specs=pl.BlockSpec((tmg, d), lambda i, src, nv: (i, 0)),
            scratch_shapes=[pltpu.VMEM((2, tmg, d), F32), pltpu.SemaphoreType.DMA((2,))]),
        out_shape=jax.ShapeDtypeStruct((n_tiles * tmg, d), BF16),
        compiler_params=_cparams("arbitrary"),
        name="moe_gather",
    )(src, n_valid, h)


def _mm_kernel(w_is_transposed, a_ref, w_ref, o_ref, wb_ref):
    @pl.when(pl.program_id(1) == 0)
    def _():
        wb_ref[...] = w_ref[...].astype(BF16)
    if w_is_transposed:
        acc = lax.dot_general(a_ref[...], wb_ref[...], (((1,), (1,)), ((), ())), preferred_element_type=F32)
    else:
        acc = jnp.dot(a_ref[...], wb_ref[...], preferred_element_type=F32)
    o_ref[...] = acc.astype(o_ref.dtype)


def _matmul(a, w, n_out, tn, tm=512, out_dtype=F32, name="matmul", w_is_transposed=False, col_block0=0):
    m, k = a.shape
    tm = _tile(m, tm)
    if w_is_transposed:
        w_spec = pl.BlockSpec((tn, k), lambda j, i: (col_block0 + j, 0))
        wb_shape = (tn, k)
    else:
        w_spec = pl.BlockSpec((k, tn), lambda j, i: (0, col_block0 + j))
        wb_shape = (k, tn)
    return pl.pallas_call(
        functools.partial(_mm_kernel, w_is_transposed),
        grid=(n_out // tn, m // tm),
        in_specs=[pl.BlockSpec((tm, k), lambda j, i: (i, 0)), w_spec],
        out_specs=pl.BlockSpec((tm, tn), lambda j, i: (i, j)),
        out_shape=jax.ShapeDtypeStruct((m, n_out), out_dtype),
        scratch_shapes=[pltpu.VMEM(wb_shape, BF16)],
        compiler_params=_cparams("parallel", "arbitrary"),
        name=name,
    )(a, w)


def _new_expert(te_ref):
    i = pl.program_id(1)
    return (i == 0) | (te_ref[i] != te_ref[jnp.maximum(i - 1, 0)])


def _swiglu_up_kernel(te_ref, nv_ref, a_ref, wg_ref, wu_ref, o_ref, wgb_ref, wub_ref):
    @pl.when(_new_expert(te_ref))
    def _():
        wgb_ref[...] = wg_ref[0].astype(BF16)
        wub_ref[...] = wu_ref[0].astype(BF16)

    @pl.when(pl.program_id(1) < nv_ref[0])
    def _():
        a = a_ref[...]
        gate = jnp.dot(a, wgb_ref[...], preferred_element_type=F32)
        up = jnp.dot(a, wub_ref[...], preferred_element_type=F32)
        o_ref[...] = (_silu(gate) * up).astype(o_ref.dtype)

    @pl.when(pl.program_id(1) >= nv_ref[0])
    def _():
        o_ref[...] = jnp.zeros_like(o_ref)


def _swiglu_up(a, w_up, te, nv, tm, tn, name):
    m, k = a.shape
    f = w_up.shape[2] // 2
    nj = f // tn
    row = lambda j, i, te, nv: jnp.minimum(i, nv[0] - 1)
    return pl.pallas_call(
        _swiglu_up_kernel,
        grid_spec=pltpu.PrefetchScalarGridSpec(
            num_scalar_prefetch=2,
            grid=(nj, m // tm),
            in_specs=[pl.BlockSpec((tm, k), lambda j, i, te, nv: (row(j, i, te, nv), 0)),
                      pl.BlockSpec((1, k, tn), lambda j, i, te, nv: (te[i], 0, j)),
                      pl.BlockSpec((1, k, tn), lambda j, i, te, nv: (te[i], 0, nj + j))],
            out_specs=pl.BlockSpec((tm, tn), lambda j, i, te, nv: (i, j)),
            scratch_shapes=[pltpu.VMEM((k, tn), BF16), pltpu.VMEM((k, tn), BF16)]),
        out_shape=jax.ShapeDtypeStruct((m, f), BF16),
        compiler_params=_cparams("parallel", "arbitrary"),
        name=name,
    )(te, nv, a, w_up, w_up)


def _down_kernel(te_ref, nv_ref, a_ref, w_ref, o_ref, wb_ref):
    @pl.when(_new_expert(te_ref))
    def _():
        wb_ref[...] = w_ref[0].astype(BF16)

    @pl.when(pl.program_id(1) < nv_ref[0])
    def _():
        o_ref[...] = jnp.dot(a_ref[...], wb_ref[...], preferred_element_type=F32)

    @pl.when(pl.program_id(1) >= nv_ref[0])
    def _():
        o_ref[...] = jnp.zeros_like(o_ref)


def _grouped_down(a, w_down, te, nv, tm, tn, name):
    m, f = a.shape
    d = w_down.shape[2]
    row = lambda j, i, te, nv: jnp.minimum(i, nv[0] - 1)
    return pl.pallas_call(
        _down_kernel,
        grid_spec=pltpu.PrefetchScalarGridSpec(
            num_scalar_prefetch=2,
            grid=(d // tn, m // tm),
            in_specs=[pl.BlockSpec((tm, f), lambda j, i, te, nv: (row(j, i, te, nv), 0)),
                      pl.BlockSpec((1, f, tn), lambda j, i, te, nv: (te[i], 0, j))],
            out_specs=pl.BlockSpec((tm, tn), lambda j, i, te, nv: (i, j)),
            scratch_shapes=[pltpu.VMEM((f, tn), BF16)]),
        out_shape=jax.ShapeDtypeStruct((m, d), F32),
        compiler_params=_cparams("parallel", "arbitrary"),
        name=name,
    )(te, nv, a, w_down)


def _shift_rows(carry, u, j):
    ext = jnp.concatenate([carry, u], axis=0)
    return pltpu.roll(ext, j, 0)[8:]


def _sconv_prompt_kernel(bg_ref, cg_ref, xi_ref, w_ref, o_ref, st_ref, carry_ref):
    @pl.when(pl.program_id(1) == 0)
    def _():
        carry_ref[...] = jnp.zeros_like(carry_ref)
    u = cg_ref[...] * xi_ref[...]
    carry = carry_ref[...]
    w = w_ref[...]
    y = w[2:3] * u + w[1:2] * _shift_rows(carry, u, 1) + w[0:1] * _shift_rows(carry, u, 2)
    o_ref[...] = (bg_ref[...] * y).astype(o_ref.dtype)
    tail = u[u.shape[0] - 8:]
    carry_ref[...] = tail
    st_ref[0] = tail


def _sconv_prompt(s_in, conv_w, n_seq, t_len, d, tt=256):
    tt = _tile(t_len, tt)
    nt = t_len // tt
    blk = lambda c: pl.BlockSpec((tt, d), lambda b, t: (b * nt + t, c))
    return pl.pallas_call(
        _sconv_prompt_kernel,
        grid=(n_seq, nt),
        in_specs=[blk(0), blk(1), blk(2), pl.BlockSpec((3, d), lambda b, t: (0, 0))],
        out_specs=[pl.BlockSpec((tt, d), lambda b, t: (b * nt + t, 0)),
                   pl.BlockSpec((1, 8, d), lambda b, t: (b, 0, 0))],
        out_shape=[jax.ShapeDtypeStruct((n_seq * t_len, d), BF16), jax.ShapeDtypeStruct((n_seq, 8, d), F32)],
        scratch_shapes=[pltpu.VMEM((8, d), F32)],
        compiler_params=_cparams("parallel", "arbitrary"),
        name="sconv_prompt",
    )(s_in, s_in, s_in, conv_w)


def _sconv_sample_kernel(nb, bg_ref, cg_ref, xi_ref, s0_ref, s1_ref, w_ref, o_ref, u_ref):
    u = cg_ref[...] * xi_ref[...]
    full = jnp.concatenate([s0_ref[...], s1_ref[...], u], axis=0)
    n = u.shape[0]
    w = w_ref[...]
    y = w[0:1] * full[0:n] + w[1:2] * full[nb:nb + n] + w[2:3] * full[2 * nb:2 * nb + n]
    o_ref[...] = (bg_ref[...] * y).astype(o_ref.dtype)
    u_ref[...] = u


def _sconv_sample(s_in, state2d, conv_w, row_block, nb, ns, d, ct=512):
    ct = _tile(d, ct)
    nj = d // ct
    blk = lambda c: pl.BlockSpec((ns, ct), lambda j: (row_block, c * nj + j))
    st = lambda s: pl.BlockSpec((nb, ct), lambda j: (0, s * nj + j))
    return pl.pallas_call(
        functools.partial(_sconv_sample_kernel, nb),
        grid=(nj,),
        in_specs=[blk(0), blk(1), blk(2), st(0), st(1), pl.BlockSpec((3, ct), lambda j: (0, j))],
        out_specs=[pl.BlockSpec((ns, ct), lambda j: (0, j)), pl.BlockSpec((ns, ct), lambda j: (0, j))],
        out_shape=[jax.ShapeDtypeStruct((ns, d), BF16), jax.ShapeDtypeStruct((ns, d), F32)],
        compiler_params=_cparams("parallel"),
        name="sconv_sample",
    )(s_in, s_in, s_in, state2d, state2d, conv_w)


def _l2norm_heads(x):
    parts = []
    for h in range(x.shape[1] // HEAD):
        seg = x[:, h * HEAD:(h + 1) * HEAD]
        parts.append(seg * lax.rsqrt(jnp.sum(seg * seg, axis=-1, keepdims=True) + NORM_EPS))
    return jnp.concatenate(parts, axis=1) if len(parts) > 1 else parts[0]


def _store_qkv(o_ref, act, is_qk):
    @pl.when(is_qk)
    def _():
        o_ref[...] = _l2norm_heads(act)

    @pl.when(jnp.logical_not(is_qk))
    def _():
        o_ref[...] = act


def _gconv_sample_kernel(n_qk_tiles, nb, s0_ref, s1_ref, s2_ref, x_ref, w_ref, o_ref):
    u = x_ref[...]
    full = jnp.concatenate([s0_ref[0], s1_ref[0], s2_ref[0], u], axis=0)
    n = u.shape[0]
    w = w_ref[...]
    y = (w[0:1] * full[0:n] + w[1:2] * full[nb:nb + n] + w[2:3] * full[2 * nb:2 * nb + n]
         + w[3:4] * full[3 * nb:3 * nb + n])
    _store_qkv(o_ref, _silu(y), pl.program_id(0) < n_qk_tiles)


def _gconv_sample(proj, state_tm, conv_w, row_block, nb, ns, d, ct=512):
    ch = 4 * d
    ct = _tile(ch, ct)
    nj = ch // ct
    st = lambda s: pl.BlockSpec((1, nb, ct), lambda j: (s, 0, j))
    return pl.pallas_call(
        functools.partial(_gconv_sample_kernel, 2 * d // ct, nb),
        grid=(nj,),
        in_specs=[st(0), st(1), st(2), pl.BlockSpec((ns, ct), lambda j: (row_block, j)),
                  pl.BlockSpec((4, ct), lambda j: (0, j))],
        out_specs=pl.BlockSpec((ns, ct), lambda j: (0, j)),
        out_shape=jax.ShapeDtypeStruct((ns, ch), F32),
        compiler_params=_cparams("parallel"),
        name="gdn_conv_sample",
    )(state_tm, state_tm, state_tm, proj, conv_w)


def _gates_kernel(chunk, ba_ref, alog_ref, dtb_ref, beta_ref, gc_ref, gtot_ref, gct_ref):
    ba = ba_ref[...]
    nh = ba.shape[1] // 2
    r = ba.shape[0]
    hps = beta_ref.shape[2]

    def store_by_group(ref, x):
        for grp in range(nh // hps):
            ref[grp] = x[:, grp * hps:(grp + 1) * hps]
    store_by_group(beta_ref, jax.nn.sigmoid(ba[:, :nh]))
    g = -jnp.exp(alog_ref[...]) * _softplus(ba[:, nh:] + dtb_ref[...])
    ri = lax.broadcasted_iota(jnp.int32, (r, r), 0)
    ci = lax.broadcasted_iota(jnp.int32, (r, r), 1)
    same = _blk(ri, chunk) == _blk(ci, chunk)
    tri = jnp.where(same & (ri >= ci), 1.0, 0.0)
    blk = jnp.where(same, 1.0, 0.0)
    gc = jnp.dot(tri, g, precision=HIGHEST, preferred_element_type=F32)
    store_by_group(gc_ref, gc)
    store_by_group(gtot_ref, jnp.dot(blk, g, precision=HIGHEST, preferred_element_type=F32))
    eye = jnp.where(lax.broadcasted_iota(jnp.int32, (nh, nh), 0) == lax.broadcasted_iota(jnp.int32, (nh, nh), 1),
                    1.0, 0.0)
    gct_ref[0] = lax.dot_general(eye, gc, (((1,), (1,)), ((), ())), precision=HIGHEST,
                                 preferred_element_type=F32)


def _gates(ba, a_log, dt_bias, chunk, r, hps):
    n, h2 = ba.shape
    nh = h2 // 2
    row = pl.BlockSpec((nh // hps, r, hps), lambda i: (0, i, 0))
    vec = pl.BlockSpec((1, nh), lambda i: (0, 0))
    return pl.pallas_call(
        functools.partial(_gates_kernel, chunk),
        grid=(n // r,),
        in_specs=[pl.BlockSpec((r, h2), lambda i: (i, 0)), vec, vec],
        out_specs=[row, row, row, pl.BlockSpec((1, nh, r), lambda i: (i, 0, 0))],
        out_shape=[jax.ShapeDtypeStruct((nh // hps, n, hps), F32)] * 3 + [jax.ShapeDtypeStruct((n // r, nh, r), F32)],
        compiler_params=_cparams("parallel"),
        name="gdn_gates",
    )(ba, a_log.reshape(1, nh), dt_bias.reshape(1, nh))


def _chunk_terms(chunk, q, k, v, beta, gc, gtot, gc_row):
    r = q.shape[0]
    ri = lax.broadcasted_iota(jnp.int32, (r, r), 0)
    ci = lax.broadcasted_iota(jnp.int32, (r, r), 1)
    causal = (_blk(ri, chunk) == _blk(ci, chunk)) & (ri >= ci)
    decay = jnp.exp(jnp.where(causal, gc - gc_row, -jnp.inf))
    kb = k * beta
    low = jnp.where(ri > ci, _dot_nt(kb, k) * decay, 0.0)
    e = -jnp.where(_blk(ri, 2) == _blk(ci, 2), low, 0.0)
    s = 2
    while s < chunk:
        lm = jnp.where((_blk(ri, 2 * s) == _blk(ci, 2 * s)) & (_blk(ri, s) != _blk(ci, s)), low, 0.0)
        x = lm + _dot(e, lm)
        e = e - (x + _dot(x, e))
        s *= 2
    eg = jnp.exp(gc)
    rhs = jnp.concatenate([v * beta, kb * eg], axis=1)
    uw = rhs + _dot(e, rhs)
    dv = v.shape[1]
    qs = q * (HEAD ** -0.5)
    a_intra = _dot_nt(qs, k) * decay
    return uw[:, :dv], uw[:, dv:], a_intra, qs * eg, k * jnp.exp(gtot - gc)


def _paired_chunk_terms(chunk, heads):
    r, pair = heads[0][0].shape[0], 2 * chunk
    ii = lax.broadcasted_iota(jnp.int32, (chunk, pair), 0)
    lane = lax.broadcasted_iota(jnp.int32, (chunk, pair), 1)
    right = lane >= chunk
    jj = lane & (chunk - 1)
    diag = lambda full: jnp.where(right, full[chunk:], full[:chunk])
    spread = lambda y: jnp.concatenate([jnp.where(right, 0.0, y), jnp.where(right, y, 0.0)], axis=0)
    slices = [slice(p * pair, (p + 1) * pair) for p in range(r // pair)]
    kbs = [k * beta for (q, k, v, beta, gc, gtot, gc_row) in heads]
    egs = [jnp.exp(gc) for (q, k, v, beta, gc, gtot, gc_row) in heads]
    qss = [q * (HEAD ** -0.5) for (q, k, v, beta, gc, gtot, gc_row) in heads]
    rhss = [jnp.concatenate([h[2] * h[3], kb * eg], axis=1) for h, kb, eg in zip(heads, kbs, egs)]
    probs = [(i, sl) for i in range(len(heads)) for sl in slices]
    decays = []
    for i, sl in probs:
        gcp, gc_row = heads[i][4][sl], heads[i][6]
        decays.append(jnp.exp(jnp.where(ii >= jj, jnp.where(right, gcp[chunk:], gcp[:chunk]) - gc_row[:, sl],
                                        -jnp.inf)))
    kks = [_dot_nt(kbs[i][sl], heads[i][1][sl]) for i, sl in probs]
    lows = [jnp.where(ii > jj, diag(kk) * decay, 0.0) for kk, decay in zip(kks, decays)]
    es = [-jnp.where(_blk(ii, 2) == _blk(jj, 2), low, 0.0) for low in lows]
    s = 2
    while s < chunk:
        level = (_blk(ii, 2 * s) == _blk(jj, 2 * s)) & (_blk(ii, s) != _blk(jj, s))
        lms = [jnp.where(level, low, 0.0) for low in lows]
        lm_sp = [spread(lm).astype(BF16) for lm in lms]
        xs = [lm + _dot(e, sp) for lm, e, sp in zip(lms, es, lm_sp)]
        e_sp = [spread(e).astype(BF16) for e in es]
        es = [e - (x + _dot(x, sp)) for e, x, sp in zip(es, xs, e_sp)]
        s *= 2
    e_sp = [spread(e).astype(BF16) for e in es]
    uws = [rhss[i][sl] + _dot(sp, rhss[i][sl]) for (i, sl), sp in zip(probs, e_sp)]
    qks = [_dot_nt(qss[i][sl], heads[i][1][sl]) for i, sl in probs]
    atts = [diag(qk) * decay for qk, decay in zip(qks, decays)]
    out = []
    n_pairs = len(slices)
    for i, (q, k, v, beta, gc, gtot, gc_row) in enumerate(heads):
        uw = jnp.concatenate(uws[i * n_pairs:(i + 1) * n_pairs], axis=0)
        a_chunks = []
        for a in atts[i * n_pairs:(i + 1) * n_pairs]:
            a_chunks += [jnp.where(right, 0.0, a), jnp.where(right, a, 0.0)]
        dv = v.shape[1]
        out.append((uw[:, :dv], uw[:, dv:], a_chunks, qss[i] * egs[i], k * jnp.exp(gtot - gc)))
    return out


def _gated_norm(o, z, gain):
    return (_rms(o, gain) * _silu(z)).astype(BF16)


def _conv_act_head(x_ref, w_ref, carry_ref, h, l2norm):
    sl = slice(h * HEAD, (h + 1) * HEAD)
    u, w, carry = x_ref[:, sl], w_ref[:, sl], carry_ref[:, sl]
    y = (w[3:4] * u + w[2:3] * _shift_rows(carry, u, 1) + w[1:2] * _shift_rows(carry, u, 2)
         + w[0:1] * _shift_rows(carry, u, 3))
    a = _silu(y)
    if l2norm:
        a = a * lax.rsqrt(jnp.sum(a * a, axis=-1, keepdims=True) + NORM_EPS)
    return a


def _gdn_prompt_kernel(chunk, hps, q_ref, k_ref, v_ref, z_ref, wq_ref, wk_ref, wv_ref, beta_ref, gc_ref, gtot_ref,
                       gct_ref, gn_ref, o_ref, s_ref, s_scr, cq_scr, ck_scr, cv_scr):
    @pl.when(pl.program_id(2) == 0)
    def _():
        s_scr[...] = jnp.zeros_like(s_scr)
        cq_scr[...] = jnp.zeros_like(cq_scr)
        ck_scr[...] = jnp.zeros_like(ck_scr)
        cv_scr[...] = jnp.zeros_like(cv_scr)
    r = q_ref.shape[0]
    h0 = pl.program_id(1) * hps
    paired = 2 * chunk == HEAD and r % HEAD == 0
    qs = [_conv_act_head(q_ref, wq_ref, cq_scr, kh, True) for kh in range(hps // 2)]
    ks = [_conv_act_head(k_ref, wk_ref, ck_scr, kh, True) for kh in range(hps // 2)]
    vs = [_conv_act_head(v_ref, wv_ref, cv_scr, hl, False) for hl in range(hps)]
    cq_scr[...] = q_ref[r - 8:r, :]
    ck_scr[...] = k_ref[r - 8:r, :]
    cv_scr[...] = v_ref[r - 8:r, :]
    heads = []
    for hl in range(hps):
        heads.append((qs[hl // 2], ks[hl // 2], vs[hl], beta_ref[0][:, hl:hl + 1],
                      gc_ref[0][:, hl:hl + 1], gtot_ref[0][:, hl:hl + 1],
                      gct_ref[0, pl.ds(h0 + hl, 1), :]))
    if paired:
        terms = _paired_chunk_terms(chunk, heads)
    else:
        terms = [_chunk_terms(chunk, *head) for head in heads]
    terms = [t + (jnp.exp(head[5]),) for t, head in zip(terms, heads)]
    states = [s_scr[hl] for hl in range(hps)]
    outs = [[] for _ in range(hps)]
    for c in range(r // chunk):
        lo, hi = c * chunk, (c + 1) * chunk
        for hl in range(hps):
            u, w, a, qd, kd, egt = terms[hl]
            s = states[hl]
            pq = _dot(jnp.concatenate([w[lo:hi], qd[lo:hi]], axis=0), s)
            v_new = u[lo:hi] - pq[:chunk]
            if paired:
                intra = _dot(a[c], jnp.concatenate([v_new, v_new], axis=0))
            else:
                intra = _dot(a[lo:hi, lo:hi], v_new)
            outs[hl].append(pq[chunk:] + intra)
            states[hl] = s * egt[lo:lo + 1] + _dot_tn(kd[lo:hi], v_new)
    for hl in range(hps):
        s_scr[hl] = states[hl]
        o = jnp.concatenate(outs[hl], axis=0)
        o_ref[:, hl * HEAD:(hl + 1) * HEAD] = _gated_norm(o, z_ref[:, hl * HEAD:(hl + 1) * HEAD], gn_ref[...])
    s_ref[0] = s_scr[...]


def _gdn_prompt(proj, conv_w, beta, gc, gtot, gct, g_onorm, n_seq, t_len, d, chunk, r=256, hps=8):
    nvh = 2 * d // HEAD
    ns = t_len // r
    kc = hps // 2 * HEAD
    vc = hps * HEAD
    rows = lambda b, g, s: b * ns + s
    return pl.pallas_call(
        functools.partial(_gdn_prompt_kernel, chunk, hps),
        grid=(n_seq, nvh // hps, ns),
        in_specs=[pl.BlockSpec((r, kc), lambda b, g, s: (rows(b, g, s), g)),
                  pl.BlockSpec((r, kc), lambda b, g, s: (rows(b, g, s), d // kc + g)),
                  pl.BlockSpec((r, vc), lambda b, g, s: (rows(b, g, s), 2 * d // vc + g)),
                  pl.BlockSpec((r, vc), lambda b, g, s: (rows(b, g, s), 4 * d // vc + g)),
                  pl.BlockSpec((4, kc), lambda b, g, s: (0, g)),
                  pl.BlockSpec((4, kc), lambda b, g, s: (0, d // kc + g)),
                  pl.BlockSpec((4, vc), lambda b, g, s: (0, 2 * d // vc + g)),
                  pl.BlockSpec((1, r, hps), lambda b, g, s: (g, rows(b, g, s), 0)),
                  pl.BlockSpec((1, r, hps), lambda b, g, s: (g, rows(b, g, s), 0)),
                  pl.BlockSpec((1, r, hps), lambda b, g, s: (g, rows(b, g, s), 0)),
                  pl.BlockSpec((1, nvh, r), lambda b, g, s: (rows(b, g, s), 0, 0)),
                  pl.BlockSpec((1, HEAD), lambda b, g, s: (0, 0))],
        out_specs=[pl.BlockSpec((r, vc), lambda b, g, s: (rows(b, g, s), g)),
                   pl.BlockSpec((1, hps, HEAD, HEAD), lambda b, g, s: (b, g, 0, 0))],
        out_shape=[jax.ShapeDtypeStruct((n_seq * t_len, 2 * d), BF16),
                   jax.ShapeDtypeStruct((n_seq, nvh, HEAD, HEAD), F32)],
        scratch_shapes=[pltpu.VMEM((hps, HEAD, HEAD), F32), pltpu.VMEM((8, kc), F32), pltpu.VMEM((8, kc), F32),
                        pltpu.VMEM((8, vc), F32)],
        compiler_params=_cparams("parallel", "parallel", "arbitrary"),
        name="gdn_prompt",
    )(proj, proj, proj, proj, conv_w, conv_w, conv_w, beta, gc, gtot, gct, g_onorm.reshape(1, HEAD))


def _gdn_sample_kernel(chunk, hps, q_ref, k_ref, v_ref, z_ref, beta_ref, gc_ref, gtot_ref, gct_ref, gn_ref, s0_ref,
                       o_ref, s_ref, wq_scr, kd_scr, eg_scr, pq_scr):
    r = q_ref.shape[0]
    h0 = pl.program_id(0) * hps
    per_row = 8 // chunk
    terms = []
    for hl in range(hps):
        kh = hl // 2
        q = q_ref[:, kh * HEAD:(kh + 1) * HEAD]
        k = k_ref[:, kh * HEAD:(kh + 1) * HEAD]
        v = v_ref[:, hl * HEAD:(hl + 1) * HEAD]
        beta = beta_ref[0][:, hl:hl + 1]
        gc = gc_ref[0][:, hl:hl + 1]
        gtot = gtot_ref[0][:, hl:hl + 1]
        gc_row = gct_ref[0, pl.ds(h0 + hl, 1), :]
        u, w, a, qd, kd = _chunk_terms(chunk, q, k, v, beta, gc, gtot, gc_row)
        wq_scr[hl, :, 0:8, :] = w.reshape(r // 8, 8, HEAD)
        wq_scr[hl, :, 8:16, :] = qd.reshape(r // 8, 8, HEAD)
        kd_scr[hl] = kd
        eg_scr[hl] = jnp.broadcast_to(jnp.exp(gtot), (r, HEAD))
        terms.append((u, a))

    row16 = lax.broadcasted_iota(jnp.int32, (16, HEAD), 0)
    row8 = lax.broadcasted_iota(jnp.int32, (8, HEAD), 0)

    def pair_pred(p, carry):
        for hl in range(hps):
            lhs = wq_scr[hl, p]
            acc = jnp.zeros((16, HEAD), F32)
            for e in range(per_row):
                pq = _dot(lhs, s0_ref[p * per_row + e, hl])
                acc = jnp.where(_blk(row16 & 7, chunk) == e, pq, acc)
            pq_scr[hl, p] = acc
        return carry
    lax.fori_loop(0, r // 8, pair_pred, 0)

    v_news = []
    for hl in range(hps):
        u, a = terms[hl]
        pq = pq_scr[hl]
        v_new = u - pq[:, 0:8, :].reshape(r, HEAD)
        o = pq[:, 8:16, :].reshape(r, HEAD) + _dot(a, v_new)
        o_ref[:, hl * HEAD:(hl + 1) * HEAD] = _gated_norm(o, z_ref[:, hl * HEAD:(hl + 1) * HEAD], gn_ref[...])
        pq_scr[hl, :, 0:8, :] = v_new.reshape(r // 8, 8, HEAD)

    def pair_update(p, carry):
        for hl in range(hps):
            v_new = pq_scr[hl, p, 0:8, :]
            kd = kd_scr[hl, pl.ds(pl.multiple_of(p * 8, 8), 8), :]
            for e in range(per_row):
                b = p * per_row + e
                decay = eg_scr[hl, pl.ds(p * 8 + e * chunk, 1), :]
                kd_e = jnp.where(_blk(row8, chunk) == e, kd, 0.0)
                s_ref[b, hl] = s0_ref[b, hl] * decay + _dot_tn(kd_e, v_new)
        return carry
    lax.fori_loop(0, r // 8, pair_update, 0)


def _gdn_sample(qkv, z, beta, gc, gtot, gct, g_onorm, state, d, chunk, r=128, hps=4):
    n = qkv.shape[0]
    nvh = 2 * d // HEAD
    kc = hps // 2 * HEAD
    vc = hps * HEAD
    nb = r // chunk
    return pl.pallas_call(
        functools.partial(_gdn_sample_kernel, chunk, hps),
        grid=(nvh // hps, n // r),
        in_specs=[pl.BlockSpec((r, kc), lambda g, s: (s, g)),
                  pl.BlockSpec((r, kc), lambda g, s: (s, d // kc + g)),
                  pl.BlockSpec((r, vc), lambda g, s: (s, 2 * d // vc + g)),
                  pl.BlockSpec((r, vc), lambda g, s: (s, g)),
                  pl.BlockSpec((1, r, hps), lambda g, s: (g, s, 0)),
                  pl.BlockSpec((1, r, hps), lambda g, s: (g, s, 0)),
                  pl.BlockSpec((1, r, hps), lambda g, s: (g, s, 0)),
                  pl.BlockSpec((1, nvh, r), lambda g, s: (s, 0, 0)),
                  pl.BlockSpec((1, HEAD), lambda g, s: (0, 0)),
                  pl.BlockSpec((nb, hps, HEAD, HEAD), lambda g, s: (s, g, 0, 0))],
        out_specs=[pl.BlockSpec((r, vc), lambda g, s: (s, g)),
                   pl.BlockSpec((nb, hps, HEAD, HEAD), lambda g, s: (s, g, 0, 0))],
        out_shape=[jax.ShapeDtypeStruct((n, 2 * d), BF16),
                   jax.ShapeDtypeStruct(state.shape, F32)],
        scratch_shapes=[pltpu.VMEM((hps, r // 8, 16, HEAD), F32), pltpu.VMEM((hps, r, HEAD), F32),
                        pltpu.VMEM((hps, r, HEAD), F32), pltpu.VMEM((hps, r // 8, 16, HEAD), F32)],
        compiler_params=_cparams("parallel", "parallel"),
        name="gdn_sample",
    )(qkv, qkv, qkv, z, beta, gc, gtot, gct, g_onorm.reshape(1, HEAD), state)


def kernel(x_prompt, x_sample, c_prompt, c_sample, state_gdn, state_gdn_conv, state_sconv, w_ada, b_ada,
           g_norm_mix, g_norm_ffn, g_norm_out, gdn_w_in, gdn_conv_w, gdn_a_log, gdn_dt_bias, gdn_g_onorm,
           gdn_w_out, sc_w_in, sc_conv_w, sc_w_out, ffn_w_up, ffn_w_down, moe_w_router, moe_b_router,
           moe_w_up, moe_w_down):
    nbp, t_len, d = x_prompt.shape
    nbs, steps, _ = x_sample.shape
    n_p, n_s = nbp * t_len, nbs * steps
    nvh = 2 * d // HEAD
    chunk_p = min(64, t_len)
    tm = nbs
    assert t_len % tm == 0 and n_p % n_s == 0 and 8 % steps == 0

    n = n_p + n_s
    tm_mm = _tile(n, 1088)
    x_in = (x_prompt.reshape(n_p, d), x_sample.reshape(nbs, steps * d))
    pad = (-(nbp + nbs)) % 8
    c_all = jnp.concatenate([c_prompt, c_sample, jnp.zeros((pad, d), F32)], axis=0)
    m = _ada(c_all, w_ada, b_ada)
    mods = [_Mod(m[i, :nbp].reshape(nbp, 1, 6 * d), m[i, nbp:nbp + nbs], d, tm, n_p // tm, t_len // tm)
            for i in range(2)]
    SH1, SC1, GA1, SH2, SC2, GA2 = range(6)

    h = _prep(x_in, g_norm_mix[0], mods[0], SC1, SH1, n, tm)
    w_in_t = jnp.swapaxes(gdn_w_in[0], 0, 1)
    proj = _matmul(h, w_in_t, 6 * d, _tile(6 * d, 1024), tm_mm, name="gdn_in_proj", w_is_transposed=True)
    ba = _matmul(h, w_in_t, 2 * nvh, 2 * nvh, tm_mm, name="gdn_in_proj_ba", w_is_transposed=True,
                 col_block0=6 * d // (2 * nvh))
    conv_w = gdn_conv_w[0]
    st_gconv = jnp.transpose(state_gdn_conv[0], (1, 0, 2))
    qkv_s = _gconv_sample(proj, st_gconv, conv_w, n_p // n_s, nbs, n_s, d)
    to_seq_major = lambda a: a.reshape(steps, nbs, a.shape[-1]).transpose(1, 0, 2).reshape(n_s, a.shape[-1])
    qkv_s = to_seq_major(qkv_s)
    z_s = to_seq_major(proj[n_p:, 4 * d:6 * d])
    ba_s = to_seq_major(ba[n_p:])
    hps_p, hps_s = min(8, nvh), min(4, nvh)
    beta_p, gc_p, gtot_p, gct_p = _gates(ba[:n_p], gdn_a_log[0], gdn_dt_bias[0], chunk_p, 256, hps_p)
    beta_s, gc_s, gtot_s, gct_s = _gates(ba_s, gdn_a_log[0], gdn_dt_bias[0], steps, 128, hps_s)
    o_p, gdn_p = _gdn_prompt(proj, conv_w, beta_p, gc_p, gtot_p, gct_p, gdn_g_onorm[0], nbp, t_len, d, chunk_p,
                             hps=hps_p)
    o_s, gdn_s = _gdn_sample(qkv_s, z_s, beta_s, gc_s, gtot_s, gct_s, gdn_g_onorm[0], state_gdn[0], d, steps,
                             hps=hps_s)
    tm_p = _tile(n_p, 1024)
    y = (_matmul(o_p, gdn_w_out[0], d, _tile(d, 512), tm_p, name="gdn_out_proj"),
         _matmul(o_s, gdn_w_out[0], d, _tile(d, 512), n_s, name="gdn_out_proj_sample").reshape(nbs, steps * d))
    gconv_p = jnp.stack([proj[(b + 1) * t_len - 3:(b + 1) * t_len, :4 * d] for b in range(nbp)])
    gconv_s = proj[n_p + nbs:, :4 * d].reshape(3, nbs, 4 * d).transpose(1, 0, 2)

    x, h = _resid_prep(x_in, y, g_norm_ffn[0], mods[0], GA1, mods[0], SC2, SH2, n, tm)
    one_expert = jnp.zeros((n // tm_mm,), jnp.int32)
    all_tiles = jnp.full((1,), n // tm_mm, jnp.int32)
    act = _swiglu_up(h, ffn_w_up, one_expert, all_tiles, tm_mm, _tile(ffn_w_down.shape[1], 512), "ffn_swiglu_up")
    f = _matmul(act, ffn_w_down[0], d, _tile(d, 512), name="ffn_down")

    x, h = _resid_prep(x, f, g_norm_mix[1], mods[0], GA2, mods[1], SC1, SH1, n, tm)
    s_in = _matmul(h, sc_w_in[0], 3 * d, _tile(3 * d, 1024), tm_mm, name="sconv_in_proj")
    mix_p, sconv_tail = _sconv_prompt(s_in, sc_conv_w[0], nbp, t_len, d)
    mix_s, u_s = _sconv_sample(s_in, state_sconv[0].reshape(nbs, 2 * d), sc_conv_w[0], n_p // n_s, nbs, n_s, d)
    y_s = _matmul(mix_s, sc_w_out[0], d, _tile(d, 1024), n_s, name="sconv_out_proj_sample")
    y = (_matmul(mix_p, sc_w_out[0], d, _tile(d, 1024), tm_p, name="sconv_out_proj"),
         y_s.reshape(steps, nbs, d).transpose(1, 0, 2).reshape(nbs, steps * d))
    sconv_p = sconv_tail[:, 6:8]
    sconv_s = u_s[n_s - 2 * nbs:].reshape(2, nbs, d).transpose(1, 0, 2)

    x, h, info = _resid_route(x, y, g_norm_ffn[1], mods[1], GA1, mods[1], SC2, SH2,
                              moe_w_router[0], moe_b_router[0], tm)
    _, n_exp, f_moe, _ = moe_w_down.shape
    tmg = 512
    pos, src, te, nv = _route_tables(info, n_exp, tmg)
    tg = _tile(tmg, 256)
    h_sorted = _gather_rows(h, src, nv * (tmg // tg), tg)
    act = _swiglu_up(h_sorted, moe_w_up[0], te, nv, tmg, _tile(f_moe, 1024), "moe_swiglu_up")
    y_sorted = _grouped_down(act, moe_w_down[0], te, nv, tmg, _tile(d, 512), "moe_down")
    out_p, out_s = _final(x, y_sorted, pos, info, g_norm_out, mods[1], GA2, n_p, nbs, tm)

    y_prompt = out_p.reshape(nbp, t_len, d)
    y_sample = out_s.reshape(nbs, steps, d)
    return (y_prompt, y_sample, gdn_p[None], gconv_p[None], sconv_p[None], gdn_s[None], gconv_s[None], sconv_s[None])
```

```python
import functools

import jax
import jax.numpy as jnp
from jax import lax
from jax.experimental import pallas as pl
from jax.experimental.pallas import tpu as pltpu

F32 = jnp.float32
BF16 = jnp.bfloat16
NORM_EPS = 1e-6
HEAD = 128
V7X_VMEM_LIMIT = 56 * 1024 * 1024
HIGHEST = lax.Precision.HIGHEST


def _cparams(*sem):
    return pltpu.CompilerParams(dimension_semantics=sem, vmem_limit_bytes=V7X_VMEM_LIMIT)


def _tile(n, pref):
    return pref if n % pref == 0 else n


def _silu(x):
    return 0.5 * x * (1.0 + jnp.tanh(0.5 * x))


def _softplus(x):
    return jnp.maximum(x, 0.0) + jnp.log1p(jnp.exp(-jnp.abs(x)))


def _blk(idx, size):
    assert size & (size - 1) == 0
    return lax.shift_right_logical(idx, size.bit_length() - 1)


def _dot(a, b):
    return jnp.dot(a.astype(BF16), b.astype(BF16), preferred_element_type=F32)


def _dot_nt(a, b):
    return lax.dot_general(a.astype(BF16), b.astype(BF16), (((1,), (1,)), ((), ())), preferred_element_type=F32)


def _dot_tn(a, b):
    return lax.dot_general(a.astype(BF16), b.astype(BF16), (((0,), (0,)), ((), ())), preferred_element_type=F32)


def _ada_kernel(c_ref, w_ref, b_ref, o_ref):
    c = c_ref[...]
    o_ref[0] = _dot(_silu(c), w_ref[0]) + b_ref[0]


def _ada(c_all, w_ada, b_ada):
    nl, d, n6 = w_ada.shape
    r = c_all.shape[0]
    tn = _tile(n6, 1024)
    return pl.pallas_call(
        _ada_kernel,
        grid=(nl, n6 // tn),
        in_specs=[pl.BlockSpec((r, d), lambda l, j: (0, 0)),
                  pl.BlockSpec((1, d, tn), lambda l, j: (l, 0, j)),
                  pl.BlockSpec((1, 1, tn), lambda l, j: (l, 0, j))],
        out_specs=pl.BlockSpec((1, r, tn), lambda l, j: (l, 0, j)),
        out_shape=jax.ShapeDtypeStruct((nl, r, n6), F32),
        compiler_params=_cparams("parallel", "parallel"),
        name="ada_mod",
    )(c_all, w_ada, b_ada.reshape(nl, 1, n6))


class _Mod:
    def __init__(self, mod_p, mod_s, d, tm, n_ptiles, tiles_per_seq):
        self.arrays = (mod_p, mod_s)
        self.n_ptiles = n_ptiles
        nb = mod_p.shape[0]

        def specs(chunk):
            return [pl.BlockSpec((1, 1, d), lambda i, *_: (jnp.minimum(i // tiles_per_seq, nb - 1), 0, chunk)),
                    pl.BlockSpec((tm, d), lambda i, *_: (0, chunk))]
        self.specs = specs
        self.row_specs = [pl.BlockSpec((tm, d), lambda i, *_: (jnp.minimum(i, n_ptiles - 1), 0)),
                          pl.BlockSpec((tm, d), lambda i, *_: (0, jnp.maximum(i - n_ptiles, 0)))]

    def is_prompt(self):
        return pl.program_id(0) < self.n_ptiles

    def pick(self, p_ref, s_ref):
        return jnp.where(self.is_prompt(), p_ref[0], s_ref[...])

    def pick_rows(self, p_ref, s_ref):
        return jnp.where(self.is_prompt(), p_ref[...], s_ref[...])


def _rms(x, gain):
    ms = jnp.mean(x * x, axis=-1, keepdims=True)
    return x * lax.rsqrt(ms + NORM_EPS) * gain


def _prep_kernel(mod, xp_ref, xs_ref, g_ref, scp, scs, shp, shs, h_ref):
    y = _rms(mod.pick_rows(xp_ref, xs_ref), g_ref[...])
    h_ref[...] = (y * (1.0 + mod.pick(scp, scs)) + mod.pick(shp, shs)).astype(h_ref.dtype)


def _prep(x_pair, gain, mod, sc_chunk, sh_chunk, n, tm):
    d = gain.shape[0]
    return pl.pallas_call(
        functools.partial(_prep_kernel, mod),
        grid=(n // tm,),
        in_specs=mod.row_specs + [pl.BlockSpec((1, d), lambda i: (0, 0))] + mod.specs(sc_chunk) + mod.specs(sh_chunk),
        out_specs=pl.BlockSpec((tm, d), lambda i: (i, 0)),
        out_shape=jax.ShapeDtypeStruct((n, d), BF16),
        compiler_params=_cparams("parallel"),
        name="norm_mod",
    )(*x_pair, gain.reshape(1, d), *mod.arrays, *mod.arrays)


def _row_inputs(mod, a, tm, d):
    if isinstance(a, tuple):
        return mod.row_specs, list(a)
    return [pl.BlockSpec((tm, d), lambda i, *_: (i, 0))], [a]


def _read_rows(mod, refs):
    return mod.pick_rows(*refs) if len(refs) == 2 else refs[0][...]


def _resid_prep_kernel(mod, nx, ny, *refs):
    x_refs, y_refs = refs[:nx], refs[nx:nx + ny]
    g_ref, gap, gas, scp, scs, shp, shs, xo_ref, h_ref = refs[nx + ny:]
    x = _read_rows(mod, x_refs) + mod.pick(gap, gas) * _read_rows(mod, y_refs)
    xo_ref[...] = x
    y = _rms(x, g_ref[...])
    h_ref[...] = (y * (1.0 + mod.pick(scp, scs)) + mod.pick(shp, shs)).astype(h_ref.dtype)


def _resid_prep(x, y, gain, mod_gate, ga_chunk, mod, sc_chunk, sh_chunk, n, tm):
    d = gain.shape[0]
    row = pl.BlockSpec((tm, d), lambda i: (i, 0))
    x_specs, x_ops = _row_inputs(mod, x, tm, d)
    y_specs, y_ops = _row_inputs(mod, y, tm, d)
    return pl.pallas_call(
        functools.partial(_resid_prep_kernel, mod, len(x_ops), len(y_ops)),
        grid=(n // tm,),
        in_specs=x_specs + y_specs + [pl.BlockSpec((1, d), lambda i: (0, 0))]
        + mod_gate.specs(ga_chunk) + mod.specs(sc_chunk) + mod.specs(sh_chunk),
        out_specs=[row, row],
        out_shape=[jax.ShapeDtypeStruct((n, d), F32), jax.ShapeDtypeStruct((n, d), BF16)],
        compiler_params=_cparams("parallel"),
        name="resid_norm_mod",
    )(*x_ops, *y_ops, gain.reshape(1, d), *mod_gate.arrays, *mod.arrays, *mod.arrays)


def _resid_route_kernel(mod, n_exp, ny, x_ref, *refs):
    y_refs = refs[:ny]
    g_ref, gap, gas, scp, scs, shp, shs, wr_ref, br_ref, xo_ref, h_ref, info_ref = refs[ny:]
    x = x_ref[...] + mod.pick(gap, gas) * _read_rows(mod, y_refs)
    xo_ref[...] = x
    h = _rms(x, g_ref[...]) * (1.0 + mod.pick(scp, scs)) + mod.pick(shp, shs)
    h_ref[...] = h
    logits = jnp.dot(h, wr_ref[...], precision=HIGHEST, preferred_element_type=F32) + br_ref[...]
    lane = lax.broadcasted_iota(jnp.int32, logits.shape, 1).astype(F32)
    m1 = jnp.max(logits, axis=-1, keepdims=True)
    i1 = jnp.min(jnp.where(logits == m1, lane, float(n_exp)), axis=-1, keepdims=True)
    rest = jnp.where(lane == i1, -jnp.inf, logits)
    m2 = jnp.max(rest, axis=-1, keepdims=True)
    i2 = jnp.min(jnp.where(rest == m2, lane, float(n_exp)), axis=-1, keepdims=True)
    e2 = jnp.exp(m2 - m1)
    g1 = 1.0 / (1.0 + e2)
    g2 = e2 / (1.0 + e2)
    info_ref[...] = (jnp.where(lane == 0.0, i1, 0.0) + jnp.where(lane == 1.0, i2, 0.0)
                     + jnp.where(lane == 2.0, g1, 0.0) + jnp.where(lane == 3.0, g2, 0.0))


def _resid_route(x, y, gain, mod_gate, ga_chunk, mod, sc_chunk, sh_chunk, w_router, b_router, tm):
    n, d = x.shape
    n_exp = w_router.shape[1]
    row = pl.BlockSpec((tm, d), lambda i: (i, 0))
    y_specs, y_ops = _row_inputs(mod, y, tm, d)
    return pl.pallas_call(
        functools.partial(_resid_route_kernel, mod, n_exp, len(y_ops)),
        grid=(n // tm,),
        in_specs=[row] + y_specs + [pl.BlockSpec((1, d), lambda i: (0, 0))]
        + mod_gate.specs(ga_chunk) + mod.specs(sc_chunk) + mod.specs(sh_chunk)
        + [pl.BlockSpec((d, n_exp), lambda i: (0, 0)), pl.BlockSpec((1, n_exp), lambda i: (0, 0))],
        out_specs=[row, row, pl.BlockSpec((tm, n_exp), lambda i: (i, 0))],
        out_shape=[jax.ShapeDtypeStruct((n, d), F32), jax.ShapeDtypeStruct((n, d), F32),
                   jax.ShapeDtypeStruct((n, n_exp), F32)],
        compiler_params=_cparams("parallel"),
        name="resid_norm_route",
    )(x, *y_ops, gain.reshape(1, d), *mod_gate.arrays, *mod.arrays, *mod.arrays, w_router,
      b_router.reshape(1, n_exp))


def _row_copy(src_hbm, row, dst, r, sem):
    return pltpu.make_async_copy(src_hbm.at[pl.ds(row, 1)], dst.at[pl.ds(r, 1)], sem)


def _final_kernel(mod, pos_ref, x_ref, y_hbm, info_ref, g_ref, gap, gas, op_ref, os_ref, buf, sem):
    tm = x_ref.shape[0]
    i = pl.program_id(0)
    n_tiles = pl.num_programs(0)

    def issue(tile, slot):
        def body(r, carry):
            for k in range(2):
                _row_copy(y_hbm, pos_ref[tile * (2 * tm) + 2 * r + k], buf.at[slot, k], r, sem.at[slot, k]).start()
            return carry
        lax.fori_loop(0, tm, body, 0, unroll=4)

    @pl.when(i == 0)
    def _():
        issue(0, 0)

    for slot in range(2):
        @pl.when((i + 1 < n_tiles) & ((i + 1) % 2 == slot))
        def _():
            issue(i + 1, slot)

    for slot in range(2):
        @pl.when(i % 2 == slot)
        def _():
            for k in range(2):
                pltpu.make_async_copy(y_hbm.at[pl.ds(0, tm)], buf.at[slot, k], sem.at[slot, k]).wait()
            info = info_ref[...]
            f = info[:, 2:3] * buf[slot, 0] + info[:, 3:4] * buf[slot, 1]
            x = x_ref[...] + mod.pick(gap, gas) * f
            out = _rms(x, g_ref[...])

            @pl.when(mod.is_prompt())
            def _():
                op_ref[...] = out

            @pl.when(jnp.logical_not(mod.is_prompt()))
            def _():
                os_ref[...] = out


def _final(x, y_sorted, pos, info, gain, mod_gate, ga_chunk, n_p, nbs, tm):
    n, d = x.shape
    row = pl.BlockSpec((tm, d), lambda i, *_: (i, 0))
    return pl.pallas_call(
        functools.partial(_final_kernel, mod_gate),
        grid_spec=pltpu.PrefetchScalarGridSpec(
            num_scalar_prefetch=1,
            grid=(n // tm,),
            in_specs=[row, pl.BlockSpec(memory_space=pl.ANY),
                      pl.BlockSpec((tm, info.shape[1]), lambda i, *_: (i, 0)),
                      pl.BlockSpec((1, d), lambda i, *_: (0, 0))] + mod_gate.specs(ga_chunk),
            out_specs=mod_gate.row_specs,
            scratch_shapes=[pltpu.VMEM((2, 2, tm, d), F32), pltpu.SemaphoreType.DMA((2, 2))]),
        out_shape=[jax.ShapeDtypeStruct((n_p, d), F32), jax.ShapeDtypeStruct((nbs, (n - n_p) // nbs * d), F32)],
        compiler_params=_cparams("arbitrary"),
        name="moe_combine_final_norm",
    )(pos, x, y_sorted, info, gain.reshape(1, d), *mod_gate.arrays)


def _route_tables(info, n_exp, tmg):
    n = info.shape[0]
    e = info[:, :2].astype(jnp.int32)
    onehot = (e[:, :, None] == jnp.arange(n_exp, dtype=jnp.int32)[None, None, :]).astype(jnp.int32).sum(axis=1)
    counts = onehot.sum(axis=0)
    padded = (counts + tmg - 1) // tmg * tmg
    ends = jnp.cumsum(padded)
    starts = ends - padded
    rank = jnp.cumsum(onehot, axis=0) - onehot
    pos = (starts[e] + jnp.take_along_axis(rank, e, axis=1)).reshape(-1).astype(jnp.int32)
    n_tiles = -(-(2 * n) // tmg) + n_exp
    src = (jnp.arange(n_tiles * tmg, dtype=jnp.int32) % n).at[pos].set(jnp.repeat(jnp.arange(n, dtype=jnp.int32), 2))
    n_valid = (ends[-1] // tmg).astype(jnp.int32)
    tiles = jnp.arange(n_tiles, dtype=jnp.int32)
    te = jnp.minimum((tiles[:, None] * tmg >= ends[None, :]).astype(jnp.int32).sum(axis=1), n_exp - 1)
    te = jnp.where(tiles < n_valid, te, te[jnp.maximum(n_valid - 1, 0)])
    return pos, src, te, n_valid.reshape(1)


def _gather_kernel(src_ref, nv_ref, h_hbm, o_ref, buf, sem):
    i = pl.program_id(0)
    tmg = buf.shape[1]
    nv = nv_ref[0]

    def issue(tile, slot):
        def body(r, carry):
            _row_copy(h_hbm, src_ref[tile * tmg + r], buf.at[slot], r, sem.at[slot]).start()
            return carry
        lax.fori_loop(0, tmg, body, 0, unroll=8)

    @pl.when((i == 0) & (nv > 0))
    def _():
        issue(0, 0)

    for slot in range(2):
        @pl.when((i + 1 < nv) & ((i + 1) % 2 == slot))
        def _():
            issue(i + 1, slot)

    for slot in range(2):
        @pl.when((i < nv) & (i % 2 == slot))
        def _():
            pltpu.make_async_copy(h_hbm.at[pl.ds(0, tmg)], buf.at[slot], sem.at[slot]).wait()
            o_ref[...] = buf[slot].astype(o_ref.dtype)

    @pl.when(i >= nv)
    def _():
        o_ref[...] = jnp.zeros_like(o_ref)


def _gather_rows(h, src, n_valid, tmg):
    d = h.shape[1]
    n_tiles = src.shape[0] // tmg
    return pl.pallas_call(
        _gather_kernel,
        grid_spec=pltpu.PrefetchScalarGridSpec(
            num_scalar_prefetch=2,
            grid=(n_tiles,),
            in_specs=[pl.BlockSpec(memory_space=pl.ANY)],
            out_specs=pl.BlockSpec((tmg, d), lambda i, src, nv: (i, 0)),
            scratch_shapes=[pltpu.VMEM((2, tmg, d), F32), pltpu.SemaphoreType.DMA((2,))]),
        out_shape=jax.ShapeDtypeStruct((n_tiles * tmg, d), BF16),
        compiler_params=_cparams("arbitrary"),
        name="moe_gather",
    )(src, n_valid, h)


def _mm_kernel(w_is_transposed, a_ref, w_ref, o_ref, wb_ref):
    @pl.when(pl.program_id(1) == 0)
    def _():
        wb_ref[...] = w_ref[...].astype(BF16)
    if w_is_transposed:
        acc = lax.dot_general(a_ref[...], wb_ref[...], (((1,), (1,)), ((), ())), preferred_element_type=F32)
    else:
        acc = jnp.dot(a_ref[...], wb_ref[...], preferred_element_type=F32)
    o_ref[...] = acc.astype(o_ref.dtype)


def _matmul(a, w, n_out, tn, tm=512, out_dtype=F32, name="matmul", w_is_transposed=False, col_block0=0):
    m, k = a.shape
    tm = _tile(m, tm)
    if w_is_transposed:
        w_spec = pl.BlockSpec((tn, k), lambda j, i: (col_block0 + j, 0))
        wb_shape = (tn, k)
    else:
        w_spec = pl.BlockSpec((k, tn), lambda j, i: (0, col_block0 + j))
        wb_shape = (k, tn)
    return pl.pallas_call(
        functools.partial(_mm_kernel, w_is_transposed),
        grid=(n_out // tn, m // tm),
        in_specs=[pl.BlockSpec((tm, k), lambda j, i: (i, 0)), w_spec],
        out_specs=pl.BlockSpec((tm, tn), lambda j, i: (i, j)),
        out_shape=jax.ShapeDtypeStruct((m, n_out), out_dtype),
        scratch_shapes=[pltpu.VMEM(wb_shape, BF16)],
        compiler_params=_cparams("parallel", "arbitrary"),
        name=name,
    )(a, w)


def _new_expert(te_ref):
    i = pl.program_id(1)
    return (i == 0) | (te_ref[i] != te_ref[jnp.maximum(i - 1, 0)])


def _swiglu_up_kernel(te_ref, nv_ref, a_ref, wg_ref, wu_ref, o_ref, wgb_ref, wub_ref):
    @pl.when(_new_expert(te_ref))
    def _():
        wgb_ref[...] = wg_ref[0].astype(BF16)
        wub_ref[...] = wu_ref[0].astype(BF16)

    @pl.when(pl.program_id(1) < nv_ref[0])
    def _():
        a = a_ref[...]
        gate = jnp.dot(a, wgb_ref[...], preferred_element_type=F32)
        up = jnp.dot(a, wub_ref[...], preferred_element_type=F32)
        o_ref[...] = (_silu(gate) * up).astype(o_ref.dtype)

    @pl.when(pl.program_id(1) >= nv_ref[0])
    def _():
        o_ref[...] = jnp.zeros_like(o_ref)


def _swiglu_up(a, w_up, te, nv, tm, tn, name):
    m, k = a.shape
    f = w_up.shape[2] // 2
    nj = f // tn
    row = lambda j, i, te, nv: jnp.minimum(i, nv[0] - 1)
    return pl.pallas_call(
        _swiglu_up_kernel,
        grid_spec=pltpu.PrefetchScalarGridSpec(
            num_scalar_prefetch=2,
            grid=(nj, m // tm),
            in_specs=[pl.BlockSpec((tm, k), lambda j, i, te, nv: (row(j, i, te, nv), 0)),
                      pl.BlockSpec((1, k, tn), lambda j, i, te, nv: (te[i], 0, j)),
                      pl.BlockSpec((1, k, tn), lambda j, i, te, nv: (te[i], 0, nj + j))],
            out_specs=pl.BlockSpec((tm, tn), lambda j, i, te, nv: (i, j)),
            scratch_shapes=[pltpu.VMEM((k, tn), BF16), pltpu.VMEM((k, tn), BF16)]),
        out_shape=jax.ShapeDtypeStruct((m, f), BF16),
        compiler_params=_cparams("parallel", "arbitrary"),
        name=name,
    )(te, nv, a, w_up, w_up)


def _down_kernel(te_ref, nv_ref, a_ref, w_ref, o_ref, wb_ref):
    @pl.when(_new_expert(te_ref))
    def _():
        wb_ref[...] = w_ref[0].astype(BF16)

    @pl.when(pl.program_id(1) < nv_ref[0])
    def _():
        o_ref[...] = jnp.dot(a_ref[...], wb_ref[...], preferred_element_type=F32)

    @pl.when(pl.program_id(1) >= nv_ref[0])
    def _():
        o_ref[...] = jnp.zeros_like(o_ref)


def _grouped_down(a, w_down, te, nv, tm, tn, name):
    m, f = a.shape
    d = w_down.shape[2]
    row = lambda j, i, te, nv: jnp.minimum(i, nv[0] - 1)
    return pl.pallas_call(
        _down_kernel,
        grid_spec=pltpu.PrefetchScalarGridSpec(
            num_scalar_prefetch=2,
            grid=(d // tn, m // tm),
            in_specs=[pl.BlockSpec((tm, f), lambda j, i, te, nv: (row(j, i, te, nv), 0)),
                      pl.BlockSpec((1, f, tn), lambda j, i, te, nv: (te[i], 0, j))],
            out_specs=pl.BlockSpec((tm, tn), lambda j, i, te, nv: (i, j)),
            scratch_shapes=[pltpu.VMEM((f, tn), BF16)]),
        out_shape=jax.ShapeDtypeStruct((m, d), F32),
        compiler_params=_cparams("parallel", "arbitrary"),
        name=name,
    )(te, nv, a, w_down)


def _shift_rows(carry, u, j):
    ext = jnp.concatenate([carry, u], axis=0)
    return pltpu.roll(ext, j, 0)[8:]


def _sconv_prompt_kernel(bg_ref, cg_ref, xi_ref, w_ref, o_ref, st_ref, carry_ref):
    @pl.when(pl.program_id(1) == 0)
    def _():
        carry_ref[...] = jnp.zeros_like(carry_ref)
    u = cg_ref[...] * xi_ref[...]
    carry = carry_ref[...]
    w = w_ref[...]
    y = w[2:3] * u + w[1:2] * _shift_rows(carry, u, 1) + w[0:1] * _shift_rows(carry, u, 2)
    o_ref[...] = (bg_ref[...] * y).astype(o_ref.dtype)
    tail = u[u.shape[0] - 8:]
    carry_ref[...] = tail
    st_ref[0] = tail


def _sconv_prompt(s_in, conv_w, n_seq, t_len, d, tt=256):
    tt = _tile(t_len, tt)
    nt = t_len // tt
    blk = lambda c: pl.BlockSpec((tt, d), lambda b, t: (b * nt + t, c))
    return pl.pallas_call(
        _sconv_prompt_kernel,
        grid=(n_seq, nt),
        in_specs=[blk(0), blk(1), blk(2), pl.BlockSpec((3, d), lambda b, t: (0, 0))],
        out_specs=[pl.BlockSpec((tt, d), lambda b, t: (b * nt + t, 0)),
                   pl.BlockSpec((1, 8, d), lambda b, t: (b, 0, 0))],
        out_shape=[jax.ShapeDtypeStruct((n_seq * t_len, d), BF16), jax.ShapeDtypeStruct((n_seq, 8, d), F32)],
        scratch_shapes=[pltpu.VMEM((8, d), F32)],
        compiler_params=_cparams("parallel", "arbitrary"),
        name="sconv_prompt",
    )(s_in, s_in, s_in, conv_w)


def _sconv_sample_kernel(nb, bg_ref, cg_ref, xi_ref, s0_ref, s1_ref, w_ref, o_ref, u_ref):
    u = cg_ref[...] * xi_ref[...]
    full = jnp.concatenate([s0_ref[...], s1_ref[...], u], axis=0)
    n = u.shape[0]
    w = w_ref[...]
    y = w[0:1] * full[0:n] + w[1:2] * full[nb:nb + n] + w[2:3] * full[2 * nb:2 * nb + n]
    o_ref[...] = (bg_ref[...] * y).astype(o_ref.dtype)
    u_ref[...] = u


def _sconv_sample(s_in, state2d, conv_w, row_block, nb, ns, d, ct=512):
    ct = _tile(d, ct)
    nj = d // ct
    blk = lambda c: pl.BlockSpec((ns, ct), lambda j: (row_block, c * nj + j))
    st = lambda s: pl.BlockSpec((nb, ct), lambda j: (0, s * nj + j))
    return pl.pallas_call(
        functools.partial(_sconv_sample_kernel, nb),
        grid=(nj,),
        in_specs=[blk(0), blk(1), blk(2), st(0), st(1), pl.BlockSpec((3, ct), lambda j: (0, j))],
        out_specs=[pl.BlockSpec((ns, ct), lambda j: (0, j)), pl.BlockSpec((ns, ct), lambda j: (0, j))],
        out_shape=[jax.ShapeDtypeStruct((ns, d), BF16), jax.ShapeDtypeStruct((ns, d), F32)],
        compiler_params=_cparams("parallel"),
        name="sconv_sample",
    )(s_in, s_in, s_in, state2d, state2d, conv_w)


def _l2norm_heads(x):
    parts = []
    for h in range(x.shape[1] // HEAD):
        seg = x[:, h * HEAD:(h + 1) * HEAD]
        parts.append(seg * lax.rsqrt(jnp.sum(seg * seg, axis=-1, keepdims=True) + NORM_EPS))
    return jnp.concatenate(parts, axis=1) if len(parts) > 1 else parts[0]


def _store_qkv(o_ref, act, is_qk):
    @pl.when(is_qk)
    def _():
        o_ref[...] = _l2norm_heads(act)

    @pl.when(jnp.logical_not(is_qk))
    def _():
        o_ref[...] = act


def _gconv_sample_kernel(n_qk_tiles, nb, s0_ref, s1_ref, s2_ref, x_ref, w_ref, o_ref):
    u = x_ref[...]
    full = jnp.concatenate([s0_ref[0], s1_ref[0], s2_ref[0], u], axis=0)
    n = u.shape[0]
    w = w_ref[...]
    y = (w[0:1] * full[0:n] + w[1:2] * full[nb:nb + n] + w[2:3] * full[2 * nb:2 * nb + n]
         + w[3:4] * full[3 * nb:3 * nb + n])
    _store_qkv(o_ref, _silu(y), pl.program_id(0) < n_qk_tiles)


def _gconv_sample(proj, state_tm, conv_w, row_block, nb, ns, d, ct=512):
    ch = 4 * d
    ct = _tile(ch, ct)
    nj = ch // ct
    st = lambda s: pl.BlockSpec((1, nb, ct), lambda j: (s, 0, j))
    return pl.pallas_call(
        functools.partial(_gconv_sample_kernel, 2 * d // ct, nb),
        grid=(nj,),
        in_specs=[st(0), st(1), st(2), pl.BlockSpec((ns, ct), lambda j: (row_block, j)),
                  pl.BlockSpec((4, ct), lambda j: (0, j))],
        out_specs=pl.BlockSpec((ns, ct), lambda j: (0, j)),
        out_shape=jax.ShapeDtypeStruct((ns, ch), F32),
        compiler_params=_cparams("parallel"),
        name="gdn_conv_sample",
    )(state_tm, state_tm, state_tm, proj, conv_w)


def _gates_kernel(chunk, ba_ref, alog_ref, dtb_ref, beta_ref, gc_ref, gtot_ref, gct_ref):
    ba = ba_ref[...]
    nh = ba.shape[1] // 2
    r = ba.shape[0]
    hps = beta_ref.shape[2]

    def store_by_group(ref, x):
        for grp in range(nh // hps):
            ref[grp] = x[:, grp * hps:(grp + 1) * hps]
    store_by_group(beta_ref, jax.nn.sigmoid(ba[:, :nh]))
    g = -jnp.exp(alog_ref[...]) * _softplus(ba[:, nh:] + dtb_ref[...])
    ri = lax.broadcasted_iota(jnp.int32, (r, r), 0)
    ci = lax.broadcasted_iota(jnp.int32, (r, r), 1)
    same = _blk(ri, chunk) == _blk(ci, chunk)
    tri = jnp.where(same & (ri >= ci), 1.0, 0.0)
    blk = jnp.where(same, 1.0, 0.0)
    gc = jnp.dot(tri, g, precision=HIGHEST, preferred_element_type=F32)
    store_by_group(gc_ref, gc)
    store_by_group(gtot_ref, jnp.dot(blk, g, precision=HIGHEST, preferred_element_type=F32))
    eye = jnp.where(lax.broadcasted_iota(jnp.int32, (nh, nh), 0) == lax.broadcasted_iota(jnp.int32, (nh, nh), 1),
                    1.0, 0.0)
    gct_ref[0] = lax.dot_general(eye, gc, (((1,), (1,)), ((), ())), precision=HIGHEST,
                                 preferred_element_type=F32)


def _gates(ba, a_log, dt_bias, chunk, r, hps):
    n, h2 = ba.shape
    nh = h2 // 2
    row = pl.BlockSpec((nh // hps, r, hps), lambda i: (0, i, 0))
    vec = pl.BlockSpec((1, nh), lambda i: (0, 0))
    return pl.pallas_call(
        functools.partial(_gates_kernel, chunk),
        grid=(n // r,),
        in_specs=[pl.BlockSpec((r, h2), lambda i: (i, 0)), vec, vec],
        out_specs=[row, row, row, pl.BlockSpec((1, nh, r), lambda i: (i, 0, 0))],
        out_shape=[jax.ShapeDtypeStruct((nh // hps, n, hps), F32)] * 3 + [jax.ShapeDtypeStruct((n // r, nh, r), F32)],
        compiler_params=_cparams("parallel"),
        name="gdn_gates",
    )(ba, a_log.reshape(1, nh), dt_bias.reshape(1, nh))


def _chunk_terms(chunk, q, k, v, beta, gc, gtot, gc_row):
    r = q.shape[0]
    ri = lax.broadcasted_iota(jnp.int32, (r, r), 0)
    ci = lax.broadcasted_iota(jnp.int32, (r, r), 1)
    causal = (_blk(ri, chunk) == _blk(ci, chunk)) & (ri >= ci)
    decay = jnp.exp(jnp.where(causal, gc - gc_row, -jnp.inf))
    kb = k * beta
    low = jnp.where(ri > ci, _dot_nt(kb, k) * decay, 0.0)
    e = -jnp.where(_blk(ri, 2) == _blk(ci, 2), low, 0.0)
    s = 2
    while s < chunk:
        lm = jnp.where((_blk(ri, 2 * s) == _blk(ci, 2 * s)) & (_blk(ri, s) != _blk(ci, s)), low, 0.0)
        x = lm + _dot(e, lm)
        e = e - (x + _dot(x, e))
        s *= 2
    eg = jnp.exp(gc)
    rhs = jnp.concatenate([v * beta, kb * eg], axis=1)
    uw = rhs + _dot(e, rhs)
    dv = v.shape[1]
    qs = q * (HEAD ** -0.5)
    a_intra = _dot_nt(qs, k) * decay
    return uw[:, :dv], uw[:, dv:], a_intra, qs * eg, k * jnp.exp(gtot - gc)


def _paired_chunk_terms(chunk, heads):
    r, pair = heads[0][0].shape[0], 2 * chunk
    ii = lax.broadcasted_iota(jnp.int32, (chunk, pair), 0)
    lane = lax.broadcasted_iota(jnp.int32, (chunk, pair), 1)
    right = lane >= chunk
    jj = lane & (chunk - 1)
    diag = lambda full: jnp.where(right, full[chunk:], full[:chunk])
    spread = lambda y: jnp.concatenate([jnp.where(right, 0.0, y), jnp.where(right, y, 0.0)], axis=0)
    slices = [slice(p * pair, (p + 1) * pair) for p in range(r // pair)]
    kbs = [k * beta for (q, k, v, beta, gc, gtot, gc_row) in heads]
    egs = [jnp.exp(gc) for (q, k, v, beta, gc, gtot, gc_row) in heads]
    qss = [q * (HEAD ** -0.5) for (q, k, v, beta, gc, gtot, gc_row) in heads]
    rhss = [jnp.concatenate([h[2] * h[3], kb * eg], axis=1) for h, kb, eg in zip(heads, kbs, egs)]
    probs = [(i, sl) for i in range(len(heads)) for sl in slices]
    decays = []
    for i, sl in probs:
        gcp, gc_row = heads[i][4][sl], heads[i][6]
        decays.append(jnp.exp(jnp.where(ii >= jj, jnp.where(right, gcp[chunk:], gcp[:chunk]) - gc_row[:, sl],
                                        -jnp.inf)))
    kks = [_dot_nt(kbs[i][sl], heads[i][1][sl]) for i, sl in probs]
    lows = [jnp.where(ii > jj, diag(kk) * decay, 0.0) for kk, decay in zip(kks, decays)]
    es = [-jnp.where(_blk(ii, 2) == _blk(jj, 2), low, 0.0) for low in lows]
    s = 2
    while s < chunk:
        level = (_blk(ii, 2 * s) == _blk(jj, 2 * s)) & (_blk(ii, s) != _blk(jj, s))
        lms = [jnp.where(level, low, 0.0) for low in lows]
        lm_sp = [spread(lm).astype(BF16) for lm in lms]
        xs = [lm + _dot(e, sp) for lm, e, sp in zip(lms, es, lm_sp)]
        e_sp = [spread(e).astype(BF16) for e in es]
        es = [e - (x + _dot(x, sp)) for e, x, sp in zip(es, xs, e_sp)]
        s *= 2
    e_sp = [spread(e).astype(BF16) for e in es]
    uws = [rhss[i][sl] + _dot(sp, rhss[i][sl]) for (i, sl), sp in zip(probs, e_sp)]
    qks = [_dot_nt(qss[i][sl], heads[i][1][sl]) for i, sl in probs]
    atts = [diag(qk) * decay for qk, decay in zip(qks, decays)]
    out = []
    n_pairs = len(slices)
    for i, (q, k, v, beta, gc, gtot, gc_row) in enumerate(heads):
        uw = jnp.concatenate(uws[i * n_pairs:(i + 1) * n_pairs], axis=0)
        a_chunks = []
        for a in atts[i * n_pairs:(i + 1) * n_pairs]:
            a_chunks += [jnp.where(right, 0.0, a), jnp.where(right, a, 0.0)]
        dv = v.shape[1]
        out.append((uw[:, :dv], uw[:, dv:], a_chunks, qss[i] * egs[i], k * jnp.exp(gtot - gc)))
    return out


def _gated_norm(o, z, gain):
    return (_rms(o, gain) * _silu(z)).astype(BF16)


def _conv_act_head(x_ref, w_ref, carry_ref, h, l2norm):
    sl = slice(h * HEAD, (h + 1) * HEAD)
    u, w, carry = x_ref[:, sl], w_ref[:, sl], carry_ref[:, sl]
    y = (w[3:4] * u + w[2:3] * _shift_rows(carry, u, 1) + w[1:2] * _shift_rows(carry, u, 2)
         + w[0:1] * _shift_rows(carry, u, 3))
    a = _silu(y)
    if l2norm:
        a = a * lax.rsqrt(jnp.sum(a * a, axis=-1, keepdims=True) + NORM_EPS)
    return a


def _gdn_prompt_kernel(chunk, hps, q_ref, k_ref, v_ref, z_ref, wq_ref, wk_ref, wv_ref, beta_ref, gc_ref, gtot_ref,
                       gct_ref, gn_ref, o_ref, s_ref, s_scr, cq_scr, ck_scr, cv_scr):
    @pl.when(pl.program_id(2) == 0)
    def _():
        s_scr[...] = jnp.zeros_like(s_scr)
        cq_scr[...] = jnp.zeros_like(cq_scr)
        ck_scr[...] = jnp.zeros_like(ck_scr)
        cv_scr[...] = jnp.zeros_like(cv_scr)
    r = q_ref.shape[0]
    h0 = pl.program_id(1) * hps
    paired = 2 * chunk == HEAD and r % HEAD == 0
    qs = [_conv_act_head(q_ref, wq_ref, cq_scr, kh, True) for kh in range(hps // 2)]
    ks = [_conv_act_head(k_ref, wk_ref, ck_scr, kh, True) for kh in range(hps // 2)]
    vs = [_conv_act_head(v_ref, wv_ref, cv_scr, hl, False) for hl in range(hps)]
    cq_scr[...] = q_ref[r - 8:r, :]
    ck_scr[...] = k_ref[r - 8:r, :]
    cv_scr[...] = v_ref[r - 8:r, :]
    heads = []
    for hl in range(hps):
        heads.append((qs[hl // 2], ks[hl // 2], vs[hl], beta_ref[0][:, hl:hl + 1],
                      gc_ref[0][:, hl:hl + 1], gtot_ref[0][:, hl:hl + 1],
                      gct_ref[0, pl.ds(h0 + hl, 1), :]))
    if paired:
        terms = _paired_chunk_terms(chunk, heads)
    else:
        terms = [_chunk_terms(chunk, *head) for head in heads]
    terms = [t + (jnp.exp(head[5]),) for t, head in zip(terms, heads)]
    states = [s_scr[hl] for hl in range(hps)]
    outs = [[] for _ in range(hps)]
    for c in range(r // chunk):
        lo, hi = c * chunk, (c + 1) * chunk
        for hl in range(hps):
            u, w, a, qd, kd, egt = terms[hl]
            s = states[hl]
            pq = _dot(jnp.concatenate([w[lo:hi], qd[lo:hi]], axis=0), s)
            v_new = u[lo:hi] - pq[:chunk]
            if paired:
                intra = _dot(a[c], jnp.concatenate([v_new, v_new], axis=0))
            else:
                intra = _dot(a[lo:hi, lo:hi], v_new)
            outs[hl].append(pq[chunk:] + intra)
            states[hl] = s * egt[lo:lo + 1] + _dot_tn(kd[lo:hi], v_new)
    for hl in range(hps):
        s_scr[hl] = states[hl]
        o = jnp.concatenate(outs[hl], axis=0)
        o_ref[:, hl * HEAD:(hl + 1) * HEAD] = _gated_norm(o, z_ref[:, hl * HEAD:(hl + 1) * HEAD], gn_ref[...])
    s_ref[0] = s_scr[...]


def _gdn_prompt(proj, conv_w, beta, gc, gtot, gct, g_onorm, n_seq, t_len, d, chunk, r=256, hps=8):
    nvh = 2 * d // HEAD
    ns = t_len // r
    kc = hps // 2 * HEAD
    vc = hps * HEAD
    rows = lambda b, g, s: b * ns + s
    return pl.pallas_call(
        functools.partial(_gdn_prompt_kernel, chunk, hps),
        grid=(n_seq, nvh // hps, ns),
        in_specs=[pl.BlockSpec((r, kc), lambda b, g, s: (rows(b, g, s), g)),
                  pl.BlockSpec((r, kc), lambda b, g, s: (rows(b, g, s), d // kc + g)),
                  pl.BlockSpec((r, vc), lambda b, g, s: (rows(b, g, s), 2 * d // vc + g)),
                  pl.BlockSpec((r, vc), lambda b, g, s: (rows(b, g, s), 4 * d // vc + g)),
                  pl.BlockSpec((4, kc), lambda b, g, s: (0, g)),
                  pl.BlockSpec((4, kc), lambda b, g, s: (0, d // kc + g)),
                  pl.BlockSpec((4, vc), lambda b, g, s: (0, 2 * d // vc + g)),
                  pl.BlockSpec((1, r, hps), lambda b, g, s: (g, rows(b, g, s), 0)),
                  pl.BlockSpec((1, r, hps), lambda b, g, s: (g, rows(b, g, s), 0)),
                  pl.BlockSpec((1, r, hps), lambda b, g, s: (g, rows(b, g, s), 0)),
                  pl.BlockSpec((1, nvh, r), lambda b, g, s: (rows(b, g, s), 0, 0)),
                  pl.BlockSpec((1, HEAD), lambda b, g, s: (0, 0))],
        out_specs=[pl.BlockSpec((r, vc), lambda b, g, s: (rows(b, g, s), g)),
                   pl.BlockSpec((1, hps, HEAD, HEAD), lambda b, g, s: (b, g, 0, 0))],
        out_shape=[jax.ShapeDtypeStruct((n_seq * t_len, 2 * d), BF16),
                   jax.ShapeDtypeStruct((n_seq, nvh, HEAD, HEAD), F32)],
        scratch_shapes=[pltpu.VMEM((hps, HEAD, HEAD), F32), pltpu.VMEM((8, kc), F32), pltpu.VMEM((8, kc), F32),
                        pltpu.VMEM((8, vc), F32)],
        compiler_params=_cparams("parallel", "parallel", "arbitrary"),
        name="gdn_prompt",
    )(proj, proj, proj, proj, conv_w, conv_w, conv_w, beta, gc, gtot, gct, g_onorm.reshape(1, HEAD))


def _gdn_sample_kernel(chunk, hps, q_ref, k_ref, v_ref, z_ref, beta_ref, gc_ref, gtot_ref, gct_ref, gn_ref, s0_ref,
                       o_ref, s_ref, wq_scr, kd_scr, eg_scr, pq_scr):
    r = q_ref.shape[0]
    h0 = pl.program_id(0) * hps
    per_row = 8 // chunk
    terms = []
    for hl in range(hps):
        kh = hl // 2
        q = q_ref[:, kh * HEAD:(kh + 1) * HEAD]
        k = k_ref[:, kh * HEAD:(kh + 1) * HEAD]
        v = v_ref[:, hl * HEAD:(hl + 1) * HEAD]
        beta = beta_ref[0][:, hl:hl + 1]
        gc = gc_ref[0][:, hl:hl + 1]
        gtot = gtot_ref[0][:, hl:hl + 1]
        gc_row = gct_ref[0, pl.ds(h0 + hl, 1), :]
        u, w, a, qd, kd = _chunk_terms(chunk, q, k, v, beta, gc, gtot, gc_row)
        wq_scr[hl, :, 0:8, :] = w.reshape(r // 8, 8, HEAD)
        wq_scr[hl, :, 8:16, :] = qd.reshape(r // 8, 8, HEAD)
        kd_scr[hl] = kd
        eg_scr[hl] = jnp.broadcast_to(jnp.exp(gtot), (r, HEAD))
        terms.append((u, a))

    row16 = lax.broadcasted_iota(jnp.int32, (16, HEAD), 0)
    row8 = lax.broadcasted_iota(jnp.int32, (8, HEAD), 0)

    def pair_pred(p, carry):
        for hl in range(hps):
            lhs = wq_scr[hl, p]
            acc = jnp.zeros((16, HEAD), F32)
            for e in range(per_row):
                pq = _dot(lhs, s0_ref[p * per_row + e, hl])
                acc = jnp.where(_blk(row16 & 7, chunk) == e, pq, acc)
            pq_scr[hl, p] = acc
        return carry
    lax.fori_loop(0, r // 8, pair_pred, 0)

    v_news = []
    for hl in range(hps):
        u, a = terms[hl]
        pq = pq_scr[hl]
        v_new = u - pq[:, 0:8, :].reshape(r, HEAD)
        o = pq[:, 8:16, :].reshape(r, HEAD) + _dot(a, v_new)
        o_ref[:, hl * HEAD:(hl + 1) * HEAD] = _gated_norm(o, z_ref[:, hl * HEAD:(hl + 1) * HEAD], gn_ref[...])
        pq_scr[hl, :, 0:8, :] = v_new.reshape(r // 8, 8, HEAD)

    def pair_update(p, carry):
        for hl in range(hps):
            v_new = pq_scr[hl, p, 0:8, :]
            kd = kd_scr[hl, pl.ds(pl.multiple_of(p * 8, 8), 8), :]
            for e in range(per_row):
                b = p * per_row + e
                decay = eg_scr[hl, pl.ds(p * 8 + e * chunk, 1), :]
                kd_e = jnp.where(_blk(row8, chunk) == e, kd, 0.0)
                s_ref[b, hl] = s0_ref[b, hl] * decay + _dot_tn(kd_e, v_new)
        return carry
    lax.fori_loop(0, r // 8, pair_update, 0)


def _gdn_sample(qkv, z, beta, gc, gtot, gct, g_onorm, state, d, chunk, r=128, hps=4):
    n = qkv.shape[0]
    nvh = 2 * d // HEAD
    kc = hps // 2 * HEAD
    vc = hps * HEAD
    nb = r // chunk
    return pl.pallas_call(
        functools.partial(_gdn_sample_kernel, chunk, hps),
        grid=(nvh // hps, n // r),
        in_specs=[pl.BlockSpec((r, kc), lambda g, s: (s, g)),
                  pl.BlockSpec((r, kc), lambda g, s: (s, d // kc + g)),
                  pl.BlockSpec((r, vc), lambda g, s: (s, 2 * d // vc + g)),
                  pl.BlockSpec((r, vc), lambda g, s: (s, g)),
                  pl.BlockSpec((1, r, hps), lambda g, s: (g, s, 0)),
                  pl.BlockSpec((1, r, hps), lambda g, s: (g, s, 0)),
                  pl.BlockSpec((1, r, hps), lambda g, s: (g, s, 0)),
                  pl.BlockSpec((1, nvh, r), lambda g, s: (s, 0, 0)),
                  pl.BlockSpec((1, HEAD), lambda g, s: (0, 0)),
                  pl.BlockSpec((nb, hps, HEAD, HEAD), lambda g, s: (s, g, 0, 0))],
        out_specs=[pl.BlockSpec((r, vc), lambda g, s: (s, g)),
                   pl.BlockSpec((nb, hps, HEAD, HEAD), lambda g, s: (s, g, 0, 0))],
        out_shape=[jax.ShapeDtypeStruct((n, 2 * d), BF16),
                   jax.ShapeDtypeStruct(state.shape, F32)],
        scratch_shapes=[pltpu.VMEM((hps, r // 8, 16, HEAD), F32), pltpu.VMEM((hps, r, HEAD), F32),
                        pltpu.VMEM((hps, r, HEAD), F32), pltpu.VMEM((hps, r // 8, 16, HEAD), F32)],
        compiler_params=_cparams("parallel", "parallel"),
        name="gdn_sample",
    )(qkv, qkv, qkv, z, beta, gc, gtot, gct, g_onorm.reshape(1, HEAD), state)


def kernel(x_prompt, x_sample, c_prompt, c_sample, state_gdn, state_gdn_conv, state_sconv, w_ada, b_ada,
           g_norm_mix, g_norm_ffn, g_norm_out, gdn_w_in, gdn_conv_w, gdn_a_log, gdn_dt_bias, gdn_g_onorm,
           gdn_w_out, sc_w_in, sc_conv_w, sc_w_out, ffn_w_up, ffn_w_down, moe_w_router, moe_b_router,
           moe_w_up, moe_w_down):
    nbp, t_len, d = x_prompt.shape
    nbs, steps, _ = x_sample.shape
    n_p, n_s = nbp * t_len, nbs * steps
    nvh = 2 * d // HEAD
    chunk_p = min(64, t_len)
    tm = nbs
    assert t_len % tm == 0 and n_p % n_s == 0 and 8 % steps == 0

    n = n_p + n_s
    tm_mm = _tile(n, 1088)
    x_in = (x_prompt.reshape(n_p, d), x_sample.reshape(nbs, steps * d))
    pad = (-(nbp + nbs)) % 8
    c_all = jnp.concatenate([c_prompt, c_sample, jnp.zeros((pad, d), F32)], axis=0)
    m = _ada(c_all, w_ada, b_ada)
    mods = [_Mod(m[i, :nbp].reshape(nbp, 1, 6 * d), m[i, nbp:nbp + nbs], d, tm, n_p // tm, t_len // tm)
            for i in range(2)]
    SH1, SC1, GA1, SH2, SC2, GA2 = range(6)

    h = _prep(x_in, g_norm_mix[0], mods[0], SC1, SH1, n, tm)
    w_in_t = jnp.swapaxes(gdn_w_in[0], 0, 1)
    proj = _matmul(h, w_in_t, 6 * d, _tile(6 * d, 1024), tm_mm, name="gdn_in_proj", w_is_transposed=True)
    ba = _matmul(h, w_in_t, 2 * nvh, 2 * nvh, tm_mm, name="gdn_in_proj_ba", w_is_transposed=True,
                 col_block0=6 * d // (2 * nvh))
    conv_w = gdn_conv_w[0]
    st_gconv = jnp.transpose(state_gdn_conv[0], (1, 0, 2))
    qkv_s = _gconv_sample(proj, st_gconv, conv_w, n_p // n_s, nbs, n_s, d)
    to_seq_major = lambda a: a.reshape(steps, nbs, a.shape[-1]).transpose(1, 0, 2).reshape(n_s, a.shape[-1])
    qkv_s = to_seq_major(qkv_s)
    z_s = to_seq_major(proj[n_p:, 4 * d:6 * d])
    ba_s = to_seq_major(ba[n_p:])
    hps_p, hps_s = min(8, nvh), min(4, nvh)
    beta_p, gc_p, gtot_p, gct_p = _gates(ba[:n_p], gdn_a_log[0], gdn_dt_bias[0], chunk_p, 256, hps_p)
    beta_s, gc_s, gtot_s, gct_s = _gates(ba_s, gdn_a_log[0], gdn_dt_bias[0], steps, 128, hps_s)
    o_p, gdn_p = _gdn_prompt(proj, conv_w, beta_p, gc_p, gtot_p, gct_p, gdn_g_onorm[0], nbp, t_len, d, chunk_p,
                             hps=hps_p)
    o_s, gdn_s = _gdn_sample(qkv_s, z_s, beta_s, gc_s, gtot_s, gct_s, gdn_g_onorm[0], state_gdn[0], d, steps,
                             hps=hps_s)
    tm_p = _tile(n_p, 1024)
    y = (_matmul(o_p, gdn_w_out[0], d, _tile(d, 512), tm_p, name="gdn_out_proj"),
         _matmul(o_s, gdn_w_out[0], d, _tile(d, 512), n_s, name="gdn_out_proj_sample").reshape(nbs, steps * d))
    gconv_p = jnp.stack([proj[(b + 1) * t_len - 3:(b + 1) * t_len, :4 * d] for b in range(nbp)])
    gconv_s = proj[n_p + nbs:, :4 * d].reshape(3, nbs, 4 * d).transpose(1, 0, 2)

    x, h = _resid_prep(x_in, y, g_norm_ffn[0], mods[0], GA1, mods[0], SC2, SH2, n, tm)
    one_expert = jnp.zeros((n // tm_mm,), jnp.int32)
    all_tiles = jnp.full((1,), n // tm_mm, jnp.int32)
    act = _swiglu_up(h, ffn_w_up, one_expert, all_tiles, tm_mm, _tile(ffn_w_down.shape[1], 512), "ffn_swiglu_up")
    f = _matmul(act, ffn_w_down[0], d, _tile(d, 512), name="ffn_down")

    x, h = _resid_prep(x, f, g_norm_mix[1], mods[0], GA2, mods[1], SC1, SH1, n, tm)
    s_in = _matmul(h, sc_w_in[0], 3 * d, _tile(3 * d, 1024), tm_mm, name="sconv_in_proj")
    mix_p, sconv_tail = _sconv_prompt(s_in, sc_conv_w[0], nbp, t_len, d)
    mix_s, u_s = _sconv_sample(s_in, state_sconv[0].reshape(nbs, 2 * d), sc_conv_w[0], n_p // n_s, nbs, n_s, d)
    y_s = _matmul(mix_s, sc_w_out[0], d, _tile(d, 1024), n_s, name="sconv_out_proj_sample")
    y = (_matmul(mix_p, sc_w_out[0], d, _tile(d, 1024), tm_p, name="sconv_out_proj"),
         y_s.reshape(steps, nbs, d).transpose(1, 0, 2).reshape(nbs, steps * d))
    sconv_p = sconv_tail[:, 6:8]
    sconv_s = u_s[n_s - 2 * nbs:].reshape(2, nbs, d).transpose(1, 0, 2)

    x, h, info = _resid_route(x, y, g_norm_ffn[1], mods[1], GA1, mods[1], SC2, SH2,
                              moe_w_router[0], moe_b_router[0], tm)
    _, n_exp, f_moe, _ = moe_w_down.shape
    tmg = 512
    pos, src, te, nv = _route_tables(info, n_exp, tmg)
    tg = _tile(tmg, 256)
    h_sorted = _gather_rows(h, src, nv * (tmg // tg), tg)
    act = _swiglu_up(h_sorted, moe_w_up[0], te, nv, tmg, _tile(f_moe, 1024), "moe_swiglu_up")
    y_sorted = _grouped_down(act, moe_w_down[0], te, nv, tmg, _tile(d, 512), "moe_down")
    out_p, out_s = _final(x, y_sorted, pos, info, g_norm_out, mods[1], GA2, n_p, nbs, tm)

    y_prompt = out_p.reshape(nbp, t_len, d)
    y_sample = out_s.reshape(nbs, steps, d)
    return (y_prompt, y_sample, gdn_p[None], gconv_p[None], sconv_p[None], gdn_s[None], gconv_s[None], sconv_s[None])
```
